```python
import jax, jax.numpy as jnp
from jax import lax
import numpy as np

D_MODEL = 1024
BATCH = 2
SEQ = 8192
DEPTH = 1

CHUNK = 64
N_MEM = 256
HGRN_HEADS = 4
HGRN_DK = 128
HGRN_DV = 128
HGRN_QK = HGRN_HEADS * HGRN_DK
HGRN_WIDTH = HGRN_HEADS * HGRN_DV
SB_HEADS = 8
SB_HEAD_DIM = 64
SB_WIDTH = SB_HEADS * SB_HEAD_DIM
SB_BLOCK = 128
XATTN_HEADS = 4
XATTN_HEAD_DIM = D_MODEL // XATTN_HEADS
D_FF = ((8 * D_MODEL // 3 + 255) // 256) * 256
EPS = 1e-6
IN_SIZES = (HGRN_QK, HGRN_QK, HGRN_WIDTH, HGRN_WIDTH, SB_WIDTH, SB_WIDTH, SB_WIDTH)
IN_OFFSETS = tuple(int(v) for v in np.cumsum(IN_SIZES)[:-1])
D_IN = int(sum(IN_SIZES))

kernel_name = 'hybrid_hgrn2_stickbreaking_xattn_block'


def rms_norm(x, gain):
    xf = x.astype(jnp.float32)
    y = xf * lax.rsqrt(jnp.mean(xf * xf, axis=-1, keepdims=True) + EPS)
    return (y * gain.astype(jnp.float32)).astype(x.dtype)


def hgrn2_mixer(q, f_pre, i, g, lb, norm_gain):
    B, T, _ = q.shape
    nc = T // CHUNK
    f32 = jnp.float32
    lbf = lb.astype(f32)
    q = jax.nn.silu(q.astype(f32))
    f = lbf + (1.0 - lbf) * jax.nn.sigmoid(f_pre.astype(f32))
    k = 1.0 - f
    log_f = jnp.log(f)

    def to_chunks(t, d):
        return t.reshape(B, nc, CHUNK, HGRN_HEADS, d).transpose(1, 0, 3, 2, 4)

    qc = to_chunks(q, HGRN_DK)
    kc = to_chunks(k, HGRN_DK)
    lfc = to_chunks(log_f, HGRN_DK)
    vc = to_chunks(i.astype(f32), HGRN_DV)
    tri = jnp.tril(jnp.ones((CHUNK, CHUNK), dtype=bool))[:, :, None]

    def step(S, xs):
        q_c, k_c, lf_c, v_c = xs
        b = jnp.cumsum(lf_c, axis=-2)
        diff = b[:, :, :, None, :] - b[:, :, None, :, :]
        decay = jnp.where(tri, jnp.exp(jnp.where(tri, diff, 0.0)), 0.0)
        scores = jnp.einsum('bhtd,bhsd,bhtsd->bhts', q_c, k_c, decay)
        o = (jnp.einsum('bhts,bhsv->bhtv', scores, v_c)
             + jnp.einsum('bhtd,bhdv->bhtv', q_c * jnp.exp(b), S))
        b_last = b[:, :, -1:, :]
        S = (jnp.exp(b_last[:, :, 0, :])[..., None] * S
             + jnp.einsum('bhsd,bhsv->bhdv', k_c * jnp.exp(b_last - b), v_c))
        return S, o

    S0 = jnp.zeros((B, HGRN_HEADS, HGRN_DK, HGRN_DV), f32)
    _, o = lax.scan(step, S0, (qc, kc, lfc, vc))
    o = o.transpose(1, 0, 3, 2, 4).reshape(B, T, HGRN_HEADS, HGRN_DV)
    o = rms_norm(o, norm_gain.reshape(HGRN_HEADS, HGRN_DV)).reshape(B, T, HGRN_WIDTH)
    return (o * jax.nn.silu(g.astype(f32))).astype(g.dtype)


def stick_breaking_mixer(q, k, v):
    B, T, _ = q.shape
    f32 = jnp.float32
    nq = T // SB_BLOCK
    scale = SB_HEAD_DIM ** -0.5
    qh = q.astype(f32).reshape(B, T, SB_HEADS, SB_HEAD_DIM).transpose(0, 2, 1, 3)
    kh = k.astype(f32).reshape(B, T, SB_HEADS, SB_HEAD_DIM).transpose(0, 2, 1, 3)
    vh = v.astype(f32).reshape(B, T, SB_HEADS, SB_HEAD_DIM).transpose(0, 2, 1, 3)
    qb = qh.reshape(B, SB_HEADS, nq, SB_BLOCK, SB_HEAD_DIM).transpose(2, 0, 1, 3, 4)
    key_pos = jnp.arange(T)

    def block(args):
        q_blk, blk = args
        z = jnp.einsum('bhqd,bhkd->bhqk', q_blk, kh) * scale
        q_pos = blk * SB_BLOCK + jnp.arange(SB_BLOCK)
        causal = key_pos[None, :] < q_pos[:, None]
        log_rest = jnp.where(causal, jax.nn.log_sigmoid(-z), 0.0)
        between = lax.cumsum(log_rest, axis=3, reverse=True) - log_rest
        log_a = jax.nn.log_sigmoid(z) + between
        a = jnp.where(causal, jnp.exp(log_a), 0.0)
        return jnp.einsum('bhqk,bhkd->bhqd', a, vh)

    o = lax.map(block, (qb, jnp.arange(nq)))
    o = o.transpose(1, 0, 3, 2, 4).reshape(B, T, SB_WIDTH)
    return o.astype(q.dtype)


def cross_attention(h, m, w_q, w_kv, w_o):
    B, T, _ = h.shape
    f32 = jnp.float32
    q = (h @ w_q).reshape(B, T, XATTN_HEADS, XATTN_HEAD_DIM)
    k, v = jnp.split(m @ w_kv, 2, axis=-1)
    k = k.reshape(B, N_MEM, XATTN_HEADS, XATTN_HEAD_DIM)
    v = v.reshape(B, N_MEM, XATTN_HEADS, XATTN_HEAD_DIM)
    s = jnp.einsum('bthd,bnhd->bhtn', q.astype(f32), k.astype(f32)) * (XATTN_HEAD_DIM ** -0.5)
    p = jax.nn.softmax(s, axis=-1)
    o = jnp.einsum('bhtn,bnhd->bthd', p, v.astype(f32)).reshape(B, T, D_MODEL)
    return o.astype(h.dtype) @ w_o


def swiglu(h, w_in, w_out):
    gate, up = jnp.split(h @ w_in, 2, axis=-1)
    return (jax.nn.silu(gate) * up) @ w_out


def setup_inputs(seed: int = 0) -> dict:
    key = jax.random.key(seed)
    ks = jax.random.split(key, 20)

    def w(k, shape, fan_in):
        return jax.random.normal(k, shape, jnp.float32) * fan_in ** -0.5

    def gain(k, shape):
        return 1.0 + 0.02 * jax.random.normal(k, shape, jnp.float32)

    return dict(
        x=jax.random.normal(ks[0], (BATCH, SEQ, D_MODEL), jnp.float32),
        mem=jax.random.normal(ks[1], (BATCH, N_MEM, D_MODEL), jnp.float32),
        g_mix=gain(ks[2], (DEPTH, D_MODEL)),
        w_in=w(ks[3], (DEPTH, D_MODEL, D_IN), D_MODEL),
        lb_table=0.5 * jax.random.normal(ks[4], (DEPTH + 1, HGRN_QK), jnp.float32),
        g_hgrn=gain(ks[5], (DEPTH, HGRN_WIDTH)),
        w_gate=w(ks[6], (DEPTH, D_MODEL, 2 * D_MODEL), D_MODEL),
        w_proj_a=w(ks[7], (DEPTH, HGRN_WIDTH, D_MODEL), HGRN_WIDTH),
        w_proj_b=w(ks[8], (DEPTH, SB_WIDTH, D_MODEL), SB_WIDTH),
        w_out=w(ks[9], (DEPTH, D_MODEL, D_MODEL), D_MODEL),
        g_xattn=gain(ks[10], (DEPTH, D_MODEL)),
        g_mem=gain(ks[11], (DEPTH, D_MODEL)),
        w_xq=w(ks[12], (DEPTH, D_MODEL, D_MODEL), D_MODEL),
        w_xkv=w(ks[13], (DEPTH, D_MODEL, 2 * D_MODEL), D_MODEL),
        w_xo=w(ks[14], (DEPTH, D_MODEL, D_MODEL), D_MODEL),
        g_ffn=gain(ks[15], (DEPTH, D_MODEL)),
        w_ffn_in=w(ks[16], (DEPTH, D_MODEL, 2 * D_FF), D_MODEL),
        w_ffn_out=w(ks[17], (DEPTH, D_FF, D_MODEL), D_FF),
        g_final=gain(ks[18], (D_MODEL,)),
    )


def reference(x, mem, g_mix, w_in, lb_table, g_hgrn, w_gate, w_proj_a, w_proj_b, w_out,
              g_xattn, g_mem, w_xq, w_xkv, w_xo, g_ffn, w_ffn_in, w_ffn_out, g_final):
    lb_all = jnp.cumsum(jax.nn.softmax(lb_table.astype(jnp.float32), axis=0), axis=0)
    for l in range(DEPTH):
        h = rms_norm(x, g_mix[l])
        q_a, f_a, i_a, o_a, q_b, k_b, v_b = jnp.split(h @ w_in[l], IN_OFFSETS, axis=-1)
        y_a = hgrn2_mixer(q_a, f_a, i_a, o_a, lb_all[l], g_hgrn[l])
        y_b = stick_breaking_mixer(q_b, k_b, v_b)
        gate_a, gate_b = jnp.split(jax.nn.sigmoid(h @ w_gate[l]), 2, axis=-1)
        merged = gate_a * (y_a @ w_proj_a[l]) + gate_b * (y_b @ w_proj_b[l])
        x = x + merged @ w_out[l]
        x = x + cross_attention(rms_norm(x, g_xattn[l]), rms_norm(mem, g_mem[l]),
                                w_xq[l], w_xkv[l], w_xo[l])
        x = x + swiglu(rms_norm(x, g_ffn[l]), w_ffn_in[l], w_ffn_out[l])
    return rms_norm(x, g_final)
```

```python
import functools

import jax
import jax.numpy as jnp
import numpy as np
from jax import lax
from jax.experimental import pallas as pl
from jax.experimental.pallas import tpu as pltpu

F32 = jnp.float32
BF16 = jnp.bfloat16
EPS = 1e-6

CHUNK = 64
HGRN_HEADS = 4
HGRN_D = 128
SB_HEADS = 8
SB_HEAD_DIM = 64
XATTN_HEADS = 4

V7X_VMEM_BYTES = 64 * 1024 * 1024
LANES = 128

_NT = (((1,), (1,)), ((), ()))
_TN = (((0,), (0,)), ((), ()))


def _rms(x, g):
    return x * lax.rsqrt(jnp.mean(x * x, axis=-1, keepdims=True) + EPS) * g


def _sigmoid(x):
    return 1.0 / (1.0 + jnp.exp(-x))


def _params(sem, vmem_mb):
    return pltpu.CompilerParams(dimension_semantics=sem, vmem_limit_bytes=vmem_mb * 1024 * 1024)


def _resident(shape):
    nd = len(shape)
    return pl.BlockSpec(shape, lambda *_: (0,) * nd, pipeline_mode=pl.Buffered(1))


def _norm_matmul_kernel(x_ref, g_ref, w_ref, *out_refs, splits):
    h = _rms(x_ref[...], g_ref[...]).astype(BF16)
    for o_ref, (lo, hi) in zip(out_refs, splits):
        o_ref[...] = jnp.dot(h, w_ref[:, lo:hi], preferred_element_type=F32).astype(o_ref.dtype)


def _norm_matmul(x2d, g, w, splits, dtypes, tm):
    m, d = x2d.shape
    n = w.shape[1]
    return pl.pallas_call(
        functools.partial(_norm_matmul_kernel, splits=splits),
        grid=(m // tm,),
        in_specs=[pl.BlockSpec((tm, d), lambda i: (i, 0)), _resident((1, d)), _resident((d, n))],
        out_specs=[pl.BlockSpec((tm, hi - lo), lambda i: (i, 0)) for lo, hi in splits],
        out_shape=[jax.ShapeDtypeStruct((m, hi - lo), dt) for (lo, hi), dt in zip(splits, dtypes)],
        compiler_params=_params(("parallel",), 48),
        name="norm_matmul",
    )(x2d, g, w)


HGRN_LEVELS = (32, 16, 8)
HGRN_DIAG = 8


def _hgrn_matrices():
    c = CHUNK
    t = np.arange(c)[:, None]
    j = np.arange(c)[None, :]
    mats = [(j <= t), (j > t)]
    for m in HGRN_LEVELS:
        p = (t // (2 * m)) * (2 * m) + m
        right = (t % (2 * m)) >= m
        mats.append(right & (j > p) & (j <= t))
        mats.append((~right) & (j > t) & (j <= p))
    p = (t // HGRN_DIAG) * HGRN_DIAG + HGRN_DIAG // 2
    mats.append(((j > p) & (j <= t)).astype(np.float32) - ((j > t) & (j <= p)).astype(np.float32))
    return np.concatenate([np.asarray(a, np.float32) for a in mats], axis=0)


def _hgrn_masks():
    c = CHUNK
    t = lax.broadcasted_iota(jnp.int32, (c, c), 0)
    s = lax.broadcasted_iota(jnp.int32, (c, c), 1)
    masks = []
    for m in HGRN_LEVELS:
        same = (t // (2 * m)) == (s // (2 * m))
        masks.append(same & ((t % (2 * m)) >= m) & ((s % (2 * m)) < m))
    masks.append(((t // HGRN_DIAG) == (s // HGRN_DIAG)) & (s <= t))
    return masks


def _split3(x):
    hi = x.astype(BF16)
    r = x - hi.astype(F32)
    mid = r.astype(BF16)
    lo = (r - mid.astype(F32)).astype(BF16)
    return hi, mid, lo


def _hgrn_kernel(hg_ref, lb_ref, gn_ref, cm_ref, o_ref, st_ref, *, n_chunks):
    c, nh, dh = CHUNK, HGRN_HEADS, HGRN_D
    w = nh * dh

    @pl.when(pl.program_id(1) == 0)
    def _():
        st_ref[...] = jnp.zeros_like(st_ref)

    masks = _hgrn_masks()
    lb = lb_ref[...]
    gn = gn_ref[...]
    cm = cm_ref[...]

    def chunk(ci, carry):
        r0 = pl.multiple_of(ci * c, c)
        rows = pl.ds(r0, c)
        qp = hg_ref[0, rows, 0:w]
        fp = hg_ref[0, rows, w:2 * w]
        iv = hg_ref[0, rows, 2 * w:3 * w]
        gp = hg_ref[0, rows, 3 * w:4 * w]
        q = qp * _sigmoid(qp)
        f = lb + (1.0 - lb) * _sigmoid(fp)
        k = 1.0 - f
        hi, mid, lo = _split3(jnp.log(f))
        dall = (jnp.dot(cm, hi, preferred_element_type=F32)
                + jnp.dot(cm, mid, preferred_element_type=F32)
                + jnp.dot(cm, lo, preferred_element_type=F32))

        def grp(g):
            return dall[g * c:(g + 1) * c, :]

        e_b = jnp.exp(grp(0))
        e_k = jnp.exp(grp(1))
        e_last = e_b[c - 1:c, :]
        qa = [(q * jnp.exp(grp(2 + 2 * l))).astype(BF16) for l in range(len(HGRN_LEVELS))]
        kb = [(k * jnp.exp(grp(3 + 2 * l))).astype(BF16) for l in range(len(HGRN_LEVELS))]
        dd = grp(2 + 2 * len(HGRN_LEVELS))
        qa.append((q * jnp.exp(dd)).astype(BF16))
        kb.append((k * jnp.exp(-dd)).astype(BF16))
        q_in = (q * e_b).astype(BF16)
        k_out = (k * e_k).astype(BF16)
        v16 = iv.astype(BF16)

        outs = []
        for h in range(nh):
            hs = slice(h * dh, (h + 1) * dh)
            scores = jnp.zeros((c, c), F32)
            for a_l, b_l, m_l in zip(qa, kb, masks):
                s_l = lax.dot_general(a_l[:, hs], b_l[:, hs], _NT, preferred_element_type=F32)
                scores = jnp.where(m_l, s_l, scores)
            st = st_ref[h]
            o = (jnp.dot(scores.astype(BF16), v16[:, hs], preferred_element_type=F32)
                 + lax.dot_general(q_in[:, hs], st.astype(BF16), _NT, preferred_element_type=F32))
            st_ref[h] = e_last[:, hs] * st + lax.dot_general(
                v16[:, hs], k_out[:, hs], _TN, preferred_element_type=F32)
            outs.append(o * lax.rsqrt(jnp.mean(o * o, axis=-1, keepdims=True) + EPS))
        on = jnp.concatenate(outs, axis=-1) * gn
        o_ref[0, rows, :] = (on * (gp * _sigmoid(gp))).astype(o_ref.dtype)
        return carry

    lax.fori_loop(0, n_chunks, chunk, 0)


def _hgrn(hg, lb, gn, rows_per_step):
    b, t, w4 = hg.shape
    w = w4 // 4
    cm = jnp.asarray(_hgrn_matrices(), BF16)
    return pl.pallas_call(
        functools.partial(_hgrn_kernel, n_chunks=rows_per_step // CHUNK),
        grid=(b, t // rows_per_step),
        in_specs=[pl.BlockSpec((1, rows_per_step, w4), lambda bi, i: (bi, i, 0)),
                  _resident((1, w)), _resident((1, w)), _resident(cm.shape)],
        out_specs=pl.BlockSpec((1, rows_per_step, w), lambda bi, i: (bi, i, 0)),
        out_shape=jax.ShapeDtypeStruct((b, t, w), BF16),
        scratch_shapes=[pltpu.VMEM((HGRN_HEADS, HGRN_D, HGRN_D), F32)],
        compiler_params=_params(("parallel", "arbitrary"), 48),
        name="hgrn2",
    )(hg, lb, gn, cm)


def _sb_tile(qe, kblk, vblk, upper, carry, diag):
    z = lax.dot_general(qe, kblk, _NT, preferred_element_type=F32)
    sp = jnp.maximum(z, 0.0) + jnp.log(1.0 + jnp.exp(-jnp.abs(z)))
    if diag:
        t = lax.broadcasted_iota(jnp.int32, z.shape, 0)
        s = lax.broadcasted_iota(jnp.int32, z.shape, 1)
        valid = s < t
        sp_m = jnp.where(valid, sp, 0.0)
    else:
        sp_m = sp
    hi = sp_m.astype(BF16)
    lo = (sp_m - hi.astype(F32)).astype(BF16)
    later = (jnp.dot(hi, upper, preferred_element_type=F32)
             + jnp.dot(lo, upper, preferred_element_type=F32))
    a = jnp.exp(z - sp - later - carry)
    if diag:
        a = jnp.where(valid, a, 0.0)
    pv = jnp.dot(a.astype(BF16), vblk, preferred_element_type=F32)
    return pv, carry + jnp.sum(sp_m, axis=-1, keepdims=True)


def _sb_kernel(q_ref, k_ref, v_ref, o_ref, *, blk):
    i = pl.program_id(2)
    lane = lax.broadcasted_iota(jnp.int32, (1, LANES), 1)
    first = lane < SB_HEAD_DIM
    jj = lax.broadcasted_iota(jnp.int32, (blk, blk), 0)
    ss = lax.broadcasted_iota(jnp.int32, (blk, blk), 1)
    upper = (jj > ss).astype(BF16)

    q = q_ref[0] * (SB_HEAD_DIM ** -0.5)
    zero = jnp.zeros_like(q)
    heads = (jnp.where(first, q, zero), jnp.where(first, zero, q))

    accs = []
    for qe in heads:
        rows = pl.ds(pl.multiple_of(i * blk, blk), blk)
        acc, carry = _sb_tile(qe, k_ref[0, rows, :], v_ref[0, rows, :], upper,
                              jnp.zeros((blk, 1), F32), True)

        def body(n, state, qe=qe):
            acc, carry = state
            rows = pl.ds(pl.multiple_of((i - 1 - n) * blk, blk), blk)
            pv, carry = _sb_tile(qe, k_ref[0, rows, :], v_ref[0, rows, :], upper, carry, False)
            return acc + pv, carry

        acc, _ = lax.fori_loop(0, i, body, (acc, carry))
        accs.append(acc)
    o_ref[0] = jnp.where(first, accs[0], accs[1]).astype(o_ref.dtype)


def _stickbreak(sb, blk):
    b, t, w3 = sb.shape
    w = w3 // 3
    npair = w // LANES
    return pl.pallas_call(
        functools.partial(_sb_kernel, blk=blk),
        grid=(b, npair, t // blk),
        in_specs=[pl.BlockSpec((1, blk, LANES), lambda bi, p, i: (bi, i, p)),
                  pl.BlockSpec((1, t, LANES), lambda bi, p, i: (bi, 0, npair + p)),
                  pl.BlockSpec((1, t, LANES), lambda bi, p, i: (bi, 0, 2 * npair + p))],
        out_specs=pl.BlockSpec((1, blk, LANES), lambda bi, p, i: (bi, i, p)),
        out_shape=jax.ShapeDtypeStruct((b, t, w), BF16),
        compiler_params=_params(("parallel", "parallel", "arbitrary"), 48),
        name="stickbreak",
    )(sb, sb, sb)


def _mix_kernel(x_ref, ya_ref, yb_ref, g_ref, wg_ref, wa_ref, wb_ref, wo_ref, o_ref):
    x = x_ref[...]
    d = x.shape[-1]
    h = _rms(x, g_ref[...]).astype(BF16)
    gate = _sigmoid(jnp.dot(h, wg_ref[...], preferred_element_type=F32))
    pa = jnp.dot(ya_ref[...], wa_ref[...], preferred_element_type=F32)
    pb = jnp.dot(yb_ref[...], wb_ref[...], preferred_element_type=F32)
    merged = gate[:, :d] * pa + gate[:, d:] * pb
    o_ref[...] = x + jnp.dot(merged.astype(BF16), wo_ref[...], preferred_element_type=F32)


def _mix_out(x2d, ya, yb, g, wg, wa, wb, wo, tm):
    m, d = x2d.shape
    wy = ya.shape[1]
    row = lambda n: pl.BlockSpec((tm, n), lambda i: (i, 0))
    return pl.pallas_call(
        _mix_kernel,
        grid=(m // tm,),
        in_specs=[row(d), row(wy), row(wy), _resident((1, d)), _resident(wg.shape),
                  _resident(wa.shape), _resident(wb.shape), _resident(wo.shape)],
        out_specs=row(d),
        out_shape=jax.ShapeDtypeStruct((m, d), F32),
        compiler_params=_params(("parallel",), 48),
        name="mix_out",
    )(x2d, ya, yb, g, wg, wa, wb, wo)


def _xattn_kernel(x_ref, g_ref, wq_ref, k_ref, v_ref, wo_ref, o_ref):
    x = x_ref[0]
    d = x.shape[-1]
    dh = d // XATTN_HEADS
    h = _rms(x, g_ref[...]).astype(BF16)
    q = (jnp.dot(h, wq_ref[...], preferred_element_type=F32) * (dh ** -0.5)).astype(BF16)
    outs = []
    for e in range(XATTN_HEADS):
        hs = slice(e * dh, (e + 1) * dh)
        s = lax.dot_general(q[:, hs], k_ref[0, :, hs], _NT, preferred_element_type=F32)
        p = jnp.exp(s - jnp.max(s, axis=-1, keepdims=True))
        o = jnp.dot(p.astype(BF16), v_ref[0, :, hs], preferred_element_type=F32)
        outs.append(o / jnp.sum(p, axis=-1, keepdims=True))
    o = jnp.concatenate(outs, axis=-1).astype(BF16)
    o_ref[0] = x + jnp.dot(o, wo_ref[...], preferred_element_type=F32)


def _xattn(x3d, g, wq, kmem, vmem, wo, tm):
    b, t, d = x3d.shape
    nm = kmem.shape[1]
    return pl.pallas_call(
        _xattn_kernel,
        grid=(b, t // tm),
        in_specs=[pl.BlockSpec((1, tm, d), lambda bi, i: (bi, i, 0)), _resident((1, d)),
                  _resident(wq.shape),
                  pl.BlockSpec((1, nm, d), lambda bi, i: (bi, 0, 0)),
                  pl.BlockSpec((1, nm, d), lambda bi, i: (bi, 0, 0)),
                  _resident(wo.shape)],
        out_specs=pl.BlockSpec((1, tm, d), lambda bi, i: (bi, i, 0)),
        out_shape=jax.ShapeDtypeStruct((b, t, d), F32),
        compiler_params=_params(("parallel", "parallel"), 48),
        name="xattn",
    )(x3d, g, wq, kmem, vmem, wo)


def _ffn_kernel(x_ref, g_ref, wi_ref, wo_ref, gf_ref, o_ref, *, dff, final_norm):
    x = x_ref[...]
    h = _rms(x, g_ref[...]).astype(BF16)
    gate = jnp.dot(h, wi_ref[:, :dff], preferred_element_type=F32)
    up = jnp.dot(h, wi_ref[:, dff:], preferred_element_type=F32)
    act = (gate * _sigmoid(gate) * up).astype(BF16)
    y = x + jnp.dot(act, wo_ref[...], preferred_element_type=F32)
    o_ref[...] = _rms(y, gf_ref[...]) if final_norm else y


def _ffn(x2d, g, wi, wo, gf, final_norm, tm):
    m, d = x2d.shape
    dff = wo.shape[0]
    row = pl.BlockSpec((tm, d), lambda i: (i, 0))
    return pl.pallas_call(
        functools.partial(_ffn_kernel, dff=dff, final_norm=final_norm),
        grid=(m // tm,),
        in_specs=[row, _resident((1, d)), _resident(wi.shape), _resident(wo.shape), _resident((1, d))],
        out_specs=row,
        out_shape=jax.ShapeDtypeStruct((m, d), F32),
        compiler_params=_params(("parallel",), 56),
        name="ffn",
    )(x2d, g, wi, wo, gf)


def _tile(n, want):
    return want if n % want == 0 else n


def kernel(x, mem, g_mix, w_in, lb_table, g_hgrn, w_gate, w_proj_a, w_proj_b, w_out, g_xattn, g_mem,
           w_xq, w_xkv, w_xo, g_ffn, w_ffn_in, w_ffn_out, g_final):
    b, t, d = x.shape
    depth = w_in.shape[0]
    nm = mem.shape[1]
    hq = HGRN_HEADS * HGRN_D
    sbw = SB_HEADS * SB_HEAD_DIM
    row = lambda v: v.reshape(1, -1).astype(F32)
    lb_all = jnp.cumsum(jax.nn.softmax(lb_table.astype(F32), axis=0), axis=0)

    x2d = x.reshape(b * t, d)
    for l in range(depth):
        hg, sb = _norm_matmul(x2d, row(g_mix[l]), w_in[l].astype(BF16),
                              ((0, 4 * hq), (4 * hq, 4 * hq + 3 * sbw)), (F32, BF16), _tile(b * t, 512))
        ya = _hgrn(hg.reshape(b, t, 4 * hq), row(lb_all[l]), row(g_hgrn[l]), _tile(t, 512))
        yb = _stickbreak(sb.reshape(b, t, 3 * sbw), _tile(t, 256))
        x2d = _mix_out(x2d, ya.reshape(b * t, hq), yb.reshape(b * t, sbw), row(g_mix[l]),
                       w_gate[l].astype(BF16), w_proj_a[l].astype(BF16), w_proj_b[l].astype(BF16),
                       w_out[l].astype(BF16), _tile(b * t, 512))
        kmem, vmem = _norm_matmul(mem.reshape(b * nm, d), row(g_mem[l]), w_xkv[l].astype(BF16),
                                  ((0, d), (d, 2 * d)), (BF16, BF16), _tile(b * nm, 256))
        x2d = _xattn(x2d.reshape(b, t, d), row(g_xattn[l]), w_xq[l].astype(BF16),
                     kmem.reshape(b, nm, d), vmem.reshape(b, nm, d), w_xo[l].astype(BF16),
                     _tile(t, 512)).reshape(b * t, d)
        x2d = _ffn(x2d, row(g_ffn[l]), w_ffn_in[l].astype(BF16), w_ffn_out[l].astype(BF16),
                   row(g_final), l == depth - 1, _tile(b * t, 256))
    return x2d.reshape(b, t, d)
```

```python
import functools

import jax
import jax.numpy as jnp
import numpy as np
from jax import lax
from jax.experimental import pallas as pl
from jax.experimental.pallas import tpu as pltpu

F32 = jnp.float32
BF16 = jnp.bfloat16
EPS = 1e-6

CHUNK = 64
HGRN_HEADS = 4
HGRN_D = 128
SB_HEADS = 8
SB_HEAD_DIM = 64
XATTN_HEADS = 4

V7X_VMEM_BYTES = 64 * 1024 * 1024
LANES = 128

_NT = (((1,), (1,)), ((), ()))
_TN = (((0,), (0,)), ((), ()))


def _rms(x, g):
    return x * lax.rsqrt(jnp.mean(x * x, axis=-1, keepdims=True) + EPS) * g


def _sigmoid(x):
    return 1.0 / (1.0 + jnp.exp(-x))


def _params(sem, vmem_mb):
    return pltpu.CompilerParams(dimension_semantics=sem, vmem_limit_bytes=vmem_mb * 1024 * 1024)


def _resident(shape):
    nd = len(shape)
    return pl.BlockSpec(shape, lambda *_: (0,) * nd, pipeline_mode=pl.Buffered(1))


def _norm_matmul_kernel(x_ref, g_ref, w_ref, *out_refs, splits):
    h = _rms(x_ref[...], g_ref[...]).astype(BF16)
    for o_ref, (lo, hi) in zip(out_refs, splits):
        o_ref[...] = jnp.dot(h, w_ref[:, lo:hi], preferred_element_type=F32).astype(o_ref.dtype)


def _norm_matmul(x2d, g, w, splits, dtypes, tm):
    m, d = x2d.shape
    n = w.shape[1]
    return pl.pallas_call(
        functools.partial(_norm_matmul_kernel, splits=splits),
        grid=(m // tm,),
        in_specs=[pl.BlockSpec((tm, d), lambda i: (i, 0)), _resident((1, d)), _resident((d, n))],
        out_specs=[pl.BlockSpec((tm, hi - lo), lambda i: (i, 0)) for lo, hi in splits],
        out_shape=[jax.ShapeDtypeStruct((m, hi - lo), dt) for (lo, hi), dt in zip(splits, dtypes)],
        compiler_params=_params(("parallel",), 48),
        name="norm_matmul",
    )(x2d, g, w)


HGRN_LEVELS = (32, 16, 8)
HGRN_DIAG = 8


def _hgrn_matrices():
    c = CHUNK
    t = np.arange(c)[:, None]
    j = np.arange(c)[None, :]
    mats = [(j <= t), (j > t)]
    for m in HGRN_LEVELS:
        p = (t // (2 * m)) * (2 * m) + m
        right = (t % (2 * m)) >= m
        mats.append(right & (j > p) & (j <= t))
        mats.append((~right) & (j > t) & (j <= p))
    p = (t // HGRN_DIAG) * HGRN_DIAG + HGRN_DIAG // 2
    mats.append(((j > p) & (j <= t)).astype(np.float32) - ((j > t) & (j <= p)).astype(np.float32))
    return np.concatenate([np.asarray(a, np.float32) for a in mats], axis=0)


def _hgrn_masks():
    c = CHUNK
    t = lax.broadcasted_iota(jnp.int32, (c, c), 0)
    s = lax.broadcasted_iota(jnp.int32, (c, c), 1)
    masks = []
    for m in HGRN_LEVELS:
        same = (t // (2 * m)) == (s // (2 * m))
        masks.append(same & ((t % (2 * m)) >= m) & ((s % (2 * m)) < m))
    masks.append(((t // HGRN_DIAG) == (s // HGRN_DIAG)) & (s <= t))
    return masks


def _split3(x):
    hi = x.astype(BF16)
    r = x - hi.astype(F32)
    mid = r.astype(BF16)
    lo = (r - mid.astype(F32)).astype(BF16)
    return hi, mid, lo


def _hgrn_kernel(hg_ref, lb_ref, gn_ref, cm_ref, o_ref, st_ref, *, n_chunks):
    c, nh, dh = CHUNK, HGRN_HEADS, HGRN_D
    w = nh * dh

    @pl.when(pl.program_id(1) == 0)
    def _():
        st_ref[...] = jnp.zeros_like(st_ref)

    masks = _hgrn_masks()
    lb = lb_ref[...]
    gn = gn_ref[...]
    cm = cm_ref[...]

    def chunk(ci, carry):
        r0 = pl.multiple_of(ci * c, c)
        rows = pl.ds(r0, c)
        qp = hg_ref[0, rows, 0:w]
        fp = hg_ref[0, rows, w:2 * w]
        iv = hg_ref[0, rows, 2 * w:3 * w]
        gp = hg_ref[0, rows, 3 * w:4 * w]
        q = qp * _sigmoid(qp)
        f = lb + (1.0 - lb) * _sigmoid(fp)
        k = 1.0 - f
        hi, mid, lo = _split3(jnp.log(f))
        dall = (jnp.dot(cm, hi, preferred_element_type=F32)
                + jnp.dot(cm, mid, preferred_element_type=F32)
                + jnp.dot(cm, lo, preferred_element_type=F32))

        def grp(g):
            return dall[g * c:(g + 1) * c, :]

        e_b = jnp.exp(grp(0))
        e_k = jnp.exp(grp(1))
        e_last = e_b[c - 1:c, :]
        qa = [(q * jnp.exp(grp(2 + 2 * l))).astype(BF16) for l in range(len(HGRN_LEVELS))]
        kb = [(k * jnp.exp(grp(3 + 2 * l))).astype(BF16) for l in range(len(HGRN_LEVELS))]
        dd = grp(2 + 2 * len(HGRN_LEVELS))
        qa.append((q * jnp.exp(dd)).astype(BF16))
        kb.append((k * jnp.exp(-dd)).astype(BF16))
        q_in = (q * e_b).astype(BF16)
        k_out = (k * e_k).astype(BF16)
        v16 = iv.astype(BF16)

        outs = []
        for h in range(nh):
            hs = slice(h * dh, (h + 1) * dh)
            scores = jnp.zeros((c, c), F32)
            for a_l, b_l, m_l in zip(qa, kb, masks):
                s_l = lax.dot_general(a_l[:, hs], b_l[:, hs], _NT, preferred_element_type=F32)
                scores = jnp.where(m_l, s_l, scores)
            st = st_ref[h]
            o = (jnp.dot(scores.astype(BF16), v16[:, hs], preferred_element_type=F32)
                 + lax.dot_general(q_in[:, hs], st.astype(BF16), _NT, preferred_element_type=F32))
            st_ref[h] = e_last[:, hs] * st + lax.dot_general(
                v16[:, hs], k_out[:, hs], _TN, preferred_element_type=F32)
            outs.append(o * lax.rsqrt(jnp.mean(o * o, axis=-1, keepdims=True) + EPS))
        on = jnp.concatenate(outs, axis=-1) * gn
        o_ref[0, rows, :] = (on * (gp * _sigmoid(gp))).astype(o_ref.dtype)
        return carry

    lax.fori_loop(0, n_chunks, chunk, 0)


def _hgrn(hg, lb, gn, rows_per_step):
    b, t, w4 = hg.shape
    w = w4 // 4
    cm = jnp.asarray(_hgrn_matrices(), BF16)
    return pl.pallas_call(
        functools.partial(_hgrn_kernel, n_chunks=rows_per_step // CHUNK),
        grid=(b, t // rows_per_step),
        in_specs=[pl.BlockSpec((1, rows_per_step, w4), lambda bi, i: (bi, i, 0)),
                  _resident((1, w)), _resident((1, w)), _resident(cm.shape)],
        out_specs=pl.BlockSpec((1, rows_per_step, w), lambda bi, i: (bi, i, 0)),
        out_shape=jax.ShapeDtypeStruct((b, t, w), BF16),
        scratch_shapes=[pltpu.VMEM((HGRN_HEADS, HGRN_D, HGRN_D), F32)],
        compiler_params=_params(("parallel", "arbitrary"), 48),
        name="hgrn2",
    )(hg, lb, gn, cm)


LOG2E = 1.4426950408889634
SB_SKIP_LOG2 = 160.0


def _sb_tile(qe, kblk, vblk, upper, carry, diag):
    z = lax.dot_general(qe, kblk, _NT, preferred_element_type=F32)
    sp = jnp.maximum(z, 0.0) + jnp.log(1.0 + jnp.exp2(-jnp.abs(z))) * LOG2E
    if diag:
        t = lax.broadcasted_iota(jnp.int32, z.shape, 0)
        s = lax.broadcasted_iota(jnp.int32, z.shape, 1)
        valid = s < t
        sp_m = jnp.where(valid, sp, 0.0)
    else:
        sp_m = sp
    hi = sp_m.astype(BF16)
    lo = (sp_m - hi.astype(F32)).astype(BF16)
    later = (jnp.dot(hi, upper, preferred_element_type=F32)
             + jnp.dot(lo, upper, preferred_element_type=F32))
    a = jnp.exp2(z - sp - later - carry)
    if diag:
        a = jnp.where(valid, a, 0.0)
    pv = jnp.dot(a.astype(BF16), vblk, preferred_element_type=F32)
    return pv, carry + later[:, 0:1] + sp_m[:, 0:1]


def _sb_kernel(q_ref, k_ref, v_ref, o_ref, *, blk, npair):
    i = pl.program_id(1)
    lane = lax.broadcasted_iota(jnp.int32, (1, LANES), 1)
    first = lane < SB_HEAD_DIM
    jj = lax.broadcasted_iota(jnp.int32, (blk, blk), 0)
    ss = lax.broadcasted_iota(jnp.int32, (blk, blk), 1)
    upper = (jj > ss).astype(BF16)

    def sweep(j, accs, carries, diag):
        rows = pl.ds(pl.multiple_of(j * blk, blk), blk)
        new_accs, new_carries = [], []
        for p in range(npair):
            ls = slice(p * LANES, (p + 1) * LANES)
            q = q_ref[0, :, ls]
            kblk = k_ref[0, rows, ls]
            vblk = v_ref[0, rows, ls]
            zero = jnp.zeros_like(q)
            acc = accs[p]
            for e in range(2):
                keep = first if e == 0 else jnp.logical_not(first)
                pv, c = _sb_tile(jnp.where(keep, q, zero), kblk, jnp.where(keep, vblk, zero), upper,
                                 carries[2 * p + e], diag)
                acc = acc + pv
                new_carries.append(c)
            new_accs.append(acc)
        low = functools.reduce(jnp.minimum, new_carries)
        done = (jnp.min(low) >= SB_SKIP_LOG2).astype(jnp.int32)
        return tuple(new_accs), tuple(new_carries), done

    accs = tuple(jnp.zeros((blk, LANES), F32) for _ in range(npair))
    carries = tuple(jnp.zeros((blk, 1), F32) for _ in range(2 * npair))
    accs, carries, done = sweep(i, accs, carries, True)

    def cond(st):
        return jnp.logical_and(st[0] <= i, st[1] == 0)

    def body(st):
        n, _, accs, carries = st
        accs, carries, done = sweep(i - n, accs, carries, False)
        return n + 1, done, accs, carries

    _, _, accs, _ = lax.while_loop(cond, body, (jnp.int32(1), done, accs, carries))
    o_ref[0] = jnp.concatenate(accs, axis=-1).astype(o_ref.dtype)


def _stickbreak(sb, blk):
    b, t, w3 = sb.shape
    w = w3 // 3
    return pl.pallas_call(
        functools.partial(_sb_kernel, blk=blk, npair=w // LANES),
        grid=(b, t // blk),
        in_specs=[pl.BlockSpec((1, blk, w), lambda bi, i: (bi, i, 0)),
                  pl.BlockSpec((1, t, w), lambda bi, i: (bi, 0, 1)),
                  pl.BlockSpec((1, t, w), lambda bi, i: (bi, 0, 2))],
        out_specs=pl.BlockSpec((1, blk, w), lambda bi, i: (bi, i, 0)),
        out_shape=jax.ShapeDtypeStruct((b, t, w), BF16),
        compiler_params=_params(("parallel", "arbitrary"), 56),
        name="stickbreak",
    )(sb, sb, sb)


def _mix_kernel(x_ref, ya_ref, yb_ref, g_ref, wg_ref, wa_ref, wb_ref, wo_ref, o_ref):
    x = x_ref[...]
    d = x.shape[-1]
    h = _rms(x, g_ref[...]).astype(BF16)
    gate = _sigmoid(jnp.dot(h, wg_ref[...], preferred_element_type=F32))
    pa = jnp.dot(ya_ref[...], wa_ref[...], preferred_element_type=F32)
    pb = jnp.dot(yb_ref[...], wb_ref[...], preferred_element_type=F32)
    merged = gate[:, :d] * pa + gate[:, d:] * pb
    o_ref[...] = x + jnp.dot(merged.astype(BF16), wo_ref[...], preferred_element_type=F32)


def _mix_out(x2d, ya, yb, g, wg, wa, wb, wo, tm):
    m, d = x2d.shape
    wy = ya.shape[1]
    row = lambda n: pl.BlockSpec((tm, n), lambda i: (i, 0))
    return pl.pallas_call(
        _mix_kernel,
        grid=(m // tm,),
        in_specs=[row(d), row(wy), row(wy), _resident((1, d)), _resident(wg.shape),
                  _resident(wa.shape), _resident(wb.shape), _resident(wo.shape)],
        out_specs=row(d),
        out_shape=jax.ShapeDtypeStruct((m, d), F32),
        compiler_params=_params(("parallel",), 48),
        name="mix_out",
    )(x2d, ya, yb, g, wg, wa, wb, wo)


def _xattn_kernel(x_ref, g_ref, wq_ref, k_ref, v_ref, wo_ref, o_ref):
    x = x_ref[0]
    d = x.shape[-1]
    dh = d // XATTN_HEADS
    h = _rms(x, g_ref[...]).astype(BF16)
    q = (jnp.dot(h, wq_ref[...], preferred_element_type=F32) * (dh ** -0.5)).astype(BF16)
    outs = []
    for e in range(XATTN_HEADS):
        hs = slice(e * dh, (e + 1) * dh)
        s = lax.dot_general(q[:, hs], k_ref[0, :, hs], _NT, preferred_element_type=F32)
        p = jnp.exp(s - jnp.max(s, axis=-1, keepdims=True))
        o = jnp.dot(p.astype(BF16), v_ref[0, :, hs], preferred_element_type=F32)
        outs.append(o / jnp.sum(p, axis=-1, keepdims=True))
    o = jnp.concatenate(outs, axis=-1).astype(BF16)
    o_ref[0] = x + jnp.dot(o, wo_ref[...], preferred_element_type=F32)


def _xattn(x3d, g, wq, kmem, vmem, wo, tm):
    b, t, d = x3d.shape
    nm = kmem.shape[1]
    return pl.pallas_call(
        _xattn_kernel,
        grid=(b, t // tm),
        in_specs=[pl.BlockSpec((1, tm, d), lambda bi, i: (bi, i, 0)), _resident((1, d)),
                  _resident(wq.shape),
                  pl.BlockSpec((1, nm, d), lambda bi, i: (bi, 0, 0)),
                  pl.BlockSpec((1, nm, d), lambda bi, i: (bi, 0, 0)),
                  _resident(wo.shape)],
        out_specs=pl.BlockSpec((1, tm, d), lambda bi, i: (bi, i, 0)),
        out_shape=jax.ShapeDtypeStruct((b, t, d), F32),
        compiler_params=_params(("parallel", "parallel"), 48),
        name="xattn",
    )(x3d, g, wq, kmem, vmem, wo)


def _ffn_kernel(x_ref, g_ref, wi_ref, wo_ref, gf_ref, o_ref, *, dff, final_norm):
    x = x_ref[...]
    h = _rms(x, g_ref[...]).astype(BF16)
    gate = jnp.dot(h, wi_ref[:, :dff], preferred_element_type=F32)
    up = jnp.dot(h, wi_ref[:, dff:], preferred_element_type=F32)
    act = (gate * _sigmoid(gate) * up).astype(BF16)
    y = x + jnp.dot(act, wo_ref[...], preferred_element_type=F32)
    o_ref[...] = _rms(y, gf_ref[...]) if final_norm else y


def _ffn(x2d, g, wi, wo, gf, final_norm, tm):
    m, d = x2d.shape
    dff = wo.shape[0]
    row = pl.BlockSpec((tm, d), lambda i: (i, 0))
    return pl.pallas_call(
        functools.partial(_ffn_kernel, dff=dff, final_norm=final_norm),
        grid=(m // tm,),
        in_specs=[row, _resident((1, d)), _resident(wi.shape), _resident(wo.shape), _resident((1, d))],
        out_specs=row,
        out_shape=jax.ShapeDtypeStruct((m, d), F32),
        compiler_params=_params(("parallel",), 56),
        name="ffn",
    )(x2d, g, wi, wo, gf)


def _tile(n, want):
    return want if n % want == 0 else n


def kernel(x, mem, g_mix, w_in, lb_table, g_hgrn, w_gate, w_proj_a, w_proj_b, w_out, g_xattn, g_mem,
           w_xq, w_xkv, w_xo, g_ffn, w_ffn_in, w_ffn_out, g_final):
    b, t, d = x.shape
    depth = w_in.shape[0]
    nm = mem.shape[1]
    hq = HGRN_HEADS * HGRN_D
    sbw = SB_HEADS * SB_HEAD_DIM
    row = lambda v: v.reshape(1, -1).astype(F32)
    lb_all = jnp.cumsum(jax.nn.softmax(lb_table.astype(F32), axis=0), axis=0)

    col_scale = jnp.ones((4 * hq + 3 * sbw,), F32).at[4 * hq:4 * hq + sbw].set(SB_HEAD_DIM ** -0.5 * LOG2E)

    x2d = x.reshape(b * t, d)
    for l in range(depth):
        hg, sb = _norm_matmul(x2d, row(g_mix[l]), (w_in[l] * col_scale).astype(BF16),
                              ((0, 4 * hq), (4 * hq, 4 * hq + 3 * sbw)), (F32, BF16), _tile(b * t, 512))
        ya = _hgrn(hg.reshape(b, t, 4 * hq), row(lb_all[l]), row(g_hgrn[l]), _tile(t, 512))
        yb = _stickbreak(sb.reshape(b, t, 3 * sbw), _tile(t, 256))
        x2d = _mix_out(x2d, ya.reshape(b * t, hq), yb.reshape(b * t, sbw), row(g_mix[l]),
                       w_gate[l].astype(BF16), w_proj_a[l].astype(BF16), w_proj_b[l].astype(BF16),
                       w_out[l].astype(BF16), _tile(b * t, 512))
        kmem, vmem = _norm_matmul(mem.reshape(b * nm, d), row(g_mem[l]), w_xkv[l].astype(BF16),
                                  ((0, d), (d, 2 * d)), (BF16, BF16), _tile(b * nm, 256))
        x2d = _xattn(x2d.reshape(b, t, d), row(g_xattn[l]), w_xq[l].astype(BF16),
                     kmem.reshape(b, nm, d), vmem.reshape(b, nm, d), w_xo[l].astype(BF16),
                     _tile(t, 512)).reshape(b * t, d)
        x2d = _ffn(x2d, row(g_ffn[l]), w_ffn_in[l].astype(BF16), w_ffn_out[l].astype(BF16),
                   row(g_final), l == depth - 1, _tile(b * t, 256))
    return x2d.reshape(b, t, d)
```

```python
import functools

import jax
import jax.numpy as jnp
import numpy as np
from jax import lax
from jax.experimental import pallas as pl
from jax.experimental.pallas import tpu as pltpu

F32 = jnp.float32
BF16 = jnp.bfloat16
EPS = 1e-6

CHUNK = 64
HGRN_HEADS = 4
HGRN_D = 128
SB_HEADS = 8
SB_HEAD_DIM = 64
XATTN_HEADS = 4

V7X_VMEM_BYTES = 64 * 1024 * 1024
LANES = 128

_NT = (((1,), (1,)), ((), ()))
_TN = (((0,), (0,)), ((), ()))


def _rms(x, g):
    return x * lax.rsqrt(jnp.mean(x * x, axis=-1, keepdims=True) + EPS) * g


def _sigmoid(x):
    return 1.0 / (1.0 + jnp.exp(-x))


def _params(sem, vmem_mb):
    return pltpu.CompilerParams(dimension_semantics=sem, vmem_limit_bytes=vmem_mb * 1024 * 1024)


def _resident(shape):
    nd = len(shape)
    return pl.BlockSpec(shape, lambda *_: (0,) * nd, pipeline_mode=pl.Buffered(1))


def _norm_matmul_kernel(x_ref, g_ref, w_ref, *out_refs, splits):
    h = _rms(x_ref[...], g_ref[...]).astype(BF16)
    for o_ref, (lo, hi) in zip(out_refs, splits):
        o_ref[...] = jnp.dot(h, w_ref[:, lo:hi], preferred_element_type=F32).astype(o_ref.dtype)


def _norm_matmul(x2d, g, w, splits, dtypes, tm):
    m, d = x2d.shape
    n = w.shape[1]
    return pl.pallas_call(
        functools.partial(_norm_matmul_kernel, splits=splits),
        grid=(m // tm,),
        in_specs=[pl.BlockSpec((tm, d), lambda i: (i, 0)), _resident((1, d)), _resident((d, n))],
        out_specs=[pl.BlockSpec((tm, hi - lo), lambda i: (i, 0)) for lo, hi in splits],
        out_shape=[jax.ShapeDtypeStruct((m, hi - lo), dt) for (lo, hi), dt in zip(splits, dtypes)],
        compiler_params=_params(("parallel",), 48),
        name="norm_matmul",
    )(x2d, g, w)


HGRN_LEVELS = (32, 16, 8)
HGRN_DIAG = 8


def _hgrn_masks():
    c = CHUNK
    t = lax.broadcasted_iota(jnp.int32, (c, c), 0)
    s = lax.broadcasted_iota(jnp.int32, (c, c), 1)
    masks = []
    for m in HGRN_LEVELS:
        same = (t // (2 * m)) == (s // (2 * m))
        masks.append(same & ((t % (2 * m)) >= m) & ((s % (2 * m)) < m))
    masks.append(((t // HGRN_DIAG) == (s // HGRN_DIAG)) & (s <= t))
    return masks


def _split3(x):
    hi = x.astype(BF16)
    r = x - hi.astype(F32)
    mid = r.astype(BF16)
    lo = (r - mid.astype(F32)).astype(BF16)
    return hi, mid, lo


def _hgrn_kernel(hg_ref, lb_ref, gn_ref, tril_ref, o_ref, st_ref, *, n_chunks):
    c, nh, dh = CHUNK, HGRN_HEADS, HGRN_D
    w = nh * dh

    @pl.when(pl.program_id(1) == 0)
    def _():
        st_ref[...] = jnp.zeros_like(st_ref)

    def wide(col):
        return jnp.concatenate(
            [hg_ref[0, ci * c:(ci + 1) * c, col * w:(col + 1) * w] for ci in range(n_chunks)], axis=1)

    def tiled(row):
        return jnp.concatenate([row] * n_chunks, axis=1)

    qp, fp, iv, gp = wide(0), wide(1), wide(2), wide(3)
    lb = tiled(lb_ref[...])
    q = qp * _sigmoid(qp)
    f = lb + (1.0 - lb) * _sigmoid(fp)
    k = 1.0 - f
    b = jnp.dot(tril_ref[...], jnp.concatenate(_split3(jnp.log(f)), axis=0), preferred_element_type=F32)

    def seg_decay(m):
        parts = []
        for s0 in range(0, c, 2 * m):
            bp = b[s0 + m:s0 + m + 1, :]
            parts += [bp - b[s0:s0 + m, :], b[s0 + m:s0 + 2 * m, :] - bp]
        return jnp.concatenate(parts, axis=0)

    half = HGRN_DIAG // 2
    dd = jnp.concatenate([b[s0:s0 + HGRN_DIAG, :] - b[s0 + half:s0 + half + 1, :]
                          for s0 in range(0, c, HGRN_DIAG)], axis=0)
    e_b = jnp.exp(b)
    e_last = e_b[c - 1:c, :]
    qa, kb = [], []
    for m in HGRN_LEVELS:
        e_l = jnp.exp(seg_decay(m))
        qa.append((q * e_l).astype(BF16))
        kb.append((k * e_l).astype(BF16))
    qa.append((q * jnp.exp(dd)).astype(BF16))
    kb.append((k * jnp.exp(-dd)).astype(BF16))
    q_in = (q * e_b).astype(BF16)
    k_out = (k * jnp.exp(b[c - 1:c, :] - b)).astype(BF16)
    v16 = iv.astype(BF16)
    masks = _hgrn_masks()

    units = [(ci, h) for ci in range(n_chunks) for h in range(nh)]
    lanes = {u: slice(u[0] * w + u[1] * dh, u[0] * w + (u[1] + 1) * dh) for u in units}
    kv = {u: lax.dot_general(v16[:, lanes[u]], k_out[:, lanes[u]], _TN, preferred_element_type=F32)
          for u in units}
    scores = {}
    for u in units:
        s_u = jnp.zeros((c, c), F32)
        for a_l, b_l, m_l in zip(qa, kb, masks):
            s_l = lax.dot_general(a_l[:, lanes[u]], b_l[:, lanes[u]], _NT, preferred_element_type=F32)
            s_u = jnp.where(m_l, s_l, s_u)
        scores[u] = s_u.astype(BF16)
    inter = {}
    for h in range(nh):
        st = st_ref[h]
        for ci in range(n_chunks):
            u = (ci, h)
            inter[u] = lax.dot_general(q_in[:, lanes[u]], st.astype(BF16), _NT, preferred_element_type=F32)
            st = e_last[:, lanes[u]] * st + kv[u]
        st_ref[h] = st
    gate = tiled(gn_ref[...]) * (gp * _sigmoid(gp))
    for ci in range(n_chunks):
        outs = []
        for h in range(nh):
            u = (ci, h)
            o = jnp.dot(scores[u], v16[:, lanes[u]], preferred_element_type=F32) + inter[u]
            outs.append(o * lax.rsqrt(jnp.mean(o * o, axis=-1, keepdims=True) + EPS))
        on = jnp.concatenate(outs, axis=-1) * gate[:, ci * w:(ci + 1) * w]
        o_ref[0, ci * c:(ci + 1) * c, :] = on.astype(o_ref.dtype)


def _hgrn(hg, lb, gn, rows_per_step):
    b, t, w4 = hg.shape
    w = w4 // 4
    tril = np.tril(np.ones((CHUNK, CHUNK), np.float32))
    tril3 = jnp.asarray(np.concatenate([tril] * 3, axis=1), BF16)
    return pl.pallas_call(
        functools.partial(_hgrn_kernel, n_chunks=rows_per_step // CHUNK),
        grid=(b, t // rows_per_step),
        in_specs=[pl.BlockSpec((1, rows_per_step, w4), lambda bi, i: (bi, i, 0)),
                  _resident((1, w)), _resident((1, w)), _resident(tril3.shape)],
        out_specs=pl.BlockSpec((1, rows_per_step, w), lambda bi, i: (bi, i, 0)),
        out_shape=jax.ShapeDtypeStruct((b, t, w), BF16),
        scratch_shapes=[pltpu.VMEM((HGRN_HEADS, HGRN_D, HGRN_D), F32)],
        compiler_params=_params(("parallel", "arbitrary"), 48),
        name="hgrn2",
    )(hg, lb, gn, tril3)


LOG2E = 1.4426950408889634
SB_SKIP_LOG2 = 160.0


def _sb_kernel(q_ref, k_ref, v_ref, o_ref, *, blk, npair):
    i = pl.program_id(1)
    lane = lax.broadcasted_iota(jnp.int32, (1, LANES), 1)
    first = lane < SB_HEAD_DIM
    t_idx = lax.broadcasted_iota(jnp.int32, (blk, blk), 0)
    s_idx = lax.broadcasted_iota(jnp.int32, (blk, blk), 1)
    upper = (t_idx > s_idx).astype(BF16)
    upper2 = jnp.concatenate([upper, upper], axis=0)
    causal = s_idx < t_idx
    nh = 2 * npair

    qs = []
    for p in range(npair):
        q = q_ref[0, :, p * LANES:(p + 1) * LANES]
        qs += [jnp.where(keep, q, jnp.zeros_like(q)) for keep in (first, jnp.logical_not(first))]

    def sweep(j, accs, carries, diag):
        rows = pl.ds(pl.multiple_of(j * blk, blk), blk)
        ks = [k_ref[0, rows, p * LANES:(p + 1) * LANES] for p in range(npair)]
        vs = [v_ref[0, rows, p * LANES:(p + 1) * LANES] for p in range(npair)]
        zs = [lax.dot_general(qs[h], ks[h // 2], _NT, preferred_element_type=F32) for h in range(nh)]
        logas, laters, firsts = [], [], []
        for z in zs:
            sp = jnp.maximum(z, 0.0) + jnp.log2(1.0 + jnp.exp2(-jnp.abs(z)))
            sp_m = jnp.where(causal, sp, 0.0) if diag else sp
            hi = sp_m.astype(BF16)
            lo = (sp_m - hi.astype(F32)).astype(BF16)
            laters.append(jnp.dot(jnp.concatenate([hi, lo], axis=1), upper2, preferred_element_type=F32))
            logas.append(z - sp)
            firsts.append(sp_m[:, 0:1])
        new_accs, new_carries = [], []
        for h in range(nh):
            a = jnp.exp2(logas[h] - laters[h] - carries[h])
            if diag:
                a = jnp.where(causal, a, 0.0)
            new_accs.append(accs[h] + jnp.dot(a.astype(BF16), vs[h // 2], preferred_element_type=F32))
            new_carries.append(carries[h] + laters[h][:, 0:1] + firsts[h])
        low = functools.reduce(jnp.minimum, new_carries)
        done = (jnp.min(low) >= SB_SKIP_LOG2).astype(jnp.int32)
        return tuple(new_accs), tuple(new_carries), done

    accs = tuple(jnp.zeros((blk, LANES), F32) for _ in range(nh))
    carries = tuple(jnp.zeros((blk, 1), F32) for _ in range(nh))
    accs, carries, done = sweep(i, accs, carries, True)

    def cond(st):
        return jnp.logical_and(st[0] <= i, st[1] == 0)

    def body(st):
        n, _, accs, carries = st
        accs, carries, done = sweep(i - n, accs, carries, False)
        return n + 1, done, accs, carries

    _, _, accs, _ = lax.while_loop(cond, body, (jnp.int32(1), done, accs, carries))
    o_ref[0] = jnp.concatenate([jnp.where(first, accs[2 * p], accs[2 * p + 1]) for p in range(npair)],
                               axis=-1).astype(o_ref.dtype)


def _stickbreak(sb, blk):
    b, t, w3 = sb.shape
    w = w3 // 3
    return pl.pallas_call(
        functools.partial(_sb_kernel, blk=blk, npair=w // LANES),
        grid=(b, t // blk),
        in_specs=[pl.BlockSpec((1, blk, w), lambda bi, i: (bi, i, 0)),
                  pl.BlockSpec((1, t, w), lambda bi, i: (bi, 0, 1)),
                  pl.BlockSpec((1, t, w), lambda bi, i: (bi, 0, 2))],
        out_specs=pl.BlockSpec((1, blk, w), lambda bi, i: (bi, i, 0)),
        out_shape=jax.ShapeDtypeStruct((b, t, w), BF16),
        compiler_params=_params(("parallel", "arbitrary"), 56),
        name="stickbreak",
    )(sb, sb, sb)


def _mix_kernel(x_ref, ya_ref, yb_ref, g_ref, wg_ref, wa_ref, wb_ref, wo_ref, o_ref):
    x = x_ref[...]
    d = x.shape[-1]
    h = _rms(x, g_ref[...]).astype(BF16)
    gate = _sigmoid(jnp.dot(h, wg_ref[...], preferred_element_type=F32))
    pa = jnp.dot(ya_ref[...], wa_ref[...], preferred_element_type=F32)
    pb = jnp.dot(yb_ref[...], wb_ref[...], preferred_element_type=F32)
    merged = gate[:, :d] * pa + gate[:, d:] * pb
    o_ref[...] = x + jnp.dot(merged.astype(BF16), wo_ref[...], preferred_element_type=F32)


def _mix_out(x2d, ya, yb, g, wg, wa, wb, wo, tm):
    m, d = x2d.shape
    wy = ya.shape[1]
    row = lambda n: pl.BlockSpec((tm, n), lambda i: (i, 0))
    return pl.pallas_call(
        _mix_kernel,
        grid=(m // tm,),
        in_specs=[row(d), row(wy), row(wy), _resident((1, d)), _resident(wg.shape),
                  _resident(wa.shape), _resident(wb.shape), _resident(wo.shape)],
        out_specs=row(d),
        out_shape=jax.ShapeDtypeStruct((m, d), F32),
        compiler_params=_params(("parallel",), 48),
        name="mix_out",
    )(x2d, ya, yb, g, wg, wa, wb, wo)


def _xattn_kernel(x_ref, g_ref, wq_ref, k_ref, v_ref, wo_ref, o_ref):
    x = x_ref[0]
    d = x.shape[-1]
    dh = d // XATTN_HEADS
    h = _rms(x, g_ref[...]).astype(BF16)
    q = (jnp.dot(h, wq_ref[...], preferred_element_type=F32) * (dh ** -0.5)).astype(BF16)
    outs = []
    for e in range(XATTN_HEADS):
        hs = slice(e * dh, (e + 1) * dh)
        s = lax.dot_general(q[:, hs], k_ref[0, :, hs], _NT, preferred_element_type=F32)
        p = jnp.exp(s - jnp.max(s, axis=-1, keepdims=True))
        o = jnp.dot(p.astype(BF16), v_ref[0, :, hs], preferred_element_type=F32)
        outs.append(o / jnp.sum(p, axis=-1, keepdims=True))
    o = jnp.concatenate(outs, axis=-1).astype(BF16)
    o_ref[0] = x + jnp.dot(o, wo_ref[...], preferred_element_type=F32)


def _xattn(x3d, g, wq, kmem, vmem, wo, tm):
    b, t, d = x3d.shape
    nm = kmem.shape[1]
    return pl.pallas_call(
        _xattn_kernel,
        grid=(b, t // tm),
        in_specs=[pl.BlockSpec((1, tm, d), lambda bi, i: (bi, i, 0)), _resident((1, d)),
                  _resident(wq.shape),
                  pl.BlockSpec((1, nm, d), lambda bi, i: (bi, 0, 0)),
                  pl.BlockSpec((1, nm, d), lambda bi, i: (bi, 0, 0)),
                  _resident(wo.shape)],
        out_specs=pl.BlockSpec((1, tm, d), lambda bi, i: (bi, i, 0)),
        out_shape=jax.ShapeDtypeStruct((b, t, d), F32),
        compiler_params=_params(("parallel", "parallel"), 48),
        name="xattn",
    )(x3d, g, wq, kmem, vmem, wo)


def _ffn_kernel(x_ref, g_ref, wi_ref, wo_ref, gf_ref, o_ref, *, dff, final_norm):
    x = x_ref[...]
    h = _rms(x, g_ref[...]).astype(BF16)
    gate = jnp.dot(h, wi_ref[:, :dff], preferred_element_type=F32)
    up = jnp.dot(h, wi_ref[:, dff:], preferred_element_type=F32)
    act = (gate * _sigmoid(gate) * up).astype(BF16)
    y = x + jnp.dot(act, wo_ref[...], preferred_element_type=F32)
    o_ref[...] = _rms(y, gf_ref[...]) if final_norm else y


def _ffn(x2d, g, wi, wo, gf, final_norm, tm):
    m, d = x2d.shape
    dff = wo.shape[0]
    row = pl.BlockSpec((tm, d), lambda i: (i, 0))
    return pl.pallas_call(
        functools.partial(_ffn_kernel, dff=dff, final_norm=final_norm),
        grid=(m // tm,),
        in_specs=[row, _resident((1, d)), _resident(wi.shape), _resident(wo.shape), _resident((1, d))],
        out_specs=row,
        out_shape=jax.ShapeDtypeStruct((m, d), F32),
        compiler_params=_params(("parallel",), 56),
        name="ffn",
    )(x2d, g, wi, wo, gf)


def _tile(n, want):
    return want if n % want == 0 else n


def kernel(x, mem, g_mix, w_in, lb_table, g_hgrn, w_gate, w_proj_a, w_proj_b, w_out, g_xattn, g_mem,
           w_xq, w_xkv, w_xo, g_ffn, w_ffn_in, w_ffn_out, g_final):
    b, t, d = x.shape
    depth = w_in.shape[0]
    nm = mem.shape[1]
    hq = HGRN_HEADS * HGRN_D
    sbw = SB_HEADS * SB_HEAD_DIM
    row = lambda v: v.reshape(1, -1).astype(F32)
    lb_all = jnp.cumsum(jax.nn.softmax(lb_table.astype(F32), axis=0), axis=0)

    col_scale = jnp.ones((4 * hq + 3 * sbw,), F32).at[4 * hq:4 * hq + sbw].set(SB_HEAD_DIM ** -0.5 * LOG2E)

    x2d = x.reshape(b * t, d)
    for l in range(depth):
        hg, sb = _norm_matmul(x2d, row(g_mix[l]), (w_in[l] * col_scale).astype(BF16),
                              ((0, 4 * hq), (4 * hq, 4 * hq + 3 * sbw)), (F32, BF16), _tile(b * t, 512))
        ya = _hgrn(hg.reshape(b, t, 4 * hq), row(lb_all[l]), row(g_hgrn[l]), _tile(t, 512))
        yb = _stickbreak(sb.reshape(b, t, 3 * sbw), _tile(t, 256))
        x2d = _mix_out(x2d, ya.reshape(b * t, hq), yb.reshape(b * t, sbw), row(g_mix[l]),
                       w_gate[l].astype(BF16), w_proj_a[l].astype(BF16), w_proj_b[l].astype(BF16),
                       w_out[l].astype(BF16), _tile(b * t, 512))
        kmem, vmem = _norm_matmul(mem.reshape(b * nm, d), row(g_mem[l]), w_xkv[l].astype(BF16),
                                  ((0, d), (d, 2 * d)), (BF16, BF16), _tile(b * nm, 256))
        x2d = _xattn(x2d.reshape(b, t, d), row(g_xattn[l]), w_xq[l].astype(BF16),
                     kmem.reshape(b, nm, d), vmem.reshape(b, nm, d), w_xo[l].astype(BF16),
                     _tile(t, 512)).reshape(b * t, d)
        x2d = _ffn(x2d, row(g_ffn[l]), w_ffn_in[l].astype(BF16), w_ffn_out[l].astype(BF16),
                   row(g_final), l == depth - 1, _tile(b * t, 256))
    return x2d.reshape(b, t, d)
```

```python
import functools

import jax
import jax.numpy as jnp
import numpy as np
from jax import lax
from jax.experimental import pallas as pl
from jax.experimental.pallas import tpu as pltpu

F32 = jnp.float32
BF16 = jnp.bfloat16
EPS = 1e-6

CHUNK = 64
HGRN_HEADS = 4
HGRN_D = 128
SB_HEADS = 8
SB_HEAD_DIM = 64
XATTN_HEADS = 4

V7X_VMEM_BYTES = 64 * 1024 * 1024
LANES = 128

_NT = (((1,), (1,)), ((), ()))
_TN = (((0,), (0,)), ((), ()))


def _rms(x, g):
    return x * lax.rsqrt(jnp.mean(x * x, axis=-1, keepdims=True) + EPS) * g


def _sigmoid(x):
    return 1.0 / (1.0 + jnp.exp(-x))


def _params(sem, vmem_mb):
    return pltpu.CompilerParams(dimension_semantics=sem, vmem_limit_bytes=vmem_mb * 1024 * 1024)


def _resident(shape):
    nd = len(shape)
    return pl.BlockSpec(shape, lambda *_: (0,) * nd, pipeline_mode=pl.Buffered(1))


def _norm_matmul_kernel(x_ref, g_ref, w_ref, *out_refs, splits):
    h = _rms(x_ref[...], g_ref[...]).astype(BF16)
    for o_ref, (lo, hi) in zip(out_refs, splits):
        o_ref[...] = jnp.dot(h, w_ref[:, lo:hi], preferred_element_type=F32).astype(o_ref.dtype)


def _norm_matmul(x2d, g, w, splits, dtypes, tm):
    m, d = x2d.shape
    n = w.shape[1]
    return pl.pallas_call(
        functools.partial(_norm_matmul_kernel, splits=splits),
        grid=(m // tm,),
        in_specs=[pl.BlockSpec((tm, d), lambda i: (i, 0)), _resident((1, d)), _resident((d, n))],
        out_specs=[pl.BlockSpec((tm, hi - lo), lambda i: (i, 0)) for lo, hi in splits],
        out_shape=[jax.ShapeDtypeStruct((m, hi - lo), dt) for (lo, hi), dt in zip(splits, dtypes)],
        compiler_params=_params(("parallel",), 48),
        name="norm_matmul",
    )(x2d, g, w)


HGRN_LEVELS = (32, 16, 8)
HGRN_DIAG = 8


def _hgrn_masks():
    c = CHUNK
    t = lax.broadcasted_iota(jnp.int32, (c, c), 0)
    s = lax.broadcasted_iota(jnp.int32, (c, c), 1)
    masks = []
    for m in HGRN_LEVELS:
        same = (t // (2 * m)) == (s // (2 * m))
        masks.append(same & ((t % (2 * m)) >= m) & ((s % (2 * m)) < m))
    masks.append(((t // HGRN_DIAG) == (s // HGRN_DIAG)) & (s <= t))
    return masks


def _split3(x):
    hi = x.astype(BF16)
    r = x - hi.astype(F32)
    mid = r.astype(BF16)
    lo = (r - mid.astype(F32)).astype(BF16)
    return hi, mid, lo


def _hgrn_kernel(hg_ref, lb_ref, gn_ref, tril_ref, o_ref, st_ref, *, n_chunks):
    c, nh, dh = CHUNK, HGRN_HEADS, HGRN_D
    w = nh * dh

    @pl.when(pl.program_id(1) == 0)
    def _():
        st_ref[...] = jnp.zeros_like(st_ref)

    def wide(col):
        return jnp.concatenate(
            [hg_ref[0, ci * c:(ci + 1) * c, col * w:(col + 1) * w] for ci in range(n_chunks)], axis=1)

    def tiled(row):
        return jnp.concatenate([row] * n_chunks, axis=1)

    qp, fp, iv, gp = wide(0), wide(1), wide(2), wide(3)
    lb = tiled(lb_ref[...])
    q = qp * _sigmoid(qp)
    f = lb + (1.0 - lb) * _sigmoid(fp)
    k = 1.0 - f
    b = jnp.dot(tril_ref[...], jnp.concatenate(_split3(jnp.log(f)), axis=0), preferred_element_type=F32)

    def seg_decay(m):
        parts = []
        for s0 in range(0, c, 2 * m):
            bp = b[s0 + m:s0 + m + 1, :]
            parts += [bp - b[s0:s0 + m, :], b[s0 + m:s0 + 2 * m, :] - bp]
        return jnp.concatenate(parts, axis=0)

    half = HGRN_DIAG // 2
    dd = jnp.concatenate([b[s0:s0 + HGRN_DIAG, :] - b[s0 + half:s0 + half + 1, :]
                          for s0 in range(0, c, HGRN_DIAG)], axis=0)
    e_b = jnp.exp(b)
    e_last = e_b[c - 1:c, :]
    qa, kb = [], []
    for m in HGRN_LEVELS:
        e_l = jnp.exp(seg_decay(m))
        qa.append((q * e_l).astype(BF16))
        kb.append((k * e_l).astype(BF16))
    qa.append((q * jnp.exp(dd)).astype(BF16))
    kb.append((k * jnp.exp(-dd)).astype(BF16))
    q_in = (q * e_b).astype(BF16)
    k_out = (k * jnp.exp(b[c - 1:c, :] - b)).astype(BF16)
    v16 = iv.astype(BF16)
    masks = _hgrn_masks()

    units = [(ci, h) for ci in range(n_chunks) for h in range(nh)]
    lanes = {u: slice(u[0] * w + u[1] * dh, u[0] * w + (u[1] + 1) * dh) for u in units}
    kv = {u: lax.dot_general(v16[:, lanes[u]], k_out[:, lanes[u]], _TN, preferred_element_type=F32)
          for u in units}
    scores = {}
    for u in units:
        s_u = jnp.zeros((c, c), F32)
        for a_l, b_l, m_l in zip(qa, kb, masks):
            s_l = lax.dot_general(a_l[:, lanes[u]], b_l[:, lanes[u]], _NT, preferred_element_type=F32)
            s_u = jnp.where(m_l, s_l, s_u)
        scores[u] = s_u.astype(BF16)
    inter = {}
    for h in range(nh):
        st = st_ref[h]
        for ci in range(n_chunks):
            u = (ci, h)
            inter[u] = lax.dot_general(q_in[:, lanes[u]], st.astype(BF16), _NT, preferred_element_type=F32)
            st = e_last[:, lanes[u]] * st + kv[u]
        st_ref[h] = st
    gate = tiled(gn_ref[...]) * (gp * _sigmoid(gp))
    for ci in range(n_chunks):
        outs = []
        for h in range(nh):
            u = (ci, h)
            o = jnp.dot(scores[u], v16[:, lanes[u]], preferred_element_type=F32) + inter[u]
            outs.append(o * lax.rsqrt(jnp.mean(o * o, axis=-1, keepdims=True) + EPS))
        on = jnp.concatenate(outs, axis=-1) * gate[:, ci * w:(ci + 1) * w]
        o_ref[0, ci * c:(ci + 1) * c, :] = on.astype(o_ref.dtype)


def _hgrn(hg, lb, gn, rows_per_step):
    b, t, w4 = hg.shape
    w = w4 // 4
    tril = np.tril(np.ones((CHUNK, CHUNK), np.float32))
    tril3 = jnp.asarray(np.concatenate([tril] * 3, axis=1), BF16)
    return pl.pallas_call(
        functools.partial(_hgrn_kernel, n_chunks=rows_per_step // CHUNK),
        grid=(b, t // rows_per_step),
        in_specs=[pl.BlockSpec((1, rows_per_step, w4), lambda bi, i: (bi, i, 0)),
                  _resident((1, w)), _resident((1, w)), _resident(tril3.shape)],
        out_specs=pl.BlockSpec((1, rows_per_step, w), lambda bi, i: (bi, i, 0)),
        out_shape=jax.ShapeDtypeStruct((b, t, w), BF16),
        scratch_shapes=[pltpu.VMEM((HGRN_HEADS, HGRN_D, HGRN_D), F32)],
        compiler_params=_params(("parallel", "arbitrary"), 48),
        name="hgrn2",
    )(hg, lb, gn, tril3)


LOG2E = 1.4426950408889634
SB_SKIP_LOG2 = 160.0
SB_TERMS = 1


def _bf16_terms(x, n):
    terms, r = [], x
    for i in range(n):
        t = r.astype(BF16)
        terms.append(t)
        if i + 1 < n:
            r = r - t.astype(F32)
    return terms[0] if n == 1 else jnp.concatenate(terms, axis=-1)


def _sb_kernel(q_ref, k_ref, v_ref, o_ref, *, blk, npair):
    i = pl.program_id(1)
    lane = lax.broadcasted_iota(jnp.int32, (1, LANES), 1)
    first = lane < SB_HEAD_DIM
    t_idx = lax.broadcasted_iota(jnp.int32, (blk, blk), 0)
    s_idx = lax.broadcasted_iota(jnp.int32, (blk, blk), 1)
    suffix = jnp.concatenate([(t_idx >= s_idx).astype(BF16)] * SB_TERMS, axis=0)
    causal = s_idx < t_idx
    nh = 2 * npair

    qs = []
    for p in range(npair):
        q = q_ref[0, :, p * LANES:(p + 1) * LANES]
        qs += [jnp.where(keep, q, jnp.zeros_like(q)) for keep in (first, jnp.logical_not(first))]

    def sweep(j, accs, carries, diag):
        rows = pl.ds(pl.multiple_of(j * blk, blk), blk)
        ks = [k_ref[0, rows, p * LANES:(p + 1) * LANES] for p in range(npair)]
        vs = [v_ref[0, rows, p * LANES:(p + 1) * LANES] for p in range(npair)]
        zs = [lax.dot_general(qs[h], ks[h // 2], _NT, preferred_element_type=F32) for h in range(nh)]
        tails = []
        for z in zs:
            sp = jnp.maximum(z, 0.0) + jnp.log2(1.0 + jnp.exp2(-jnp.abs(z)))
            if diag:
                sp = jnp.where(causal, sp, 0.0)
            tails.append(jnp.dot(_bf16_terms(sp, SB_TERMS), suffix, preferred_element_type=F32))
        new_accs, new_carries = [], []
        for h in range(nh):
            a = jnp.exp2(zs[h] - tails[h] - carries[h])
            if diag:
                a = jnp.where(causal, a, 0.0)
            new_accs.append(accs[h] + jnp.dot(a.astype(BF16), vs[h // 2], preferred_element_type=F32))
            new_carries.append(carries[h] + tails[h][:, 0:1])
        low = functools.reduce(jnp.minimum, new_carries)
        done = (jnp.min(low) >= SB_SKIP_LOG2).astype(jnp.int32)
        return tuple(new_accs), tuple(new_carries), done

    accs = tuple(jnp.zeros((blk, LANES), F32) for _ in range(nh))
    carries = tuple(jnp.zeros((blk, 1), F32) for _ in range(nh))
    accs, carries, done = sweep(i, accs, carries, True)

    def cond(st):
        return jnp.logical_and(st[0] <= i, st[1] == 0)

    def body(st):
        n, _, accs, carries = st
        accs, carries, done = sweep(i - n, accs, carries, False)
        return n + 1, done, accs, carries

    _, _, accs, _ = lax.while_loop(cond, body, (jnp.int32(1), done, accs, carries))
    o_ref[0] = jnp.concatenate([jnp.where(first, accs[2 * p], accs[2 * p + 1]) for p in range(npair)],
                               axis=-1).astype(o_ref.dtype)


def _stickbreak(sb, blk):
    b, t, w3 = sb.shape
    w = w3 // 3
    return pl.pallas_call(
        functools.partial(_sb_kernel, blk=blk, npair=w // LANES),
        grid=(b, t // blk),
        in_specs=[pl.BlockSpec((1, blk, w), lambda bi, i: (bi, i, 0)),
                  pl.BlockSpec((1, t, w), lambda bi, i: (bi, 0, 1)),
                  pl.BlockSpec((1, t, w), lambda bi, i: (bi, 0, 2))],
        out_specs=pl.BlockSpec((1, blk, w), lambda bi, i: (bi, i, 0)),
        out_shape=jax.ShapeDtypeStruct((b, t, w), BF16),
        compiler_params=_params(("parallel", "arbitrary"), 56),
        name="stickbreak",
    )(sb, sb, sb)


def _mix_kernel(x_ref, ya_ref, yb_ref, g_ref, wg_ref, wa_ref, wb_ref, wo_ref, o_ref):
    x = x_ref[...]
    d = x.shape[-1]
    h = _rms(x, g_ref[...]).astype(BF16)
    gate = _sigmoid(jnp.dot(h, wg_ref[...], preferred_element_type=F32))
    pa = jnp.dot(ya_ref[...], wa_ref[...], preferred_element_type=F32)
    pb = jnp.dot(yb_ref[...], wb_ref[...], preferred_element_type=F32)
    merged = gate[:, :d] * pa + gate[:, d:] * pb
    o_ref[...] = x + jnp.dot(merged.astype(BF16), wo_ref[...], preferred_element_type=F32)


def _mix_out(x2d, ya, yb, g, wg, wa, wb, wo, tm):
    m, d = x2d.shape
    wy = ya.shape[1]
    row = lambda n: pl.BlockSpec((tm, n), lambda i: (i, 0))
    return pl.pallas_call(
        _mix_kernel,
        grid=(m // tm,),
        in_specs=[row(d), row(wy), row(wy), _resident((1, d)), _resident(wg.shape),
                  _resident(wa.shape), _resident(wb.shape), _resident(wo.shape)],
        out_specs=row(d),
        out_shape=jax.ShapeDtypeStruct((m, d), F32),
        compiler_params=_params(("parallel",), 48),
        name="mix_out",
    )(x2d, ya, yb, g, wg, wa, wb, wo)


def _xattn_kernel(x_ref, g_ref, wq_ref, k_ref, v_ref, wo_ref, o_ref):
    x = x_ref[0]
    d = x.shape[-1]
    dh = d // XATTN_HEADS
    h = _rms(x, g_ref[...]).astype(BF16)
    q = (jnp.dot(h, wq_ref[...], preferred_element_type=F32) * (dh ** -0.5)).astype(BF16)
    outs = []
    for e in range(XATTN_HEADS):
        hs = slice(e * dh, (e + 1) * dh)
        s = lax.dot_general(q[:, hs], k_ref[0, :, hs], _NT, preferred_element_type=F32)
        p = jnp.exp(s - jnp.max(s, axis=-1, keepdims=True))
        o = jnp.dot(p.astype(BF16), v_ref[0, :, hs], preferred_element_type=F32)
        outs.append(o / jnp.sum(p, axis=-1, keepdims=True))
    o = jnp.concatenate(outs, axis=-1).astype(BF16)
    o_ref[0] = x + jnp.dot(o, wo_ref[...], preferred_element_type=F32)


def _xattn(x3d, g, wq, kmem, vmem, wo, tm):
    b, t, d = x3d.shape
    nm = kmem.shape[1]
    return pl.pallas_call(
        _xattn_kernel,
        grid=(b, t // tm),
        in_specs=[pl.BlockSpec((1, tm, d), lambda bi, i: (bi, i, 0)), _resident((1, d)),
                  _resident(wq.shape),
                  pl.BlockSpec((1, nm, d), lambda bi, i: (bi, 0, 0)),
                  pl.BlockSpec((1, nm, d), lambda bi, i: (bi, 0, 0)),
                  _resident(wo.shape)],
        out_specs=pl.BlockSpec((1, tm, d), lambda bi, i: (bi, i, 0)),
        out_shape=jax.ShapeDtypeStruct((b, t, d), F32),
        compiler_params=_params(("parallel", "parallel"), 48),
        name="xattn",
    )(x3d, g, wq, kmem, vmem, wo)


def _ffn_kernel(x_ref, g_ref, wi_ref, wo_ref, gf_ref, o_ref, *, dff, final_norm):
    x = x_ref[...]
    h = _rms(x, g_ref[...]).astype(BF16)
    gate = jnp.dot(h, wi_ref[:, :dff], preferred_element_type=F32)
    up = jnp.dot(h, wi_ref[:, dff:], preferred_element_type=F32)
    act = (gate * _sigmoid(gate) * up).astype(BF16)
    y = x + jnp.dot(act, wo_ref[...], preferred_element_type=F32)
    o_ref[...] = _rms(y, gf_ref[...]) if final_norm else y


def _ffn(x2d, g, wi, wo, gf, final_norm, tm):
    m, d = x2d.shape
    dff = wo.shape[0]
    row = pl.BlockSpec((tm, d), lambda i: (i, 0))
    return pl.pallas_call(
        functools.partial(_ffn_kernel, dff=dff, final_norm=final_norm),
        grid=(m // tm,),
        in_specs=[row, _resident((1, d)), _resident(wi.shape), _resident(wo.shape), _resident((1, d))],
        out_specs=row,
        out_shape=jax.ShapeDtypeStruct((m, d), F32),
        compiler_params=_params(("parallel",), 56),
        name="ffn",
    )(x2d, g, wi, wo, gf)


def _tile(n, want):
    return want if n % want == 0 else n


def kernel(x, mem, g_mix, w_in, lb_table, g_hgrn, w_gate, w_proj_a, w_proj_b, w_out, g_xattn, g_mem,
           w_xq, w_xkv, w_xo, g_ffn, w_ffn_in, w_ffn_out, g_final):
    b, t, d = x.shape
    depth = w_in.shape[0]
    nm = mem.shape[1]
    hq = HGRN_HEADS * HGRN_D
    sbw = SB_HEADS * SB_HEAD_DIM
    row = lambda v: v.reshape(1, -1).astype(F32)
    lb_all = jnp.cumsum(jax.nn.softmax(lb_table.astype(F32), axis=0), axis=0)

    col_scale = jnp.ones((4 * hq + 3 * sbw,), F32).at[4 * hq:4 * hq + sbw].set(SB_HEAD_DIM ** -0.5 * LOG2E)

    x2d = x.reshape(b * t, d)
    for l in range(depth):
        hg, sb = _norm_matmul(x2d, row(g_mix[l]), (w_in[l] * col_scale).astype(BF16),
                              ((0, 4 * hq), (4 * hq, 4 * hq + 3 * sbw)), (F32, BF16), _tile(b * t, 512))
        ya = _hgrn(hg.reshape(b, t, 4 * hq), row(lb_all[l]), row(g_hgrn[l]), _tile(t, 512))
        yb = _stickbreak(sb.reshape(b, t, 3 * sbw), _tile(t, 256))
        x2d = _mix_out(x2d, ya.reshape(b * t, hq), yb.reshape(b * t, sbw), row(g_mix[l]),
                       w_gate[l].astype(BF16), w_proj_a[l].astype(BF16), w_proj_b[l].astype(BF16),
                       w_out[l].astype(BF16), _tile(b * t, 512))
        kmem, vmem = _norm_matmul(mem.reshape(b * nm, d), row(g_mem[l]), w_xkv[l].astype(BF16),
                                  ((0, d), (d, 2 * d)), (BF16, BF16), _tile(b * nm, 256))
        x2d = _xattn(x2d.reshape(b, t, d), row(g_xattn[l]), w_xq[l].astype(BF16),
                     kmem.reshape(b, nm, d), vmem.reshape(b, nm, d), w_xo[l].astype(BF16),
                     _tile(t, 512)).reshape(b * t, d)
        x2d = _ffn(x2d, row(g_ffn[l]), w_ffn_in[l].astype(BF16), w_ffn_out[l].astype(BF16),
                   row(g_final), l == depth - 1, _tile(b * t, 256))
    return x2d.reshape(b, t, d)
```

```python
import functools

import jax
import jax.numpy as jnp
import numpy as np
from jax import lax
from jax.experimental import pallas as pl
from jax.experimental.pallas import tpu as pltpu

F32 = jnp.float32
BF16 = jnp.bfloat16
EPS = 1e-6

CHUNK = 64
HGRN_HEADS = 4
HGRN_D = 128
SB_HEADS = 8
SB_HEAD_DIM = 64
XATTN_HEADS = 4

V7X_VMEM_BYTES = 64 * 1024 * 1024
LANES = 128

_NT = (((1,), (1,)), ((), ()))
_TN = (((0,), (0,)), ((), ()))


def _rms(x, g):
    return x * lax.rsqrt(jnp.mean(x * x, axis=-1, keepdims=True) + EPS) * g


def _sigmoid(x):
    return 1.0 / (1.0 + jnp.exp(-x))


def _params(sem, vmem_mb):
    return pltpu.CompilerParams(dimension_semantics=sem, vmem_limit_bytes=vmem_mb * 1024 * 1024)


def _resident(shape):
    nd = len(shape)
    return pl.BlockSpec(shape, lambda *_: (0,) * nd, pipeline_mode=pl.Buffered(1))


def _norm_matmul_kernel(x_ref, g_ref, w_ref, *out_refs, splits, sub):
    for r in range(0, x_ref.shape[0], sub):
        h = _rms(x_ref[r:r + sub, :], g_ref[...]).astype(BF16)
        for o_ref, (lo, hi) in zip(out_refs, splits):
            o_ref[r:r + sub, :] = jnp.dot(h, w_ref[:, lo:hi], preferred_element_type=F32).astype(o_ref.dtype)


def _norm_matmul(x2d, g, w, splits, dtypes, tm, sub):
    m, d = x2d.shape
    n = w.shape[1]
    return pl.pallas_call(
        functools.partial(_norm_matmul_kernel, splits=splits, sub=sub),
        grid=(m // tm,),
        in_specs=[pl.BlockSpec((tm, d), lambda i: (i, 0)), _resident((1, d)), _resident((d, n))],
        out_specs=[pl.BlockSpec((tm, hi - lo), lambda i: (i, 0)) for lo, hi in splits],
        out_shape=[jax.ShapeDtypeStruct((m, hi - lo), dt) for (lo, hi), dt in zip(splits, dtypes)],
        compiler_params=_params(("parallel",), 48),
        name="norm_matmul",
    )(x2d, g, w)


HGRN_LEVELS = (32, 16, 8)
HGRN_DIAG = 8


def _hgrn_masks():
    c = CHUNK
    t = lax.broadcasted_iota(jnp.int32, (c, c), 0)
    s = lax.broadcasted_iota(jnp.int32, (c, c), 1)
    masks = []
    for m in HGRN_LEVELS:
        same = (t // (2 * m)) == (s // (2 * m))
        masks.append(same & ((t % (2 * m)) >= m) & ((s % (2 * m)) < m))
    masks.append(((t // HGRN_DIAG) == (s // HGRN_DIAG)) & (s <= t))
    return masks


def _split3(x):
    hi = x.astype(BF16)
    r = x - hi.astype(F32)
    mid = r.astype(BF16)
    lo = (r - mid.astype(F32)).astype(BF16)
    return hi, mid, lo


def _hgrn_kernel(hg_ref, lb_ref, gn_ref, tril_ref, o_ref, st_ref, *, n_chunks):
    c, nh, dh = CHUNK, HGRN_HEADS, HGRN_D
    w = nh * dh

    @pl.when(pl.program_id(1) == 0)
    def _():
        st_ref[...] = jnp.zeros_like(st_ref)

    def wide(col):
        return jnp.concatenate(
            [hg_ref[0, ci * c:(ci + 1) * c, col * w:(col + 1) * w] for ci in range(n_chunks)], axis=1)

    def tiled(row):
        return jnp.concatenate([row] * n_chunks, axis=1)

    qp, fp, iv, gp = wide(0), wide(1), wide(2), wide(3)
    lb = tiled(lb_ref[...])
    q = qp * _sigmoid(qp)
    f = lb + (1.0 - lb) * _sigmoid(fp)
    k = 1.0 - f
    b = jnp.dot(tril_ref[...], jnp.concatenate(_split3(jnp.log(f)), axis=0), preferred_element_type=F32)

    def seg_decay(m):
        parts = []
        for s0 in range(0, c, 2 * m):
            bp = b[s0 + m:s0 + m + 1, :]
            parts += [bp - b[s0:s0 + m, :], b[s0 + m:s0 + 2 * m, :] - bp]
        return jnp.concatenate(parts, axis=0)

    half = HGRN_DIAG // 2
    dd = jnp.concatenate([b[s0:s0 + HGRN_DIAG, :] - b[s0 + half:s0 + half + 1, :]
                          for s0 in range(0, c, HGRN_DIAG)], axis=0)
    e_b = jnp.exp(b)
    e_last = e_b[c - 1:c, :]
    qa, kb = [], []
    for m in HGRN_LEVELS:
        e_l = jnp.exp(seg_decay(m))
        qa.append((q * e_l).astype(BF16))
        kb.append((k * e_l).astype(BF16))
    qa.append((q * jnp.exp(dd)).astype(BF16))
    kb.append((k * jnp.exp(-dd)).astype(BF16))
    q_in = (q * e_b).astype(BF16)
    k_out = (k * jnp.exp(b[c - 1:c, :] - b)).astype(BF16)
    v16 = iv.astype(BF16)
    masks = _hgrn_masks()

    units = [(ci, h) for ci in range(n_chunks) for h in range(nh)]
    lanes = {u: slice(u[0] * w + u[1] * dh, u[0] * w + (u[1] + 1) * dh) for u in units}
    kv = {u: lax.dot_general(v16[:, lanes[u]], k_out[:, lanes[u]], _TN, preferred_element_type=F32)
          for u in units}
    scores = {}
    for u in units:
        s_u = jnp.zeros((c, c), F32)
        for a_l, b_l, m_l in zip(qa, kb, masks):
            s_l = lax.dot_general(a_l[:, lanes[u]], b_l[:, lanes[u]], _NT, preferred_element_type=F32)
            s_u = jnp.where(m_l, s_l, s_u)
        scores[u] = s_u.astype(BF16)
    inter = {}
    for h in range(nh):
        st = st_ref[h]
        for ci in range(n_chunks):
            u = (ci, h)
            inter[u] = lax.dot_general(q_in[:, lanes[u]], st.astype(BF16), _NT, preferred_element_type=F32)
            st = e_last[:, lanes[u]] * st + kv[u]
        st_ref[h] = st
    gate = tiled(gn_ref[...]) * (gp * _sigmoid(gp))
    for ci in range(n_chunks):
        outs = []
        for h in range(nh):
            u = (ci, h)
            o = jnp.dot(scores[u], v16[:, lanes[u]], preferred_element_type=F32) + inter[u]
            outs.append(o * lax.rsqrt(jnp.mean(o * o, axis=-1, keepdims=True) + EPS))
        on = jnp.concatenate(outs, axis=-1) * gate[:, ci * w:(ci + 1) * w]
        o_ref[0, ci * c:(ci + 1) * c, :] = on.astype(o_ref.dtype)


def _hgrn(hg, lb, gn, rows_per_step):
    b, t, w4 = hg.shape
    w = w4 // 4
    tril = np.tril(np.ones((CHUNK, CHUNK), np.float32))
    tril3 = jnp.asarray(np.concatenate([tril] * 3, axis=1), BF16)
    return pl.pallas_call(
        functools.partial(_hgrn_kernel, n_chunks=rows_per_step // CHUNK),
        grid=(b, t // rows_per_step),
        in_specs=[pl.BlockSpec((1, rows_per_step, w4), lambda bi, i: (bi, i, 0)),
                  _resident((1, w)), _resident((1, w)), _resident(tril3.shape)],
        out_specs=pl.BlockSpec((1, rows_per_step, w), lambda bi, i: (bi, i, 0)),
        out_shape=jax.ShapeDtypeStruct((b, t, w), BF16),
        scratch_shapes=[pltpu.VMEM((HGRN_HEADS, HGRN_D, HGRN_D), F32)],
        compiler_params=_params(("parallel", "arbitrary"), 48),
        name="hgrn2",
    )(hg, lb, gn, tril3)


LOG2E = 1.4426950408889634
SB_SKIP_LOG2 = 160.0
SB_TERMS = 1


def _bf16_terms(x, n):
    terms, r = [], x
    for i in range(n):
        t = r.astype(BF16)
        terms.append(t)
        if i + 1 < n:
            r = r - t.astype(F32)
    return terms[0] if n == 1 else jnp.concatenate(terms, axis=-1)


def _sb_kernel(q_ref, k_ref, v_ref, o_ref, *, blk, npair):
    i = pl.program_id(1)
    lane = lax.broadcasted_iota(jnp.int32, (1, LANES), 1)
    first = lane < SB_HEAD_DIM
    t_idx = lax.broadcasted_iota(jnp.int32, (blk, blk), 0)
    s_idx = lax.broadcasted_iota(jnp.int32, (blk, blk), 1)
    suffix = jnp.concatenate([(t_idx >= s_idx).astype(BF16)] * SB_TERMS, axis=0)
    causal = s_idx < t_idx
    nh = 2 * npair

    qs = []
    for p in range(npair):
        q = q_ref[0, :, p * LANES:(p + 1) * LANES]
        qs += [jnp.where(keep, q, jnp.zeros_like(q)) for keep in (first, jnp.logical_not(first))]

    def sweep(j, accs, carries, diag):
        rows = pl.ds(pl.multiple_of(j * blk, blk), blk)
        ks = [k_ref[0, rows, p * LANES:(p + 1) * LANES] for p in range(npair)]
        vs = [v_ref[0, rows, p * LANES:(p + 1) * LANES] for p in range(npair)]
        zs = [lax.dot_general(qs[h], ks[h // 2], _NT, preferred_element_type=F32) for h in range(nh)]
        tails = []
        for z in zs:
            sp = jnp.maximum(z, 0.0) + jnp.log2(1.0 + jnp.exp2(-jnp.abs(z)))
            if diag:
                sp = jnp.where(causal, sp, 0.0)
            tails.append(jnp.dot(_bf16_terms(sp, SB_TERMS), suffix, preferred_element_type=F32))
        new_accs, new_carries = [], []
        for h in range(nh):
            a = jnp.exp2(zs[h] - tails[h] - carries[h])
            if diag:
                a = jnp.where(causal, a, 0.0)
            new_accs.append(accs[h] + jnp.dot(a.astype(BF16), vs[h // 2], preferred_element_type=F32))
            new_carries.append(carries[h] + tails[h][:, 0:1])
        low = functools.reduce(jnp.minimum, new_carries)
        done = (jnp.min(low) >= SB_SKIP_LOG2).astype(jnp.int32)
        return tuple(new_accs), tuple(new_carries), done

    accs = tuple(jnp.zeros((blk, LANES), F32) for _ in range(nh))
    carries = tuple(jnp.zeros((blk, 1), F32) for _ in range(nh))
    accs, carries, done = sweep(i, accs, carries, True)

    def cond(st):
        return jnp.logical_and(st[0] <= i, st[1] == 0)

    def body(st):
        n, _, accs, carries = st
        accs, carries, done = sweep(i - n, accs, carries, False)
        return n + 1, done, accs, carries

    _, _, accs, _ = lax.while_loop(cond, body, (jnp.int32(1), done, accs, carries))
    o_ref[0] = jnp.concatenate([jnp.where(first, accs[2 * p], accs[2 * p + 1]) for p in range(npair)],
                               axis=-1).astype(o_ref.dtype)


def _stickbreak(sb, blk):
    b, t, w3 = sb.shape
    w = w3 // 3
    return pl.pallas_call(
        functools.partial(_sb_kernel, blk=blk, npair=w // LANES),
        grid=(b, t // blk),
        in_specs=[pl.BlockSpec((1, blk, w), lambda bi, i: (bi, i, 0)),
                  pl.BlockSpec((1, t, w), lambda bi, i: (bi, 0, 1)),
                  pl.BlockSpec((1, t, w), lambda bi, i: (bi, 0, 2))],
        out_specs=pl.BlockSpec((1, blk, w), lambda bi, i: (bi, i, 0)),
        out_shape=jax.ShapeDtypeStruct((b, t, w), BF16),
        compiler_params=_params(("parallel", "arbitrary"), 56),
        name="stickbreak",
    )(sb, sb, sb)


def _mix_kernel(x_ref, ya_ref, yb_ref, g_ref, wg_ref, wa_ref, wb_ref, wo_ref, o_ref, *, sub):
    d = x_ref.shape[-1]
    for r in range(0, x_ref.shape[0], sub):
        rows = slice(r, r + sub)
        x = x_ref[rows, :]
        h = _rms(x, g_ref[...]).astype(BF16)
        gate = _sigmoid(jnp.dot(h, wg_ref[...], preferred_element_type=F32))
        pa = jnp.dot(ya_ref[rows, :], wa_ref[...], preferred_element_type=F32)
        pb = jnp.dot(yb_ref[rows, :], wb_ref[...], preferred_element_type=F32)
        merged = gate[:, :d] * pa + gate[:, d:] * pb
        o_ref[rows, :] = x + jnp.dot(merged.astype(BF16), wo_ref[...], preferred_element_type=F32)


def _mix_out(x2d, ya, yb, g, wg, wa, wb, wo, tm, sub):
    m, d = x2d.shape
    wy = ya.shape[1]
    row = lambda n: pl.BlockSpec((tm, n), lambda i: (i, 0))
    return pl.pallas_call(
        functools.partial(_mix_kernel, sub=sub),
        grid=(m // tm,),
        in_specs=[row(d), row(wy), row(wy), _resident((1, d)), _resident(wg.shape),
                  _resident(wa.shape), _resident(wb.shape), _resident(wo.shape)],
        out_specs=row(d),
        out_shape=jax.ShapeDtypeStruct((m, d), F32),
        compiler_params=_params(("parallel",), 48),
        name="mix_out",
    )(x2d, ya, yb, g, wg, wa, wb, wo)


def _xattn_kernel(x_ref, g_ref, wq_ref, k_ref, v_ref, wo_ref, o_ref, *, sub):
    d = x_ref.shape[-1]
    dh = d // XATTN_HEADS
    for r in range(0, x_ref.shape[1], sub):
        x = x_ref[0, r:r + sub, :]
        h = _rms(x, g_ref[...]).astype(BF16)
        q = (jnp.dot(h, wq_ref[...], preferred_element_type=F32) * (dh ** -0.5)).astype(BF16)
        outs = []
        for e in range(XATTN_HEADS):
            hs = slice(e * dh, (e + 1) * dh)
            s = lax.dot_general(q[:, hs], k_ref[0, :, hs], _NT, preferred_element_type=F32)
            p = jnp.exp(s - jnp.max(s, axis=-1, keepdims=True))
            o = jnp.dot(p.astype(BF16), v_ref[0, :, hs], preferred_element_type=F32)
            outs.append(o / jnp.sum(p, axis=-1, keepdims=True))
        o = jnp.concatenate(outs, axis=-1).astype(BF16)
        o_ref[0, r:r + sub, :] = x + jnp.dot(o, wo_ref[...], preferred_element_type=F32)


def _xattn(x3d, g, wq, kmem, vmem, wo, tm, sub):
    b, t, d = x3d.shape
    nm = kmem.shape[1]
    return pl.pallas_call(
        functools.partial(_xattn_kernel, sub=sub),
        grid=(b, t // tm),
        in_specs=[pl.BlockSpec((1, tm, d), lambda bi, i: (bi, i, 0)), _resident((1, d)),
                  _resident(wq.shape),
                  pl.BlockSpec((1, nm, d), lambda bi, i: (bi, 0, 0)),
                  pl.BlockSpec((1, nm, d), lambda bi, i: (bi, 0, 0)),
                  _resident(wo.shape)],
        out_specs=pl.BlockSpec((1, tm, d), lambda bi, i: (bi, i, 0)),
        out_shape=jax.ShapeDtypeStruct((b, t, d), F32),
        compiler_params=_params(("parallel", "parallel"), 48),
        name="xattn",
    )(x3d, g, wq, kmem, vmem, wo)


def _ffn_kernel(x_ref, g_ref, wi_ref, wo_ref, gf_ref, o_ref, *, dff, final_norm, sub):
    for r in range(0, x_ref.shape[0], sub):
        x = x_ref[r:r + sub, :]
        h = _rms(x, g_ref[...]).astype(BF16)
        gate = jnp.dot(h, wi_ref[:, :dff], preferred_element_type=F32)
        up = jnp.dot(h, wi_ref[:, dff:], preferred_element_type=F32)
        act = (gate * _sigmoid(gate) * up).astype(BF16)
        y = x + jnp.dot(act, wo_ref[...], preferred_element_type=F32)
        o_ref[r:r + sub, :] = _rms(y, gf_ref[...]) if final_norm else y


def _ffn(x2d, g, wi, wo, gf, final_norm, tm, sub):
    m, d = x2d.shape
    dff = wo.shape[0]
    row = pl.BlockSpec((tm, d), lambda i: (i, 0))
    return pl.pallas_call(
        functools.partial(_ffn_kernel, dff=dff, final_norm=final_norm, sub=sub),
        grid=(m // tm,),
        in_specs=[row, _resident((1, d)), _resident(wi.shape), _resident(wo.shape), _resident((1, d))],
        out_specs=row,
        out_shape=jax.ShapeDtypeStruct((m, d), F32),
        compiler_params=_params(("parallel",), 56),
        name="ffn",
    )(x2d, g, wi, wo, gf)


def _tile(n, want):
    return want if n % want == 0 else n


def kernel(x, mem, g_mix, w_in, lb_table, g_hgrn, w_gate, w_proj_a, w_proj_b, w_out, g_xattn, g_mem,
           w_xq, w_xkv, w_xo, g_ffn, w_ffn_in, w_ffn_out, g_final):
    b, t, d = x.shape
    depth = w_in.shape[0]
    nm = mem.shape[1]
    hq = HGRN_HEADS * HGRN_D
    sbw = SB_HEADS * SB_HEAD_DIM
    row = lambda v: v.reshape(1, -1).astype(F32)
    lb_all = jnp.cumsum(jax.nn.softmax(lb_table.astype(F32), axis=0), axis=0)

    col_scale = jnp.ones((4 * hq + 3 * sbw,), F32).at[4 * hq:4 * hq + sbw].set(SB_HEAD_DIM ** -0.5 * LOG2E)

    blk_rows, sub_rows, ffn_sub_rows = _tile(t, 1024), _tile(t, 512), _tile(t, 256)

    x2d = x.reshape(b * t, d)
    for l in range(depth):
        hg, sb = _norm_matmul(x2d, row(g_mix[l]), (w_in[l] * col_scale).astype(BF16),
                              ((0, 4 * hq), (4 * hq, 4 * hq + 3 * sbw)), (F32, BF16), blk_rows, sub_rows)
        ya = _hgrn(hg.reshape(b, t, 4 * hq), row(lb_all[l]), row(g_hgrn[l]), _tile(t, 512))
        yb = _stickbreak(sb.reshape(b, t, 3 * sbw), _tile(t, 256))
        x2d = _mix_out(x2d, ya.reshape(b * t, hq), yb.reshape(b * t, sbw), row(g_mix[l]),
                       w_gate[l].astype(BF16), w_proj_a[l].astype(BF16), w_proj_b[l].astype(BF16),
                       w_out[l].astype(BF16), blk_rows, sub_rows)
        kmem, vmem = _norm_matmul(mem.reshape(b * nm, d), row(g_mem[l]), w_xkv[l].astype(BF16),
                                  ((0, d), (d, 2 * d)), (BF16, BF16), nm, nm)
        x2d = _xattn(x2d.reshape(b, t, d), row(g_xattn[l]), w_xq[l].astype(BF16),
                     kmem.reshape(b, nm, d), vmem.reshape(b, nm, d), w_xo[l].astype(BF16),
                     blk_rows, sub_rows).reshape(b * t, d)
        x2d = _ffn(x2d, row(g_ffn[l]), w_ffn_in[l].astype(BF16), w_ffn_out[l].astype(BF16),
                   row(g_final), l == depth - 1, blk_rows, ffn_sub_rows)
    return x2d.reshape(b, t, d)
```

```python
import functools

import jax
import jax.numpy as jnp
import numpy as np
from jax import lax
from jax.experimental import pallas as pl
from jax.experimental.pallas import tpu as pltpu

F32 = jnp.float32
BF16 = jnp.bfloat16
EPS = 1e-6

CHUNK = 64
HGRN_HEADS = 4
HGRN_D = 128
SB_HEADS = 8
SB_HEAD_DIM = 64
XATTN_HEADS = 4

V7X_VMEM_BYTES = 64 * 1024 * 1024
LANES = 128

_NT = (((1,), (1,)), ((), ()))
_TN = (((0,), (0,)), ((), ()))


def _rms(x, g):
    return x * lax.rsqrt(jnp.mean(x * x, axis=-1, keepdims=True) + EPS) * g


def _sigmoid(x):
    return 1.0 / (1.0 + jnp.exp(-x))


def _params(sem, vmem_mb):
    return pltpu.CompilerParams(dimension_semantics=sem, vmem_limit_bytes=vmem_mb * 1024 * 1024)


def _resident(shape):
    nd = len(shape)
    return pl.BlockSpec(shape, lambda *_: (0,) * nd, pipeline_mode=pl.Buffered(1))


def _norm_matmul_kernel(x_ref, g_ref, w_ref, *out_refs, splits, sub):
    for r in range(0, x_ref.shape[0], sub):
        h = _rms(x_ref[r:r + sub, :], g_ref[...]).astype(BF16)
        for o_ref, (lo, hi) in zip(out_refs, splits):
            o_ref[r:r + sub, :] = jnp.dot(h, w_ref[:, lo:hi], preferred_element_type=F32).astype(o_ref.dtype)


def _norm_matmul(x2d, g, w, splits, dtypes, tm, sub):
    m, d = x2d.shape
    n = w.shape[1]
    return pl.pallas_call(
        functools.partial(_norm_matmul_kernel, splits=splits, sub=sub),
        grid=(m // tm,),
        in_specs=[pl.BlockSpec((tm, d), lambda i: (i, 0)), _resident((1, d)), _resident((d, n))],
        out_specs=[pl.BlockSpec((tm, hi - lo), lambda i: (i, 0)) for lo, hi in splits],
        out_shape=[jax.ShapeDtypeStruct((m, hi - lo), dt) for (lo, hi), dt in zip(splits, dtypes)],
        compiler_params=_params(("parallel",), 48),
        name="norm_matmul",
    )(x2d, g, w)


HGRN_LEVELS = (32, 16, 8)
HGRN_DIAG = 8


def _hgrn_masks():
    c = CHUNK
    t = lax.broadcasted_iota(jnp.int32, (c, c), 0)
    s = lax.broadcasted_iota(jnp.int32, (c, c), 1)
    masks = []
    for m in HGRN_LEVELS:
        same = (t // (2 * m)) == (s // (2 * m))
        masks.append(same & ((t % (2 * m)) >= m) & ((s % (2 * m)) < m))
    masks.append(((t // HGRN_DIAG) == (s // HGRN_DIAG)) & (s <= t))
    return masks


def _split3(x):
    hi = x.astype(BF16)
    r = x - hi.astype(F32)
    mid = r.astype(BF16)
    lo = (r - mid.astype(F32)).astype(BF16)
    return hi, mid, lo


def _hgrn_kernel(hg_ref, lb_ref, gn_ref, tril_ref, o_ref, st_ref, *, n_chunks):
    c, nh, dh = CHUNK, HGRN_HEADS, HGRN_D
    w = nh * dh

    @pl.when(pl.program_id(1) == 0)
    def _():
        st_ref[...] = jnp.zeros_like(st_ref)

    def wide(col):
        return jnp.concatenate(
            [hg_ref[0, ci * c:(ci + 1) * c, col * w:(col + 1) * w] for ci in range(n_chunks)], axis=1)

    def tiled(row):
        return jnp.concatenate([row] * n_chunks, axis=1)

    qp, fp, iv, gp = wide(0), wide(1), wide(2), wide(3)
    lb = tiled(lb_ref[...])
    q = qp * _sigmoid(qp)
    f = lb + (1.0 - lb) * _sigmoid(fp)
    k = 1.0 - f
    b = jnp.dot(tril_ref[...], jnp.concatenate(_split3(jnp.log(f)), axis=0), preferred_element_type=F32)

    def seg_decay(m):
        parts = []
        for s0 in range(0, c, 2 * m):
            bp = b[s0 + m:s0 + m + 1, :]
            parts += [bp - b[s0:s0 + m, :], b[s0 + m:s0 + 2 * m, :] - bp]
        return jnp.concatenate(parts, axis=0)

    half = HGRN_DIAG // 2
    dd = jnp.concatenate([b[s0:s0 + HGRN_DIAG, :] - b[s0 + half:s0 + half + 1, :]
                          for s0 in range(0, c, HGRN_DIAG)], axis=0)
    e_b = jnp.exp(b)
    e_last = e_b[c - 1:c, :]
    qa, kb = [], []
    for m in HGRN_LEVELS:
        e_l = jnp.exp(seg_decay(m))
        qa.append((q * e_l).astype(BF16))
        kb.append((k * e_l).astype(BF16))
    qa.append((q * jnp.exp(dd)).astype(BF16))
    kb.append((k * jnp.exp(-dd)).astype(BF16))
    q_in = (q * e_b).astype(BF16)
    k_out = (k * jnp.exp(b[c - 1:c, :] - b)).astype(BF16)
    v16 = iv.astype(BF16)
    masks = _hgrn_masks()

    units = [(ci, h) for ci in range(n_chunks) for h in range(nh)]
    lanes = {u: slice(u[0] * w + u[1] * dh, u[0] * w + (u[1] + 1) * dh) for u in units}
    kv = {u: lax.dot_general(v16[:, lanes[u]], k_out[:, lanes[u]], _TN, preferred_element_type=F32)
          for u in units}
    scores = {}
    for u in units:
        s_u = jnp.zeros((c, c), F32)
        for a_l, b_l, m_l in zip(qa, kb, masks):
            s_l = lax.dot_general(a_l[:, lanes[u]], b_l[:, lanes[u]], _NT, preferred_element_type=F32)
            s_u = jnp.where(m_l, s_l, s_u)
        scores[u] = s_u.astype(BF16)
    inter = {}
    for h in range(nh):
        st = st_ref[h]
        for ci in range(n_chunks):
            u = (ci, h)
            inter[u] = lax.dot_general(q_in[:, lanes[u]], st.astype(BF16), _NT, preferred_element_type=F32)
            st = e_last[:, lanes[u]] * st + kv[u]
        st_ref[h] = st
    gate = tiled(gn_ref[...]) * (gp * _sigmoid(gp))
    for ci in range(n_chunks):
        outs = []
        for h in range(nh):
            u = (ci, h)
            o = jnp.dot(scores[u], v16[:, lanes[u]], preferred_element_type=F32) + inter[u]
            outs.append(o * lax.rsqrt(jnp.mean(o * o, axis=-1, keepdims=True) + EPS))
        on = jnp.concatenate(outs, axis=-1) * gate[:, ci * w:(ci + 1) * w]
        o_ref[0, ci * c:(ci + 1) * c, :] = on.astype(o_ref.dtype)


def _hgrn(hg, lb, gn, rows_per_step):
    b, t, w4 = hg.shape
    w = w4 // 4
    tril = np.tril(np.ones((CHUNK, CHUNK), np.float32))
    tril3 = jnp.asarray(np.concatenate([tril] * 3, axis=1), BF16)
    return pl.pallas_call(
        functools.partial(_hgrn_kernel, n_chunks=rows_per_step // CHUNK),
        grid=(b, t // rows_per_step),
        in_specs=[pl.BlockSpec((1, rows_per_step, w4), lambda bi, i: (bi, i, 0)),
                  _resident((1, w)), _resident((1, w)), _resident(tril3.shape)],
        out_specs=pl.BlockSpec((1, rows_per_step, w), lambda bi, i: (bi, i, 0)),
        out_shape=jax.ShapeDtypeStruct((b, t, w), BF16),
        scratch_shapes=[pltpu.VMEM((HGRN_HEADS, HGRN_D, HGRN_D), F32)],
        compiler_params=_params(("parallel", "arbitrary"), 48),
        name="hgrn2",
    )(hg, lb, gn, tril3)


LOG2E = 1.4426950408889634
SB_SKIP_LOG2 = 160.0
SB_TERMS = 1
SB_NO_TILE = 1e30


def _bf16_terms(x, n):
    terms, r = [], x
    for i in range(n):
        t = r.astype(BF16)
        terms.append(t)
        if i + 1 < n:
            r = r - t.astype(F32)
    return terms[0] if n == 1 else jnp.concatenate(terms, axis=-1)


def _sb_kernel(q_ref, k_ref, v_ref, o_ref, acc_ref, carry_ref, *, blk, npair):
    i = pl.program_id(1)
    lane = lax.broadcasted_iota(jnp.int32, (1, LANES), 1)
    first = lane < SB_HEAD_DIM
    t_idx = lax.broadcasted_iota(jnp.int32, (blk, blk), 0)
    s_idx = lax.broadcasted_iota(jnp.int32, (blk, blk), 1)
    suffix = jnp.concatenate([(t_idx >= s_idx).astype(BF16)] * SB_TERMS, axis=0)
    causal = s_idx < t_idx
    nh = 2 * npair

    qs = []
    for p in range(npair):
        q = q_ref[0, :, p * LANES:(p + 1) * LANES]
        qs += [jnp.where(keep, q, jnp.zeros_like(q)) for keep in (first, jnp.logical_not(first))]

    def sweep(tiles, carries):
        ks, vs = [], []
        for j, _, _ in tiles:
            rows = pl.ds(pl.multiple_of(j * blk, blk), blk)
            ks.append([k_ref[0, rows, p * LANES:(p + 1) * LANES] for p in range(npair)])
            vs.append([v_ref[0, rows, p * LANES:(p + 1) * LANES] for p in range(npair)])
        units = [(n, h) for n in range(len(tiles)) for h in range(nh)]
        zs = {(n, h): lax.dot_general(qs[h], ks[n][h // 2], _NT, preferred_element_type=F32) for n, h in units}
        tails = {}
        for u in units:
            z = zs[u]
            sp = jnp.maximum(z, 0.0) + jnp.log2(1.0 + jnp.exp2(-jnp.abs(z)))
            if tiles[u[0]][1]:
                sp = jnp.where(causal, sp, 0.0)
            tails[u] = jnp.dot(_bf16_terms(sp, SB_TERMS), suffix, preferred_element_type=F32)
        carries, pvs = list(carries), [None] * nh
        for n, h in units:
            if tiles[n][2] is not None:
                carries[h] = carries[h] + tiles[n][2]
            loga = zs[n, h] - tails[n, h]
            a = jnp.exp2(loga if carries[h] is None else loga - carries[h])
            if tiles[n][1]:
                a = jnp.where(causal, a, 0.0)
            pv = jnp.dot(a.astype(BF16), vs[n][h // 2], preferred_element_type=F32)
            pvs[h] = pv if pvs[h] is None else pvs[h] + pv
            total = tails[n, h][:, 0:1]
            carries[h] = total if carries[h] is None else carries[h] + total
        done = jnp.min(functools.reduce(jnp.minimum, carries)) >= SB_SKIP_LOG2
        return pvs, carries, done

    def merge_heads(accs):
        return jnp.concatenate([jnp.where(first, accs[2 * p], accs[2 * p + 1]) for p in range(npair)],
                               axis=-1).astype(o_ref.dtype)

    no_prev = jnp.where(i == 0, jnp.float32(SB_NO_TILE), jnp.float32(0.0))
    accs, carries, done = sweep([(i, True, None), (jnp.maximum(i - 1, 0), False, no_prev)], [None] * nh)
    o_ref[0] = merge_heads(accs)
    for h in range(nh):
        acc_ref[h] = accs[h]
        carry_ref[h] = carries[h]

    @pl.when(jnp.logical_and(i >= 2, jnp.logical_not(done)))
    def _():
        def body(st):
            n, _ = st
            pvs, new_carries, done = sweep([(i - n, False, None)], [carry_ref[h] for h in range(nh)])
            for h in range(nh):
                acc_ref[h] += pvs[h]
                carry_ref[h] = new_carries[h]
            return n + 1, done.astype(jnp.int32)

        lax.while_loop(lambda st: jnp.logical_and(st[0] <= i, st[1] == 0), body, (jnp.int32(2), jnp.int32(0)))
        o_ref[0] = merge_heads([acc_ref[h] for h in range(nh)])


def _stickbreak(sb, blk):
    b, t, w3 = sb.shape
    w = w3 // 3
    return pl.pallas_call(
        functools.partial(_sb_kernel, blk=blk, npair=w // LANES),
        grid=(b, t // blk),
        in_specs=[pl.BlockSpec((1, blk, w), lambda bi, i: (bi, i, 0)),
                  pl.BlockSpec((1, t, w), lambda bi, i: (bi, 0, 1)),
                  pl.BlockSpec((1, t, w), lambda bi, i: (bi, 0, 2))],
        out_specs=pl.BlockSpec((1, blk, w), lambda bi, i: (bi, i, 0)),
        out_shape=jax.ShapeDtypeStruct((b, t, w), BF16),
        scratch_shapes=[pltpu.VMEM((SB_HEADS, blk, LANES), F32), pltpu.VMEM((SB_HEADS, blk, 1), F32)],
        compiler_params=_params(("parallel", "arbitrary"), 56),
        name="stickbreak",
    )(sb, sb, sb)


def _mix_kernel(x_ref, ya_ref, yb_ref, g_ref, wg_ref, wa_ref, wb_ref, wo_ref, o_ref, *, sub):
    d = x_ref.shape[-1]
    for r in range(0, x_ref.shape[0], sub):
        rows = slice(r, r + sub)
        x = x_ref[rows, :]
        h = _rms(x, g_ref[...]).astype(BF16)
        gate = _sigmoid(jnp.dot(h, wg_ref[...], preferred_element_type=F32))
        pa = jnp.dot(ya_ref[rows, :], wa_ref[...], preferred_element_type=F32)
        pb = jnp.dot(yb_ref[rows, :], wb_ref[...], preferred_element_type=F32)
        merged = gate[:, :d] * pa + gate[:, d:] * pb
        o_ref[rows, :] = x + jnp.dot(merged.astype(BF16), wo_ref[...], preferred_element_type=F32)


def _mix_out(x2d, ya, yb, g, wg, wa, wb, wo, tm, sub):
    m, d = x2d.shape
    wy = ya.shape[1]
    row = lambda n: pl.BlockSpec((tm, n), lambda i: (i, 0))
    return pl.pallas_call(
        functools.partial(_mix_kernel, sub=sub),
        grid=(m // tm,),
        in_specs=[row(d), row(wy), row(wy), _resident((1, d)), _resident(wg.shape),
                  _resident(wa.shape), _resident(wb.shape), _resident(wo.shape)],
        out_specs=row(d),
        out_shape=jax.ShapeDtypeStruct((m, d), F32),
        compiler_params=_params(("parallel",), 48),
        name="mix_out",
    )(x2d, ya, yb, g, wg, wa, wb, wo)


def _xattn_kernel(x_ref, g_ref, wq_ref, k_ref, v_ref, wo_ref, o_ref, *, sub):
    d = x_ref.shape[-1]
    dh = d // XATTN_HEADS
    for r in range(0, x_ref.shape[1], sub):
        x = x_ref[0, r:r + sub, :]
        h = _rms(x, g_ref[...]).astype(BF16)
        q = (jnp.dot(h, wq_ref[...], preferred_element_type=F32) * (dh ** -0.5)).astype(BF16)
        outs = []
        for e in range(XATTN_HEADS):
            hs = slice(e * dh, (e + 1) * dh)
            s = lax.dot_general(q[:, hs], k_ref[0, :, hs], _NT, preferred_element_type=F32)
            p = jnp.exp(s - jnp.max(s, axis=-1, keepdims=True))
            o = jnp.dot(p.astype(BF16), v_ref[0, :, hs], preferred_element_type=F32)
            outs.append(o / jnp.sum(p, axis=-1, keepdims=True))
        o = jnp.concatenate(outs, axis=-1).astype(BF16)
        o_ref[0, r:r + sub, :] = x + jnp.dot(o, wo_ref[...], preferred_element_type=F32)


def _xattn(x3d, g, wq, kmem, vmem, wo, tm, sub):
    b, t, d = x3d.shape
    nm = kmem.shape[1]
    return pl.pallas_call(
        functools.partial(_xattn_kernel, sub=sub),
        grid=(b, t // tm),
        in_specs=[pl.BlockSpec((1, tm, d), lambda bi, i: (bi, i, 0)), _resident((1, d)),
                  _resident(wq.shape),
                  pl.BlockSpec((1, nm, d), lambda bi, i: (bi, 0, 0)),
                  pl.BlockSpec((1, nm, d), lambda bi, i: (bi, 0, 0)),
                  _resident(wo.shape)],
        out_specs=pl.BlockSpec((1, tm, d), lambda bi, i: (bi, i, 0)),
        out_shape=jax.ShapeDtypeStruct((b, t, d), F32),
        compiler_params=_params(("parallel", "parallel"), 48),
        name="xattn",
    )(x3d, g, wq, kmem, vmem, wo)


def _ffn_kernel(x_ref, g_ref, wi_ref, wo_ref, gf_ref, o_ref, *, dff, final_norm, sub):
    for r in range(0, x_ref.shape[0], sub):
        x = x_ref[r:r + sub, :]
        h = _rms(x, g_ref[...]).astype(BF16)
        gate = jnp.dot(h, wi_ref[:, :dff], preferred_element_type=F32)
        up = jnp.dot(h, wi_ref[:, dff:], preferred_element_type=F32)
        act = (gate * _sigmoid(gate) * up).astype(BF16)
        y = x + jnp.dot(act, wo_ref[...], preferred_element_type=F32)
        o_ref[r:r + sub, :] = _rms(y, gf_ref[...]) if final_norm else y


def _ffn(x2d, g, wi, wo, gf, final_norm, tm, sub):
    m, d = x2d.shape
    dff = wo.shape[0]
    row = pl.BlockSpec((tm, d), lambda i: (i, 0))
    return pl.pallas_call(
        functools.partial(_ffn_kernel, dff=dff, final_norm=final_norm, sub=sub),
        grid=(m // tm,),
        in_specs=[row, _resident((1, d)), _resident(wi.shape), _resident(wo.shape), _resident((1, d))],
        out_specs=row,
        out_shape=jax.ShapeDtypeStruct((m, d), F32),
        compiler_params=_params(("parallel",), 56),
        name="ffn",
    )(x2d, g, wi, wo, gf)


def _tile(n, want):
    return want if n % want == 0 else n


def kernel(x, mem, g_mix, w_in, lb_table, g_hgrn, w_gate, w_proj_a, w_proj_b, w_out, g_xattn, g_mem,
           w_xq, w_xkv, w_xo, g_ffn, w_ffn_in, w_ffn_out, g_final):
    b, t, d = x.shape
    depth = w_in.shape[0]
    nm = mem.shape[1]
    hq = HGRN_HEADS * HGRN_D
    sbw = SB_HEADS * SB_HEAD_DIM
    row = lambda v: v.reshape(1, -1).astype(F32)
    lb_all = jnp.cumsum(jax.nn.softmax(lb_table.astype(F32), axis=0), axis=0)

    col_scale = jnp.ones((4 * hq + 3 * sbw,), F32).at[4 * hq:4 * hq + sbw].set(SB_HEAD_DIM ** -0.5 * LOG2E)

    blk_rows, sub_rows, ffn_sub_rows = _tile(t, 1024), _tile(t, 512), _tile(t, 256)

    x2d = x.reshape(b * t, d)
    for l in range(depth):
        hg, sb = _norm_matmul(x2d, row(g_mix[l]), (w_in[l] * col_scale).astype(BF16),
                              ((0, 4 * hq), (4 * hq, 4 * hq + 3 * sbw)), (F32, BF16), blk_rows, sub_rows)
        ya = _hgrn(hg.reshape(b, t, 4 * hq), row(lb_all[l]), row(g_hgrn[l]), _tile(t, 512))
        yb = _stickbreak(sb.reshape(b, t, 3 * sbw), _tile(t, 256))
        x2d = _mix_out(x2d, ya.reshape(b * t, hq), yb.reshape(b * t, sbw), row(g_mix[l]),
                       w_gate[l].astype(BF16), w_proj_a[l].astype(BF16), w_proj_b[l].astype(BF16),
                       w_out[l].astype(BF16), blk_rows, sub_rows)
        kmem, vmem = _norm_matmul(mem.reshape(b * nm, d), row(g_mem[l]), w_xkv[l].astype(BF16),
                                  ((0, d), (d, 2 * d)), (BF16, BF16), nm, nm)
        x2d = _xattn(x2d.reshape(b, t, d), row(g_xattn[l]), w_xq[l].astype(BF16),
                     kmem.reshape(b, nm, d), vmem.reshape(b, nm, d), w_xo[l].astype(BF16),
                     blk_rows, sub_rows).reshape(b * t, d)
        x2d = _ffn(x2d, row(g_ffn[l]), w_ffn_in[l].astype(BF16), w_ffn_out[l].astype(BF16),
                   row(g_final), l == depth - 1, blk_rows, ffn_sub_rows)
    return x2d.reshape(b, t, d)
```

```python
import functools

import jax
import jax.numpy as jnp
import numpy as np
from jax import lax
from jax.experimental import pallas as pl
from jax.experimental.pallas import tpu as pltpu

F32 = jnp.float32
BF16 = jnp.bfloat16
EPS = 1e-6

CHUNK = 64
HGRN_HEADS = 4
HGRN_D = 128
SB_HEADS = 8
SB_HEAD_DIM = 64
XATTN_HEADS = 4

V7X_VMEM_BYTES = 64 * 1024 * 1024
LANES = 128

_NT = (((1,), (1,)), ((), ()))
_TN = (((0,), (0,)), ((), ()))


def _rms(x, g):
    return x * lax.rsqrt(jnp.mean(x * x, axis=-1, keepdims=True) + EPS) * g


def _sigmoid(x):
    return 1.0 / (1.0 + jnp.exp(-x))


def _params(sem, vmem_mb):
    return pltpu.CompilerParams(dimension_semantics=sem, vmem_limit_bytes=vmem_mb * 1024 * 1024)


def _resident(shape):
    nd = len(shape)
    return pl.BlockSpec(shape, lambda *_: (0,) * nd, pipeline_mode=pl.Buffered(1))


def _norm_matmul_kernel(x_ref, g_ref, w_ref, *out_refs, splits, sub):
    for r in range(0, x_ref.shape[0], sub):
        h = _rms(x_ref[r:r + sub, :], g_ref[...]).astype(BF16)
        for o_ref, (lo, hi) in zip(out_refs, splits):
            o_ref[r:r + sub, :] = jnp.dot(h, w_ref[:, lo:hi], preferred_element_type=F32).astype(o_ref.dtype)


def _norm_matmul(x2d, g, w, splits, dtypes, tm, sub):
    m, d = x2d.shape
    n = w.shape[1]
    return pl.pallas_call(
        functools.partial(_norm_matmul_kernel, splits=splits, sub=sub),
        grid=(m // tm,),
        in_specs=[pl.BlockSpec((tm, d), lambda i: (i, 0)), _resident((1, d)), _resident((d, n))],
        out_specs=[pl.BlockSpec((tm, hi - lo), lambda i: (i, 0)) for lo, hi in splits],
        out_shape=[jax.ShapeDtypeStruct((m, hi - lo), dt) for (lo, hi), dt in zip(splits, dtypes)],
        compiler_params=_params(("parallel",), 48),
        name="norm_matmul",
    )(x2d, g, w)


HGRN_LEVELS = (32, 16, 8)
HGRN_DIAG = 8


def _hgrn_masks():
    c = CHUNK
    t = lax.broadcasted_iota(jnp.int32, (c, c), 0)
    s = lax.broadcasted_iota(jnp.int32, (c, c), 1)
    masks = []
    for m in HGRN_LEVELS:
        same = (t // (2 * m)) == (s // (2 * m))
        masks.append(same & ((t % (2 * m)) >= m) & ((s % (2 * m)) < m))
    masks.append(((t // HGRN_DIAG) == (s // HGRN_DIAG)) & (s <= t))
    return masks


HGRN_CUMSUM_TERMS = 2
HGRN_GROUP = 8


def _silu(x):
    hx = 0.5 * x
    return hx + hx * jnp.tanh(hx)


def _hgrn_kernel(hg_ref, lb_ref, gn_ref, tril_ref, o_ref, st_ref, *, n_chunks, group):
    c, nh, dh = CHUNK, HGRN_HEADS, HGRN_D
    w = nh * dh

    @pl.when(pl.program_id(1) == 0)
    def _():
        st_ref[...] = jnp.zeros_like(st_ref)

    masks = _hgrn_masks()
    states = [st_ref[h] for h in range(nh)]
    groups = [_hgrn_group(hg_ref, lb_ref, gn_ref, tril_ref, o_ref, states, masks, c0, group)
              for c0 in range(0, n_chunks, group)]
    next(groups[0])
    next(groups[0])
    for g, gen in enumerate(groups):
        if g + 1 < len(groups):
            next(groups[g + 1])
        next(gen, None)
        if g + 1 < len(groups):
            next(groups[g + 1])
    for h in range(nh):
        st_ref[h] = states[h]


def _hgrn_group(hg_ref, lb_ref, gn_ref, tril_ref, o_ref, states, masks, c0, n):
    c, nh, dh = CHUNK, HGRN_HEADS, HGRN_D
    w = nh * dh

    def wide(col):
        return jnp.concatenate(
            [hg_ref[0, (c0 + ci) * c:(c0 + ci + 1) * c, col * w:(col + 1) * w] for ci in range(n)], axis=1)

    def tiled(row):
        return jnp.concatenate([row] * n, axis=1)

    qp, fp, iv, gp = wide(0), wide(1), wide(2), wide(3)
    lb = tiled(lb_ref[...])
    q = _silu(qp)
    f = 0.5 * (1.0 + lb) + (0.5 * (1.0 - lb)) * jnp.tanh(0.5 * fp)
    k = 1.0 - f
    b = jnp.dot(tril_ref[...], _bf16_terms(jnp.log(f) * LOG2E, HGRN_CUMSUM_TERMS, axis=0),
                preferred_element_type=F32)

    units = [(ci, h) for ci in range(n) for h in range(nh)]
    lanes = {u: slice(u[0] * w + u[1] * dh, u[0] * w + (u[1] + 1) * dh) for u in units}
    e_b = jnp.exp2(b)
    e_last = e_b[c - 1:c, :]
    q_in = (q * e_b).astype(BF16)
    k_out = (k * jnp.exp2(b[c - 1:c, :] - b)).astype(BF16)
    v16 = iv.astype(BF16)
    sides = []
    for m in HGRN_LEVELS:
        decay, qk = [], []
        for s0 in range(0, c, 2 * m):
            bp = b[s0 + m:s0 + m + 1, :]
            decay += [bp - b[s0:s0 + m, :], b[s0 + m:s0 + 2 * m, :] - bp]
            qk += [k[s0:s0 + m, :], q[s0 + m:s0 + 2 * m, :]]
        both = (jnp.concatenate(qk, axis=0) * jnp.exp2(jnp.concatenate(decay, axis=0))).astype(BF16)
        sides.append((both, both))
    half = HGRN_DIAG // 2
    dd = jnp.concatenate([b[s0:s0 + HGRN_DIAG, :] - b[s0 + half:s0 + half + 1, :]
                          for s0 in range(0, c, HGRN_DIAG)], axis=0)
    sides.append(((q * jnp.exp2(dd)).astype(BF16), (k * jnp.exp2(-dd)).astype(BF16)))
    gate = tiled(gn_ref[...]) * _silu(gp)
    yield

    kv = {u: lax.dot_general(v16[:, lanes[u]], k_out[:, lanes[u]], _TN, preferred_element_type=F32)
          for u in units}
    inter = {}
    for h in range(nh):
        for ci in range(n):
            u = (ci, h)
            inter[u] = lax.dot_general(q_in[:, lanes[u]], states[h].astype(BF16), _NT,
                                       preferred_element_type=F32)
            states[h] = e_last[:, lanes[u]] * states[h] + kv[u]
    yield

    scores = {}
    for u in units:
        s_u = jnp.zeros((c, c), F32)
        for (a_l, b_l), m_l in zip(sides, masks):
            s_l = lax.dot_general(a_l[:, lanes[u]], b_l[:, lanes[u]], _NT, preferred_element_type=F32)
            s_u = jnp.where(m_l, s_l, s_u)
        scores[u] = s_u.astype(BF16)
    for ci in range(n):
        outs = []
        for h in range(nh):
            u = (ci, h)
            o = jnp.dot(scores[u], v16[:, lanes[u]], preferred_element_type=F32) + inter[u]
            outs.append(o * lax.rsqrt(jnp.mean(o * o, axis=-1, keepdims=True) + EPS))
        on = jnp.concatenate(outs, axis=-1) * gate[:, ci * w:(ci + 1) * w]
        o_ref[0, (c0 + ci) * c:(c0 + ci + 1) * c, :] = on.astype(o_ref.dtype)


def _hgrn(hg, lb, gn, rows_per_step):
    b, t, w4 = hg.shape
    w = w4 // 4
    tril = np.tril(np.ones((CHUNK, CHUNK), np.float32))
    tril3 = jnp.asarray(np.concatenate([tril] * HGRN_CUMSUM_TERMS, axis=1), BF16)
    return pl.pallas_call(
        functools.partial(_hgrn_kernel, n_chunks=rows_per_step // CHUNK, group=HGRN_GROUP),
        grid=(b, t // rows_per_step),
        in_specs=[pl.BlockSpec((1, rows_per_step, w4), lambda bi, i: (bi, i, 0)),
                  _resident((1, w)), _resident((1, w)), _resident(tril3.shape)],
        out_specs=pl.BlockSpec((1, rows_per_step, w), lambda bi, i: (bi, i, 0)),
        out_shape=jax.ShapeDtypeStruct((b, t, w), BF16),
        scratch_shapes=[pltpu.VMEM((HGRN_HEADS, HGRN_D, HGRN_D), F32)],
        compiler_params=_params(("parallel", "arbitrary"), 48),
        name="hgrn2",
    )(hg, lb, gn, tril3)


LOG2E = 1.4426950408889634
SB_SKIP_LOG2 = 160.0
SB_TERMS = 1
SB_NO_TILE = 1e30


def _bf16_terms(x, n, axis=-1):
    terms, r = [], x
    for i in range(n):
        t = r.astype(BF16)
        terms.append(t)
        if i + 1 < n:
            r = r - t.astype(F32)
    return terms[0] if n == 1 else jnp.concatenate(terms, axis=axis)


def _sb_kernel(q_ref, k_ref, v_ref, o_ref, acc_ref, carry_ref, *, blk, npair):
    i = pl.program_id(1)
    lane = lax.broadcasted_iota(jnp.int32, (1, LANES), 1)
    first = lane < SB_HEAD_DIM
    t_idx = lax.broadcasted_iota(jnp.int32, (blk, blk), 0)
    s_idx = lax.broadcasted_iota(jnp.int32, (blk, blk), 1)
    suffix = jnp.concatenate([(t_idx >= s_idx).astype(BF16)] * SB_TERMS, axis=0)
    causal = s_idx < t_idx
    nh = 2 * npair

    qs = []
    for p in range(npair):
        q = q_ref[0, :, p * LANES:(p + 1) * LANES]
        qs += [jnp.where(keep, q, jnp.zeros_like(q)) for keep in (first, jnp.logical_not(first))]

    def sweep(tiles, carries):
        ks, vs = [], []
        for j, _, _ in tiles:
            rows = pl.ds(pl.multiple_of(j * blk, blk), blk)
            ks.append([k_ref[0, rows, p * LANES:(p + 1) * LANES] for p in range(npair)])
            vs.append([v_ref[0, rows, p * LANES:(p + 1) * LANES] for p in range(npair)])
        units = [(n, h) for n in range(len(tiles)) for h in range(nh)]
        zs = {(n, h): lax.dot_general(qs[h], ks[n][h // 2], _NT, preferred_element_type=F32) for n, h in units}
        tails = {}
        for u in units:
            z = zs[u]
            sp = jnp.maximum(z, 0.0) + jnp.log2(1.0 + jnp.exp2(-jnp.abs(z)))
            if tiles[u[0]][1]:
                sp = jnp.where(causal, sp, 0.0)
            tails[u] = jnp.dot(_bf16_terms(sp, SB_TERMS), suffix, preferred_element_type=F32)
        carries, pvs = list(carries), [None] * nh
        for n, h in units:
            if tiles[n][2] is not None:
                carries[h] = carries[h] + tiles[n][2]
            loga = zs[n, h] - tails[n, h]
            a = jnp.exp2(loga if carries[h] is None else loga - carries[h])
            if tiles[n][1]:
                a = jnp.where(causal, a, 0.0)
            pv = jnp.dot(a.astype(BF16), vs[n][h // 2], preferred_element_type=F32)
            pvs[h] = pv if pvs[h] is None else pvs[h] + pv
            total = tails[n, h][:, 0:1]
            carries[h] = total if carries[h] is None else carries[h] + total
        done = jnp.min(functools.reduce(jnp.minimum, carries)) >= SB_SKIP_LOG2
        return pvs, carries, done

    def merge_heads(accs):
        return jnp.concatenate([jnp.where(first, accs[2 * p], accs[2 * p + 1]) for p in range(npair)],
                               axis=-1).astype(o_ref.dtype)

    no_prev = jnp.where(i == 0, jnp.float32(SB_NO_TILE), jnp.float32(0.0))
    accs, carries, done = sweep([(i, True, None), (jnp.maximum(i - 1, 0), False, no_prev)], [None] * nh)
    o_ref[0] = merge_heads(accs)
    for h in range(nh):
        acc_ref[h] = accs[h]
        carry_ref[h] = carries[h]

    @pl.when(jnp.logical_and(i >= 2, jnp.logical_not(done)))
    def _():
        def body(st):
            n, _ = st
            pvs, new_carries, done = sweep([(i - n, False, None)], [carry_ref[h] for h in range(nh)])
            for h in range(nh):
                acc_ref[h] += pvs[h]
                carry_ref[h] = new_carries[h]
            return n + 1, done.astype(jnp.int32)

        lax.while_loop(lambda st: jnp.logical_and(st[0] <= i, st[1] == 0), body, (jnp.int32(2), jnp.int32(0)))
        o_ref[0] = merge_heads([acc_ref[h] for h in range(nh)])


def _stickbreak(sb, blk):
    b, t, w3 = sb.shape
    w = w3 // 3
    return pl.pallas_call(
        functools.partial(_sb_kernel, blk=blk, npair=w // LANES),
        grid=(b, t // blk),
        in_specs=[pl.BlockSpec((1, blk, w), lambda bi, i: (bi, i, 0)),
                  pl.BlockSpec((1, t, w), lambda bi, i: (bi, 0, 1)),
                  pl.BlockSpec((1, t, w), lambda bi, i: (bi, 0, 2))],
        out_specs=pl.BlockSpec((1, blk, w), lambda bi, i: (bi, i, 0)),
        out_shape=jax.ShapeDtypeStruct((b, t, w), BF16),
        scratch_shapes=[pltpu.VMEM((SB_HEADS, blk, LANES), F32), pltpu.VMEM((SB_HEADS, blk, 1), F32)],
        compiler_params=_params(("parallel", "arbitrary"), 56),
        name="stickbreak",
    )(sb, sb, sb)


def _mix_kernel(x_ref, ya_ref, yb_ref, g_ref, wg_ref, wa_ref, wb_ref, wo_ref, o_ref, *, sub):
    d = x_ref.shape[-1]
    for r in range(0, x_ref.shape[0], sub):
        rows = slice(r, r + sub)
        x = x_ref[rows, :]
        h = _rms(x, g_ref[...]).astype(BF16)
        gate = _sigmoid(jnp.dot(h, wg_ref[...], preferred_element_type=F32))
        pa = jnp.dot(ya_ref[rows, :], wa_ref[...], preferred_element_type=F32)
        pb = jnp.dot(yb_ref[rows, :], wb_ref[...], preferred_element_type=F32)
        merged = gate[:, :d] * pa + gate[:, d:] * pb
        o_ref[rows, :] = x + jnp.dot(merged.astype(BF16), wo_ref[...], preferred_element_type=F32)


def _mix_out(x2d, ya, yb, g, wg, wa, wb, wo, tm, sub):
    m, d = x2d.shape
    wy = ya.shape[1]
    row = lambda n: pl.BlockSpec((tm, n), lambda i: (i, 0))
    return pl.pallas_call(
        functools.partial(_mix_kernel, sub=sub),
        grid=(m // tm,),
        in_specs=[row(d), row(wy), row(wy), _resident((1, d)), _resident(wg.shape),
                  _resident(wa.shape), _resident(wb.shape), _resident(wo.shape)],
        out_specs=row(d),
        out_shape=jax.ShapeDtypeStruct((m, d), F32),
        compiler_params=_params(("parallel",), 48),
        name="mix_out",
    )(x2d, ya, yb, g, wg, wa, wb, wo)


def _xattn_kernel(x_ref, g_ref, wq_ref, k_ref, v_ref, wo_ref, o_ref, *, sub):
    d = x_ref.shape[-1]
    dh = d // XATTN_HEADS
    for r in range(0, x_ref.shape[1], sub):
        x = x_ref[0, r:r + sub, :]
        h = _rms(x, g_ref[...]).astype(BF16)
        q = (jnp.dot(h, wq_ref[...], preferred_element_type=F32) * (dh ** -0.5)).astype(BF16)
        outs = []
        for e in range(XATTN_HEADS):
            hs = slice(e * dh, (e + 1) * dh)
            s = lax.dot_general(q[:, hs], k_ref[0, :, hs], _NT, preferred_element_type=F32)
            p = jnp.exp(s - jnp.max(s, axis=-1, keepdims=True))
            o = jnp.dot(p.astype(BF16), v_ref[0, :, hs], preferred_element_type=F32)
            outs.append(o / jnp.sum(p, axis=-1, keepdims=True))
        o = jnp.concatenate(outs, axis=-1).astype(BF16)
        o_ref[0, r:r + sub, :] = x + jnp.dot(o, wo_ref[...], preferred_element_type=F32)


def _xattn(x3d, g, wq, kmem, vmem, wo, tm, sub):
    b, t, d = x3d.shape
    nm = kmem.shape[1]
    return pl.pallas_call(
        functools.partial(_xattn_kernel, sub=sub),
        grid=(b, t // tm),
        in_specs=[pl.BlockSpec((1, tm, d), lambda bi, i: (bi, i, 0)), _resident((1, d)),
                  _resident(wq.shape),
                  pl.BlockSpec((1, nm, d), lambda bi, i: (bi, 0, 0)),
                  pl.BlockSpec((1, nm, d), lambda bi, i: (bi, 0, 0)),
                  _resident(wo.shape)],
        out_specs=pl.BlockSpec((1, tm, d), lambda bi, i: (bi, i, 0)),
        out_shape=jax.ShapeDtypeStruct((b, t, d), F32),
        compiler_params=_params(("parallel", "parallel"), 48),
        name="xattn",
    )(x3d, g, wq, kmem, vmem, wo)


def _ffn_kernel(x_ref, g_ref, wi_ref, wo_ref, gf_ref, o_ref, *, dff, final_norm, sub):
    for r in range(0, x_ref.shape[0], sub):
        x = x_ref[r:r + sub, :]
        h = _rms(x, g_ref[...]).astype(BF16)
        gate = jnp.dot(h, wi_ref[:, :dff], preferred_element_type=F32)
        up = jnp.dot(h, wi_ref[:, dff:], preferred_element_type=F32)
        act = (gate * _sigmoid(gate) * up).astype(BF16)
        y = x + jnp.dot(act, wo_ref[...], preferred_element_type=F32)
        o_ref[r:r + sub, :] = _rms(y, gf_ref[...]) if final_norm else y


def _ffn(x2d, g, wi, wo, gf, final_norm, tm, sub):
    m, d = x2d.shape
    dff = wo.shape[0]
    row = pl.BlockSpec((tm, d), lambda i: (i, 0))
    return pl.pallas_call(
        functools.partial(_ffn_kernel, dff=dff, final_norm=final_norm, sub=sub),
        grid=(m // tm,),
        in_specs=[row, _resident((1, d)), _resident(wi.shape), _resident(wo.shape), _resident((1, d))],
        out_specs=row,
        out_shape=jax.ShapeDtypeStruct((m, d), F32),
        compiler_params=_params(("parallel",), 56),
        name="ffn",
    )(x2d, g, wi, wo, gf)


def _tile(n, want):
    return want if n % want == 0 else n


def kernel(x, mem, g_mix, w_in, lb_table, g_hgrn, w_gate, w_proj_a, w_proj_b, w_out, g_xattn, g_mem,
           w_xq, w_xkv, w_xo, g_ffn, w_ffn_in, w_ffn_out, g_final):
    b, t, d = x.shape
    depth = w_in.shape[0]
    nm = mem.shape[1]
    hq = HGRN_HEADS * HGRN_D
    sbw = SB_HEADS * SB_HEAD_DIM
    row = lambda v: v.reshape(1, -1).astype(F32)
    lb_all = jnp.cumsum(jax.nn.softmax(lb_table.astype(F32), axis=0), axis=0)

    col_scale = jnp.ones((4 * hq + 3 * sbw,), F32).at[4 * hq:4 * hq + sbw].set(SB_HEAD_DIM ** -0.5 * LOG2E)

    blk_rows, sub_rows, ffn_sub_rows = _tile(t, 1024), _tile(t, 512), _tile(t, 256)

    x2d = x.reshape(b * t, d)
    for l in range(depth):
        hg, sb = _norm_matmul(x2d, row(g_mix[l]), (w_in[l] * col_scale).astype(BF16),
                              ((0, 4 * hq), (4 * hq, 4 * hq + 3 * sbw)), (F32, BF16), blk_rows, sub_rows)
        ya = _hgrn(hg.reshape(b, t, 4 * hq), row(lb_all[l]), row(g_hgrn[l]), _tile(t, 512))
        yb = _stickbreak(sb.reshape(b, t, 3 * sbw), _tile(t, 256))
        x2d = _mix_out(x2d, ya.reshape(b * t, hq), yb.reshape(b * t, sbw), row(g_mix[l]),
                       w_gate[l].astype(BF16), w_proj_a[l].astype(BF16), w_proj_b[l].astype(BF16),
                       w_out[l].astype(BF16), blk_rows, sub_rows)
        kmem, vmem = _norm_matmul(mem.reshape(b * nm, d), row(g_mem[l]), w_xkv[l].astype(BF16),
                                  ((0, d), (d, 2 * d)), (BF16, BF16), nm, nm)
        x2d = _xattn(x2d.reshape(b, t, d), row(g_xattn[l]), w_xq[l].astype(BF16),
                     kmem.reshape(b, nm, d), vmem.reshape(b, nm, d), w_xo[l].astype(BF16),
                     blk_rows, sub_rows).reshape(b * t, d)
        x2d = _ffn(x2d, row(g_ffn[l]), w_ffn_in[l].astype(BF16), w_ffn_out[l].astype(BF16),
                   row(g_final), l == depth - 1, blk_rows, ffn_sub_rows)
    return x2d.reshape(b, t, d)
```

```python
import functools

import jax
import jax.numpy as jnp
import numpy as np
from jax import lax
from jax.experimental import pallas as pl
from jax.experimental.pallas import tpu as pltpu

F32 = jnp.float32
BF16 = jnp.bfloat16
EPS = 1e-6

CHUNK = 64
HGRN_HEADS = 4
HGRN_D = 128
SB_HEADS = 8
SB_HEAD_DIM = 64
XATTN_HEADS = 4

V7X_VMEM_BYTES = 64 * 1024 * 1024
LANES = 128

_NT = (((1,), (1,)), ((), ()))
_TN = (((0,), (0,)), ((), ()))


def _rms(x, g):
    return x * lax.rsqrt(jnp.mean(x * x, axis=-1, keepdims=True) + EPS) * g


def _sigmoid(x):
    return 1.0 / (1.0 + jnp.exp(-x))


def _params(sem, vmem_mb):
    return pltpu.CompilerParams(dimension_semantics=sem, vmem_limit_bytes=vmem_mb * 1024 * 1024)


def _resident(shape):
    nd = len(shape)
    return pl.BlockSpec(shape, lambda *_: (0,) * nd, pipeline_mode=pl.Buffered(1))


def _norm_matmul_kernel(x_ref, g_ref, w_ref, *out_refs, splits, sub):
    for r in range(0, x_ref.shape[0], sub):
        h = _rms(x_ref[r:r + sub, :], g_ref[...]).astype(BF16)
        for o_ref, (lo, hi) in zip(out_refs, splits):
            o_ref[r:r + sub, :] = jnp.dot(h, w_ref[:, lo:hi], preferred_element_type=F32).astype(o_ref.dtype)


def _norm_matmul(x2d, g, w, splits, dtypes, tm, sub):
    m, d = x2d.shape
    n = w.shape[1]
    return pl.pallas_call(
        functools.partial(_norm_matmul_kernel, splits=splits, sub=sub),
        grid=(m // tm,),
        in_specs=[pl.BlockSpec((tm, d), lambda i: (i, 0)), _resident((1, d)), _resident((d, n))],
        out_specs=[pl.BlockSpec((tm, hi - lo), lambda i: (i, 0)) for lo, hi in splits],
        out_shape=[jax.ShapeDtypeStruct((m, hi - lo), dt) for (lo, hi), dt in zip(splits, dtypes)],
        compiler_params=_params(("parallel",), 48),
        name="norm_matmul",
    )(x2d, g, w)


HGRN_LEVELS = (32, 16, 8)
HGRN_DIAG = 8


def _hgrn_masks():
    c = CHUNK
    t = lax.broadcasted_iota(jnp.int32, (c, c), 0)
    s = lax.broadcasted_iota(jnp.int32, (c, c), 1)
    masks = []
    for m in HGRN_LEVELS:
        same = (t // (2 * m)) == (s // (2 * m))
        masks.append(same & ((t % (2 * m)) >= m) & ((s % (2 * m)) < m))
    masks.append(((t // HGRN_DIAG) == (s // HGRN_DIAG)) & (s <= t))
    return masks


HGRN_CUMSUM_TERMS = 2
HGRN_GROUP = 8


def _silu(x):
    hx = 0.5 * x
    return hx + hx * jnp.tanh(hx)


def _hgrn_kernel(hg_ref, lb_ref, gn_ref, tril_ref, o_ref, st_ref, *, n_chunks, group):
    c, nh, dh = CHUNK, HGRN_HEADS, HGRN_D
    w = nh * dh

    @pl.when(pl.program_id(1) == 0)
    def _():
        st_ref[...] = jnp.zeros_like(st_ref)

    masks = _hgrn_masks()
    states = [st_ref[h] for h in range(nh)]
    groups = [_hgrn_group(hg_ref, lb_ref, gn_ref, tril_ref, o_ref, states, masks, c0, group)
              for c0 in range(0, n_chunks, group)]
    next(groups[0])
    next(groups[0])
    for g, gen in enumerate(groups):
        if g + 1 < len(groups):
            next(groups[g + 1])
        next(gen, None)
        if g + 1 < len(groups):
            next(groups[g + 1])
    for h in range(nh):
        st_ref[h] = states[h]


def _hgrn_group(hg_ref, lb_ref, gn_ref, tril_ref, o_ref, states, masks, c0, n):
    c, nh, dh = CHUNK, HGRN_HEADS, HGRN_D
    w = nh * dh

    def wide(col):
        return jnp.concatenate(
            [hg_ref[0, (c0 + ci) * c:(c0 + ci + 1) * c, col * w:(col + 1) * w] for ci in range(n)], axis=1)

    def tiled(row):
        return jnp.concatenate([row] * n, axis=1)

    qp, fp, iv, gp = wide(0), wide(1), wide(2), wide(3)
    lb = tiled(lb_ref[...])
    q = _silu(qp)
    f = 0.5 * (1.0 + lb) + (0.5 * (1.0 - lb)) * jnp.tanh(0.5 * fp)
    k = 1.0 - f
    b = jnp.dot(tril_ref[...], _bf16_terms(jnp.log(f) * LOG2E, HGRN_CUMSUM_TERMS, axis=0),
                preferred_element_type=F32)

    units = [(ci, h) for ci in range(n) for h in range(nh)]
    lanes = {u: slice(u[0] * w + u[1] * dh, u[0] * w + (u[1] + 1) * dh) for u in units}
    e_b = jnp.exp2(b)
    e_last = e_b[c - 1:c, :]
    q_in = (q * e_b).astype(BF16)
    k_out = (k * jnp.exp2(b[c - 1:c, :] - b)).astype(BF16)
    v16 = iv.astype(BF16)
    sides = []
    for m in HGRN_LEVELS:
        decay, qk = [], []
        for s0 in range(0, c, 2 * m):
            bp = b[s0 + m:s0 + m + 1, :]
            decay += [bp - b[s0:s0 + m, :], b[s0 + m:s0 + 2 * m, :] - bp]
            qk += [k[s0:s0 + m, :], q[s0 + m:s0 + 2 * m, :]]
        both = (jnp.concatenate(qk, axis=0) * jnp.exp2(jnp.concatenate(decay, axis=0))).astype(BF16)
        sides.append((both, both))
    half = HGRN_DIAG // 2
    dd = jnp.concatenate([b[s0:s0 + HGRN_DIAG, :] - b[s0 + half:s0 + half + 1, :]
                          for s0 in range(0, c, HGRN_DIAG)], axis=0)
    sides.append(((q * jnp.exp2(dd)).astype(BF16), (k * jnp.exp2(-dd)).astype(BF16)))
    gate = tiled(gn_ref[...]) * _silu(gp)
    yield

    kv = {u: lax.dot_general(v16[:, lanes[u]], k_out[:, lanes[u]], _TN, preferred_element_type=F32)
          for u in units}
    inter = {}
    for h in range(nh):
        for ci in range(n):
            u = (ci, h)
            inter[u] = lax.dot_general(q_in[:, lanes[u]], states[h].astype(BF16), _NT,
                                       preferred_element_type=F32)
            states[h] = e_last[:, lanes[u]] * states[h] + kv[u]
    yield

    scores = {}
    for u in units:
        s_u = jnp.zeros((c, c), F32)
        for (a_l, b_l), m_l in zip(sides, masks):
            s_l = lax.dot_general(a_l[:, lanes[u]], b_l[:, lanes[u]], _NT, preferred_element_type=F32)
            s_u = jnp.where(m_l, s_l, s_u)
        scores[u] = s_u.astype(BF16)
    for ci in range(n):
        outs = []
        for h in range(nh):
            u = (ci, h)
            o = jnp.dot(scores[u], v16[:, lanes[u]], preferred_element_type=F32) + inter[u]
            outs.append(o * lax.rsqrt(jnp.mean(o * o, axis=-1, keepdims=True) + EPS))
        on = jnp.concatenate(outs, axis=-1) * gate[:, ci * w:(ci + 1) * w]
        o_ref[0, (c0 + ci) * c:(c0 + ci + 1) * c, :] = on.astype(o_ref.dtype)


def _hgrn(hg, lb, gn, rows_per_step):
    b, t, w4 = hg.shape
    w = w4 // 4
    tril = np.tril(np.ones((CHUNK, CHUNK), np.float32))
    tril3 = jnp.asarray(np.concatenate([tril] * HGRN_CUMSUM_TERMS, axis=1), BF16)
    return pl.pallas_call(
        functools.partial(_hgrn_kernel, n_chunks=rows_per_step // CHUNK, group=HGRN_GROUP),
        grid=(b, t // rows_per_step),
        in_specs=[pl.BlockSpec((1, rows_per_step, w4), lambda bi, i: (bi, i, 0)),
                  _resident((1, w)), _resident((1, w)), _resident(tril3.shape)],
        out_specs=pl.BlockSpec((1, rows_per_step, w), lambda bi, i: (bi, i, 0)),
        out_shape=jax.ShapeDtypeStruct((b, t, w), BF16),
        scratch_shapes=[pltpu.VMEM((HGRN_HEADS, HGRN_D, HGRN_D), F32)],
        compiler_params=_params(("parallel", "arbitrary"), 48),
        name="hgrn2",
    )(hg, lb, gn, tril3)


LOG2E = 1.4426950408889634
SB_SKIP_LOG2 = 160.0
SB_TERMS = 1
SB_NO_TILE = 1e30
SB_LINEAR_ABOVE = 64.0


def _bf16_terms(x, n, axis=-1):
    terms, r = [], x
    for i in range(n):
        t = r.astype(BF16)
        terms.append(t)
        if i + 1 < n:
            r = r - t.astype(F32)
    return terms[0] if n == 1 else jnp.concatenate(terms, axis=axis)


def _sb_kernel(q_ref, k_ref, v_ref, o_ref, acc_ref, carry_ref, *, blk, npair):
    i = pl.program_id(1)
    lane = lax.broadcasted_iota(jnp.int32, (1, LANES), 1)
    first = lane < SB_HEAD_DIM
    t_idx = lax.broadcasted_iota(jnp.int32, (blk, blk), 0)
    s_idx = lax.broadcasted_iota(jnp.int32, (blk, blk), 1)
    after = jnp.concatenate([(t_idx > s_idx).astype(BF16)] * SB_TERMS, axis=0)
    causal = s_idx < t_idx
    nh = 2 * npair

    qs = []
    for p in range(npair):
        q = q_ref[0, :, p * LANES:(p + 1) * LANES]
        qs += [jnp.where(keep, q, jnp.zeros_like(q)) for keep in (first, jnp.logical_not(first))]

    def sweep(tiles, carries):
        ks, vs = [], []
        for j, _, _ in tiles:
            rows = pl.ds(pl.multiple_of(j * blk, blk), blk)
            ks.append([k_ref[0, rows, p * LANES:(p + 1) * LANES] for p in range(npair)])
            vs.append([v_ref[0, rows, p * LANES:(p + 1) * LANES] for p in range(npair)])
        units = [(n, h) for n in range(len(tiles)) for h in range(nh)]
        zs = {}
        for n, h in units:
            z = lax.dot_general(qs[h], ks[n][h // 2], _NT, preferred_element_type=F32)
            zs[n, h] = jnp.where(causal, z, -SB_NO_TILE) if tiles[n][1] else z
        logb, later, total = {}, {}, {}
        for u in units:
            z = zs[u]
            sp = jnp.where(z > SB_LINEAR_ABOVE, z, jnp.log2(1.0 + jnp.exp2(z)))
            later[u] = jnp.dot(_bf16_terms(sp, SB_TERMS), after, preferred_element_type=F32)
            logb[u] = z - sp
            total[u] = later[u][:, 0:1] + sp[:, 0:1]
        carries, pvs = list(carries), [None] * nh
        for n, h in units:
            if tiles[n][2] is not None:
                carries[h] = carries[h] + tiles[n][2]
            loga = logb[n, h] - later[n, h]
            a = jnp.exp2(loga if carries[h] is None else loga - carries[h])
            pv = jnp.dot(a.astype(BF16), vs[n][h // 2], preferred_element_type=F32)
            pvs[h] = pv if pvs[h] is None else pvs[h] + pv
            carries[h] = total[n, h] if carries[h] is None else carries[h] + total[n, h]
        done = jnp.min(functools.reduce(jnp.minimum, carries)) >= SB_SKIP_LOG2
        return pvs, carries, done

    def merge_heads(accs):
        return jnp.concatenate([jnp.where(first, accs[2 * p], accs[2 * p + 1]) for p in range(npair)],
                               axis=-1).astype(o_ref.dtype)

    no_prev = jnp.where(i == 0, jnp.float32(SB_NO_TILE), jnp.float32(0.0))
    accs, carries, done = sweep([(i, True, None), (jnp.maximum(i - 1, 0), False, no_prev)], [None] * nh)
    o_ref[0] = merge_heads(accs)
    for h in range(nh):
        acc_ref[h] = accs[h]
        carry_ref[h] = carries[h]

    @pl.when(jnp.logical_and(i >= 2, jnp.logical_not(done)))
    def _():
        def body(st):
            n, _ = st
            pvs, new_carries, done = sweep([(i - n, False, None)], [carry_ref[h] for h in range(nh)])
            for h in range(nh):
                acc_ref[h] += pvs[h]
                carry_ref[h] = new_carries[h]
            return n + 1, done.astype(jnp.int32)

        lax.while_loop(lambda st: jnp.logical_and(st[0] <= i, st[1] == 0), body, (jnp.int32(2), jnp.int32(0)))
        o_ref[0] = merge_heads([acc_ref[h] for h in range(nh)])


def _stickbreak(sb, blk):
    b, t, w3 = sb.shape
    w = w3 // 3
    return pl.pallas_call(
        functools.partial(_sb_kernel, blk=blk, npair=w // LANES),
        grid=(b, t // blk),
        in_specs=[pl.BlockSpec((1, blk, w), lambda bi, i: (bi, i, 0)),
                  pl.BlockSpec((1, t, w), lambda bi, i: (bi, 0, 1)),
                  pl.BlockSpec((1, t, w), lambda bi, i: (bi, 0, 2))],
        out_specs=pl.BlockSpec((1, blk, w), lambda bi, i: (bi, i, 0)),
        out_shape=jax.ShapeDtypeStruct((b, t, w), BF16),
        scratch_shapes=[pltpu.VMEM((SB_HEADS, blk, LANES), F32), pltpu.VMEM((SB_HEADS, blk, 1), F32)],
        compiler_params=_params(("parallel", "arbitrary"), 56),
        name="stickbreak",
    )(sb, sb, sb)


def _mix_kernel(x_ref, ya_ref, yb_ref, g_ref, wg_ref, wa_ref, wb_ref, wo_ref, o_ref, *, sub):
    d = x_ref.shape[-1]
    for r in range(0, x_ref.shape[0], sub):
        rows = slice(r, r + sub)
        x = x_ref[rows, :]
        h = _rms(x, g_ref[...]).astype(BF16)
        gate = _sigmoid(jnp.dot(h, wg_ref[...], preferred_element_type=F32))
        pa = jnp.dot(ya_ref[rows, :], wa_ref[...], preferred_element_type=F32)
        pb = jnp.dot(yb_ref[rows, :], wb_ref[...], preferred_element_type=F32)
        merged = gate[:, :d] * pa + gate[:, d:] * pb
        o_ref[rows, :] = x + jnp.dot(merged.astype(BF16), wo_ref[...], preferred_element_type=F32)


def _mix_out(x2d, ya, yb, g, wg, wa, wb, wo, tm, sub):
    m, d = x2d.shape
    wy = ya.shape[1]
    row = lambda n: pl.BlockSpec((tm, n), lambda i: (i, 0))
    return pl.pallas_call(
        functools.partial(_mix_kernel, sub=sub),
        grid=(m // tm,),
        in_specs=[row(d), row(wy), row(wy), _resident((1, d)), _resident(wg.shape),
                  _resident(wa.shape), _resident(wb.shape), _resident(wo.shape)],
        out_specs=row(d),
        out_shape=jax.ShapeDtypeStruct((m, d), F32),
        compiler_params=_params(("parallel",), 48),
        name="mix_out",
    )(x2d, ya, yb, g, wg, wa, wb, wo)


def _xattn_kernel(x_ref, g_ref, wq_ref, k_ref, v_ref, wo_ref, o_ref, *, sub):
    d = x_ref.shape[-1]
    dh = d // XATTN_HEADS
    for r in range(0, x_ref.shape[1], sub):
        x = x_ref[0, r:r + sub, :]
        h = _rms(x, g_ref[...]).astype(BF16)
        q = (jnp.dot(h, wq_ref[...], preferred_element_type=F32) * (dh ** -0.5)).astype(BF16)
        outs = []
        for e in range(XATTN_HEADS):
            hs = slice(e * dh, (e + 1) * dh)
            s = lax.dot_general(q[:, hs], k_ref[0, :, hs], _NT, preferred_element_type=F32)
            p = jnp.exp(s - jnp.max(s, axis=-1, keepdims=True))
            o = jnp.dot(p.astype(BF16), v_ref[0, :, hs], preferred_element_type=F32)
            outs.append(o / jnp.sum(p, axis=-1, keepdims=True))
        o = jnp.concatenate(outs, axis=-1).astype(BF16)
        o_ref[0, r:r + sub, :] = x + jnp.dot(o, wo_ref[...], preferred_element_type=F32)


def _xattn(x3d, g, wq, kmem, vmem, wo, tm, sub):
    b, t, d = x3d.shape
    nm = kmem.shape[1]
    return pl.pallas_call(
        functools.partial(_xattn_kernel, sub=sub),
        grid=(b, t // tm),
        in_specs=[pl.BlockSpec((1, tm, d), lambda bi, i: (bi, i, 0)), _resident((1, d)),
                  _resident(wq.shape),
                  pl.BlockSpec((1, nm, d), lambda bi, i: (bi, 0, 0)),
                  pl.BlockSpec((1, nm, d), lambda bi, i: (bi, 0, 0)),
                  _resident(wo.shape)],
        out_specs=pl.BlockSpec((1, tm, d), lambda bi, i: (bi, i, 0)),
        out_shape=jax.ShapeDtypeStruct((b, t, d), F32),
        compiler_params=_params(("parallel", "parallel"), 48),
        name="xattn",
    )(x3d, g, wq, kmem, vmem, wo)


def _ffn_kernel(x_ref, g_ref, wi_ref, wo_ref, gf_ref, o_ref, *, dff, final_norm, sub):
    for r in range(0, x_ref.shape[0], sub):
        x = x_ref[r:r + sub, :]
        h = _rms(x, g_ref[...]).astype(BF16)
        gate = jnp.dot(h, wi_ref[:, :dff], preferred_element_type=F32)
        up = jnp.dot(h, wi_ref[:, dff:], preferred_element_type=F32)
        act = (gate * _sigmoid(gate) * up).astype(BF16)
        y = x + jnp.dot(act, wo_ref[...], preferred_element_type=F32)
        o_ref[r:r + sub, :] = _rms(y, gf_ref[...]) if final_norm else y


def _ffn(x2d, g, wi, wo, gf, final_norm, tm, sub):
    m, d = x2d.shape
    dff = wo.shape[0]
    row = pl.BlockSpec((tm, d), lambda i: (i, 0))
    return pl.pallas_call(
        functools.partial(_ffn_kernel, dff=dff, final_norm=final_norm, sub=sub),
        grid=(m // tm,),
        in_specs=[row, _resident((1, d)), _resident(wi.shape), _resident(wo.shape), _resident((1, d))],
        out_specs=row,
        out_shape=jax.ShapeDtypeStruct((m, d), F32),
        compiler_params=_params(("parallel",), 56),
        name="ffn",
    )(x2d, g, wi, wo, gf)


def _tile(n, want):
    return want if n % want == 0 else n


def kernel(x, mem, g_mix, w_in, lb_table, g_hgrn, w_gate, w_proj_a, w_proj_b, w_out, g_xattn, g_mem,
           w_xq, w_xkv, w_xo, g_ffn, w_ffn_in, w_ffn_out, g_final):
    b, t, d = x.shape
    depth = w_in.shape[0]
    nm = mem.shape[1]
    hq = HGRN_HEADS * HGRN_D
    sbw = SB_HEADS * SB_HEAD_DIM
    row = lambda v: v.reshape(1, -1).astype(F32)
    lb_all = jnp.cumsum(jax.nn.softmax(lb_table.astype(F32), axis=0), axis=0)

    col_scale = jnp.ones((4 * hq + 3 * sbw,), F32).at[4 * hq:4 * hq + sbw].set(SB_HEAD_DIM ** -0.5 * LOG2E)

    blk_rows, sub_rows, ffn_sub_rows = _tile(t, 1024), _tile(t, 512), _tile(t, 256)

    x2d = x.reshape(b * t, d)
    for l in range(depth):
        hg, sb = _norm_matmul(x2d, row(g_mix[l]), (w_in[l] * col_scale).astype(BF16),
                              ((0, 4 * hq), (4 * hq, 4 * hq + 3 * sbw)), (F32, BF16), blk_rows, sub_rows)
        ya = _hgrn(hg.reshape(b, t, 4 * hq), row(lb_all[l]), row(g_hgrn[l]), _tile(t, 512))
        yb = _stickbreak(sb.reshape(b, t, 3 * sbw), _tile(t, 256))
        x2d = _mix_out(x2d, ya.reshape(b * t, hq), yb.reshape(b * t, sbw), row(g_mix[l]),
                       w_gate[l].astype(BF16), w_proj_a[l].astype(BF16), w_proj_b[l].astype(BF16),
                       w_out[l].astype(BF16), blk_rows, sub_rows)
        kmem, vmem = _norm_matmul(mem.reshape(b * nm, d), row(g_mem[l]), w_xkv[l].astype(BF16),
                                  ((0, d), (d, 2 * d)), (BF16, BF16), nm, nm)
        x2d = _xattn(x2d.reshape(b, t, d), row(g_xattn[l]), w_xq[l].astype(BF16),
                     kmem.reshape(b, nm, d), vmem.reshape(b, nm, d), w_xo[l].astype(BF16),
                     blk_rows, sub_rows).reshape(b * t, d)
        x2d = _ffn(x2d, row(g_ffn[l]), w_ffn_in[l].astype(BF16), w_ffn_out[l].astype(BF16),
                   row(g_final), l == depth - 1, blk_rows, ffn_sub_rows)
    return x2d.reshape(b, t, d)
```

```python
import functools

import jax
import jax.numpy as jnp
import numpy as np
from jax import lax
from jax.experimental import pallas as pl
from jax.experimental.pallas import tpu as pltpu

F32 = jnp.float32
BF16 = jnp.bfloat16
EPS = 1e-6

CHUNK = 64
HGRN_HEADS = 4
HGRN_D = 128
SB_HEADS = 8
SB_HEAD_DIM = 64
XATTN_HEADS = 4

V7X_VMEM_BYTES = 64 * 1024 * 1024
LANES = 128

_NT = (((1,), (1,)), ((), ()))
_TN = (((0,), (0,)), ((), ()))


def _rms(x, g):
    return x * lax.rsqrt(jnp.mean(x * x, axis=-1, keepdims=True) + EPS) * g


def _sigmoid(x):
    return 1.0 / (1.0 + jnp.exp(-x))


def _params(sem, vmem_mb):
    return pltpu.CompilerParams(dimension_semantics=sem, vmem_limit_bytes=vmem_mb * 1024 * 1024)


def _resident(shape):
    nd = len(shape)
    return pl.BlockSpec(shape, lambda *_: (0,) * nd, pipeline_mode=pl.Buffered(1))


def _norm_matmul_kernel(x_ref, g_ref, w_ref, *out_refs, splits, sub):
    for r in range(0, x_ref.shape[0], sub):
        h = _rms(x_ref[r:r + sub, :], g_ref[...]).astype(BF16)
        for o_ref, (lo, hi) in zip(out_refs, splits):
            o_ref[r:r + sub, :] = jnp.dot(h, w_ref[:, lo:hi], preferred_element_type=F32).astype(o_ref.dtype)


def _norm_matmul(x2d, g, w, splits, dtypes, tm, sub):
    m, d = x2d.shape
    n = w.shape[1]
    return pl.pallas_call(
        functools.partial(_norm_matmul_kernel, splits=splits, sub=sub),
        grid=(m // tm,),
        in_specs=[pl.BlockSpec((tm, d), lambda i: (i, 0)), _resident((1, d)), _resident((d, n))],
        out_specs=[pl.BlockSpec((tm, hi - lo), lambda i: (i, 0)) for lo, hi in splits],
        out_shape=[jax.ShapeDtypeStruct((m, hi - lo), dt) for (lo, hi), dt in zip(splits, dtypes)],
        compiler_params=_params(("parallel",), 48),
        name="norm_matmul",
    )(x2d, g, w)


HGRN_LEVELS = (32, 16, 8)
HGRN_DIAG = 8


def _hgrn_masks():
    c = CHUNK
    t = lax.broadcasted_iota(jnp.int32, (c, c), 0)
    s = lax.broadcasted_iota(jnp.int32, (c, c), 1)
    masks = []
    for m in HGRN_LEVELS:
        same = (t // (2 * m)) == (s // (2 * m))
        masks.append(same & ((t % (2 * m)) >= m) & ((s % (2 * m)) < m))
    masks.append(((t // HGRN_DIAG) == (s // HGRN_DIAG)) & (s <= t))
    return masks


HGRN_CUMSUM_TERMS = 2
HGRN_GROUP = 8


def _silu(x):
    hx = 0.5 * x
    return hx + hx * jnp.tanh(hx)


def _hgrn_kernel(hg_ref, lb_ref, gn_ref, tril_ref, o_ref, st_ref, *, n_chunks, group):
    c, nh, dh = CHUNK, HGRN_HEADS, HGRN_D
    w = nh * dh

    @pl.when(pl.program_id(1) == 0)
    def _():
        st_ref[...] = jnp.zeros_like(st_ref)

    masks = _hgrn_masks()
    states = [st_ref[h] for h in range(nh)]
    groups = [_hgrn_group(hg_ref, lb_ref, gn_ref, tril_ref, o_ref, states, masks, c0, group)
              for c0 in range(0, n_chunks, group)]
    next(groups[0])
    next(groups[0])
    for g, gen in enumerate(groups):
        if g + 1 < len(groups):
            next(groups[g + 1])
        next(gen, None)
        if g + 1 < len(groups):
            next(groups[g + 1])
    for h in range(nh):
        st_ref[h] = states[h]


def _hgrn_group(hg_ref, lb_ref, gn_ref, tril_ref, o_ref, states, masks, c0, n):
    c, nh, dh = CHUNK, HGRN_HEADS, HGRN_D
    w = nh * dh

    def wide(col):
        return jnp.concatenate(
            [hg_ref[0, (c0 + ci) * c:(c0 + ci + 1) * c, col * w:(col + 1) * w] for ci in range(n)], axis=1)

    def tiled(row):
        return jnp.concatenate([row] * n, axis=1)

    qp, fp, iv, gp = wide(0), wide(1), wide(2), wide(3)
    lb = tiled(lb_ref[...])
    q = _silu(qp)
    f = 0.5 * (1.0 + lb) + (0.5 * (1.0 - lb)) * jnp.tanh(0.5 * fp)
    k = 1.0 - f
    b = jnp.dot(tril_ref[...], _bf16_terms(jnp.log(f) * LOG2E, HGRN_CUMSUM_TERMS, axis=0),
                preferred_element_type=F32)

    units = [(ci, h) for ci in range(n) for h in range(nh)]
    lanes = {u: slice(u[0] * w + u[1] * dh, u[0] * w + (u[1] + 1) * dh) for u in units}
    e_b = jnp.exp2(b)
    e_last = e_b[c - 1:c, :]
    q_in = (q * e_b).astype(BF16)
    k_out = (k * jnp.exp2(b[c - 1:c, :] - b)).astype(BF16)
    v16 = iv.astype(BF16)
    sides = []
    for m in HGRN_LEVELS:
        decay, qk = [], []
        for s0 in range(0, c, 2 * m):
            bp = b[s0 + m:s0 + m + 1, :]
            decay += [bp - b[s0:s0 + m, :], b[s0 + m:s0 + 2 * m, :] - bp]
            qk += [k[s0:s0 + m, :], q[s0 + m:s0 + 2 * m, :]]
        both = (jnp.concatenate(qk, axis=0) * jnp.exp2(jnp.concatenate(decay, axis=0))).astype(BF16)
        sides.append((both, both))
    half = HGRN_DIAG // 2
    dd = jnp.concatenate([b[s0:s0 + HGRN_DIAG, :] - b[s0 + half:s0 + half + 1, :]
                          for s0 in range(0, c, HGRN_DIAG)], axis=0)
    sides.append(((q * jnp.exp2(dd)).astype(BF16), (k * jnp.exp2(-dd)).astype(BF16)))
    gate = tiled(gn_ref[...]) * _silu(gp)
    yield

    kv = {u: lax.dot_general(v16[:, lanes[u]], k_out[:, lanes[u]], _TN, preferred_element_type=F32)
          for u in units}
    inter = {}
    for h in range(nh):
        for ci in range(n):
            u = (ci, h)
            inter[u] = lax.dot_general(q_in[:, lanes[u]], states[h].astype(BF16), _NT,
                                       preferred_element_type=F32)
            states[h] = e_last[:, lanes[u]] * states[h] + kv[u]
    yield

    scores = {}
    for u in units:
        s_u = jnp.zeros((c, c), F32)
        for (a_l, b_l), m_l in zip(sides, masks):
            s_l = lax.dot_general(a_l[:, lanes[u]], b_l[:, lanes[u]], _NT, preferred_element_type=F32)
            s_u = jnp.where(m_l, s_l, s_u)
        scores[u] = s_u.astype(BF16)
    for ci in range(n):
        outs = []
        for h in range(nh):
            u = (ci, h)
            o = jnp.dot(scores[u], v16[:, lanes[u]], preferred_element_type=F32) + inter[u]
            outs.append(o * lax.rsqrt(jnp.mean(o * o, axis=-1, keepdims=True) + EPS))
        on = jnp.concatenate(outs, axis=-1) * gate[:, ci * w:(ci + 1) * w]
        o_ref[0, (c0 + ci) * c:(c0 + ci + 1) * c, :] = on.astype(o_ref.dtype)


def _hgrn(hg, lb, gn, rows_per_step):
    b, t, w4 = hg.shape
    w = w4 // 4
    tril = np.tril(np.ones((CHUNK, CHUNK), np.float32))
    tril3 = jnp.asarray(np.concatenate([tril] * HGRN_CUMSUM_TERMS, axis=1), BF16)
    return pl.pallas_call(
        functools.partial(_hgrn_kernel, n_chunks=rows_per_step // CHUNK, group=HGRN_GROUP),
        grid=(b, t // rows_per_step),
        in_specs=[pl.BlockSpec((1, rows_per_step, w4), lambda bi, i: (bi, i, 0)),
                  _resident((1, w)), _resident((1, w)), _resident(tril3.shape)],
        out_specs=pl.BlockSpec((1, rows_per_step, w), lambda bi, i: (bi, i, 0)),
        out_shape=jax.ShapeDtypeStruct((b, t, w), BF16),
        scratch_shapes=[pltpu.VMEM((HGRN_HEADS, HGRN_D, HGRN_D), F32)],
        compiler_params=_params(("parallel", "arbitrary"), 48),
        name="hgrn2",
    )(hg, lb, gn, tril3)


LOG2E = 1.4426950408889634
SB_SKIP_LOG2 = 160.0
SB_TERMS = 1
SB_NO_TILE = 1e30
SB_LINEAR_ABOVE = 64.0
SB_SKEW = 1


def _bf16_terms(x, n, axis=-1):
    terms, r = [], x
    for i in range(n):
        t = r.astype(BF16)
        terms.append(t)
        if i + 1 < n:
            r = r - t.astype(F32)
    return terms[0] if n == 1 else jnp.concatenate(terms, axis=axis)


def _sb_kernel(q_ref, k_ref, v_ref, o_ref, acc_ref, carry_ref, *, blk, npair):
    i = pl.program_id(1)
    lane = lax.broadcasted_iota(jnp.int32, (1, LANES), 1)
    first = lane < SB_HEAD_DIM
    t_idx = lax.broadcasted_iota(jnp.int32, (blk, blk), 0)
    s_idx = lax.broadcasted_iota(jnp.int32, (blk, blk), 1)
    after = jnp.concatenate([(t_idx > s_idx).astype(BF16)] * SB_TERMS, axis=0)
    causal = s_idx < t_idx
    nh = 2 * npair

    qs = []
    for p in range(npair):
        q = q_ref[0, :, p * LANES:(p + 1) * LANES]
        qs += [jnp.where(keep, q, jnp.zeros_like(q)) for keep in (first, jnp.logical_not(first))]

    def sweep(tiles, carries):
        ks, vs = [], []
        for j, _, _ in tiles:
            rows = pl.ds(pl.multiple_of(j * blk, blk), blk)
            ks.append([k_ref[0, rows, p * LANES:(p + 1) * LANES] for p in range(npair)])
            vs.append([v_ref[0, rows, p * LANES:(p + 1) * LANES] for p in range(npair)])
        units = [(n, h) for n in range(len(tiles)) for h in range(nh)]
        carries, pvs = list(carries), [None] * nh
        zs, logb, later, total = {}, {}, {}, {}

        def logits(u):
            n, h = u
            z = lax.dot_general(qs[h], ks[n][h // 2], _NT, preferred_element_type=F32)
            zs[u] = jnp.where(causal, z, -SB_NO_TILE) if tiles[n][1] else z

        def suffix_sums(u):
            z = zs.pop(u)
            sp = jnp.where(z > SB_LINEAR_ABOVE, z, jnp.log2(1.0 + jnp.exp2(z)))
            later[u] = jnp.dot(_bf16_terms(sp, SB_TERMS), after, preferred_element_type=F32)
            logb[u] = z - sp
            total[u] = later[u][:, 0:1] + sp[:, 0:1]

        def weigh(u):
            n, h = u
            if tiles[n][2] is not None:
                carries[h] = carries[h] + tiles[n][2]
            loga = logb.pop(u) - later.pop(u)
            a = jnp.exp2(loga if carries[h] is None else loga - carries[h])
            pv = jnp.dot(a.astype(BF16), vs[n][h // 2], preferred_element_type=F32)
            pvs[h] = pv if pvs[h] is None else pvs[h] + pv
            carries[h] = total[u] if carries[h] is None else carries[h] + total[u]

        for step in range(len(units) + 2 * SB_SKEW):
            for stage, fn in enumerate((logits, suffix_sums, weigh)):
                idx = step - stage * SB_SKEW
                if 0 <= idx < len(units):
                    fn(units[idx])
        done = jnp.min(functools.reduce(jnp.minimum, carries)) >= SB_SKIP_LOG2
        return pvs, carries, done

    def merge_heads(accs):
        return jnp.concatenate([jnp.where(first, accs[2 * p], accs[2 * p + 1]) for p in range(npair)],
                               axis=-1).astype(o_ref.dtype)

    no_prev = jnp.where(i == 0, jnp.float32(SB_NO_TILE), jnp.float32(0.0))
    accs, carries, done = sweep([(i, True, None), (jnp.maximum(i - 1, 0), False, no_prev)], [None] * nh)
    o_ref[0] = merge_heads(accs)
    for h in range(nh):
        acc_ref[h] = accs[h]
        carry_ref[h] = carries[h]

    @pl.when(jnp.logical_and(i >= 2, jnp.logical_not(done)))
    def _():
        def body(st):
            n, _ = st
            pvs, new_carries, done = sweep([(i - n, False, None)], [carry_ref[h] for h in range(nh)])
            for h in range(nh):
                acc_ref[h] += pvs[h]
                carry_ref[h] = new_carries[h]
            return n + 1, done.astype(jnp.int32)

        lax.while_loop(lambda st: jnp.logical_and(st[0] <= i, st[1] == 0), body, (jnp.int32(2), jnp.int32(0)))
        o_ref[0] = merge_heads([acc_ref[h] for h in range(nh)])


def _stickbreak(sb, blk):
    b, t, w3 = sb.shape
    w = w3 // 3
    return pl.pallas_call(
        functools.partial(_sb_kernel, blk=blk, npair=w // LANES),
        grid=(b, t // blk),
        in_specs=[pl.BlockSpec((1, blk, w), lambda bi, i: (bi, i, 0)),
                  pl.BlockSpec((1, t, w), lambda bi, i: (bi, 0, 1)),
                  pl.BlockSpec((1, t, w), lambda bi, i: (bi, 0, 2))],
        out_specs=pl.BlockSpec((1, blk, w), lambda bi, i: (bi, i, 0)),
        out_shape=jax.ShapeDtypeStruct((b, t, w), BF16),
        scratch_shapes=[pltpu.VMEM((SB_HEADS, blk, LANES), F32), pltpu.VMEM((SB_HEADS, blk, 1), F32)],
        compiler_params=_params(("parallel", "arbitrary"), 56),
        name="stickbreak",
    )(sb, sb, sb)


def _mix_kernel(x_ref, ya_ref, yb_ref, g_ref, wg_ref, wa_ref, wb_ref, wo_ref, o_ref, *, sub):
    d = x_ref.shape[-1]
    for r in range(0, x_ref.shape[0], sub):
        rows = slice(r, r + sub)
        x = x_ref[rows, :]
        h = _rms(x, g_ref[...]).astype(BF16)
        gate = _sigmoid(jnp.dot(h, wg_ref[...], preferred_element_type=F32))
        pa = jnp.dot(ya_ref[rows, :], wa_ref[...], preferred_element_type=F32)
        pb = jnp.dot(yb_ref[rows, :], wb_ref[...], preferred_element_type=F32)
        merged = gate[:, :d] * pa + gate[:, d:] * pb
        o_ref[rows, :] = x + jnp.dot(merged.astype(BF16), wo_ref[...], preferred_element_type=F32)


def _mix_out(x2d, ya, yb, g, wg, wa, wb, wo, tm, sub):
    m, d = x2d.shape
    wy = ya.shape[1]
    row = lambda n: pl.BlockSpec((tm, n), lambda i: (i, 0))
    return pl.pallas_call(
        functools.partial(_mix_kernel, sub=sub),
        grid=(m // tm,),
        in_specs=[row(d), row(wy), row(wy), _resident((1, d)), _resident(wg.shape),
                  _resident(wa.shape), _resident(wb.shape), _resident(wo.shape)],
        out_specs=row(d),
        out_shape=jax.ShapeDtypeStruct((m, d), F32),
        compiler_params=_params(("parallel",), 48),
        name="mix_out",
    )(x2d, ya, yb, g, wg, wa, wb, wo)


def _xattn_kernel(x_ref, g_ref, wq_ref, k_ref, v_ref, wo_ref, o_ref, *, sub):
    d = x_ref.shape[-1]
    dh = d // XATTN_HEADS
    for r in range(0, x_ref.shape[1], sub):
        x = x_ref[0, r:r + sub, :]
        h = _rms(x, g_ref[...]).astype(BF16)
        q = (jnp.dot(h, wq_ref[...], preferred_element_type=F32) * (dh ** -0.5)).astype(BF16)
        outs = []
        for e in range(XATTN_HEADS):
            hs = slice(e * dh, (e + 1) * dh)
            s = lax.dot_general(q[:, hs], k_ref[0, :, hs], _NT, preferred_element_type=F32)
            p = jnp.exp(s - jnp.max(s, axis=-1, keepdims=True))
            o = jnp.dot(p.astype(BF16), v_ref[0, :, hs], preferred_element_type=F32)
            outs.append(o / jnp.sum(p, axis=-1, keepdims=True))
        o = jnp.concatenate(outs, axis=-1).astype(BF16)
        o_ref[0, r:r + sub, :] = x + jnp.dot(o, wo_ref[...], preferred_element_type=F32)


def _xattn(x3d, g, wq, kmem, vmem, wo, tm, sub):
    b, t, d = x3d.shape
    nm = kmem.shape[1]
    return pl.pallas_call(
        functools.partial(_xattn_kernel, sub=sub),
        grid=(b, t // tm),
        in_specs=[pl.BlockSpec((1, tm, d), lambda bi, i: (bi, i, 0)), _resident((1, d)),
                  _resident(wq.shape),
                  pl.BlockSpec((1, nm, d), lambda bi, i: (bi, 0, 0)),
                  pl.BlockSpec((1, nm, d), lambda bi, i: (bi, 0, 0)),
                  _resident(wo.shape)],
        out_specs=pl.BlockSpec((1, tm, d), lambda bi, i: (bi, i, 0)),
        out_shape=jax.ShapeDtypeStruct((b, t, d), F32),
        compiler_params=_params(("parallel", "parallel"), 48),
        name="xattn",
    )(x3d, g, wq, kmem, vmem, wo)


def _ffn_kernel(x_ref, g_ref, wi_ref, wo_ref, gf_ref, o_ref, *, dff, final_norm, sub):
    for r in range(0, x_ref.shape[0], sub):
        x = x_ref[r:r + sub, :]
        h = _rms(x, g_ref[...]).astype(BF16)
        gate = jnp.dot(h, wi_ref[:, :dff], preferred_element_type=F32)
        up = jnp.dot(h, wi_ref[:, dff:], preferred_element_type=F32)
        act = (gate * _sigmoid(gate) * up).astype(BF16)
        y = x + jnp.dot(act, wo_ref[...], preferred_element_type=F32)
        o_ref[r:r + sub, :] = _rms(y, gf_ref[...]) if final_norm else y


def _ffn(x2d, g, wi, wo, gf, final_norm, tm, sub):
    m, d = x2d.shape
    dff = wo.shape[0]
    row = pl.BlockSpec((tm, d), lambda i: (i, 0))
    return pl.pallas_call(
        functools.partial(_ffn_kernel, dff=dff, final_norm=final_norm, sub=sub),
        grid=(m // tm,),
        in_specs=[row, _resident((1, d)), _resident(wi.shape), _resident(wo.shape), _resident((1, d))],
        out_specs=row,
        out_shape=jax.ShapeDtypeStruct((m, d), F32),
        compiler_params=_params(("parallel",), 56),
        name="ffn",
    )(x2d, g, wi, wo, gf)


def _tile(n, want):
    return want if n % want == 0 else n


def kernel(x, mem, g_mix, w_in, lb_table, g_hgrn, w_gate, w_proj_a, w_proj_b, w_out, g_xattn, g_mem,
           w_xq, w_xkv, w_xo, g_ffn, w_ffn_in, w_ffn_out, g_final):
    b, t, d = x.shape
    depth = w_in.shape[0]
    nm = mem.shape[1]
    hq = HGRN_HEADS * HGRN_D
    sbw = SB_HEADS * SB_HEAD_DIM
    row = lambda v: v.reshape(1, -1).astype(F32)
    lb_all = jnp.cumsum(jax.nn.softmax(lb_table.astype(F32), axis=0), axis=0)

    col_scale = jnp.ones((4 * hq + 3 * sbw,), F32).at[4 * hq:4 * hq + sbw].set(SB_HEAD_DIM ** -0.5 * LOG2E)

    blk_rows, sub_rows, ffn_sub_rows = _tile(t, 1024), _tile(t, 512), _tile(t, 256)

    x2d = x.reshape(b * t, d)
    for l in range(depth):
        hg, sb = _norm_matmul(x2d, row(g_mix[l]), (w_in[l] * col_scale).astype(BF16),
                              ((0, 4 * hq), (4 * hq, 4 * hq + 3 * sbw)), (F32, BF16), blk_rows, sub_rows)
        ya = _hgrn(hg.reshape(b, t, 4 * hq), row(lb_all[l]), row(g_hgrn[l]), _tile(t, 512))
        yb = _stickbreak(sb.reshape(b, t, 3 * sbw), _tile(t, 256))
        x2d = _mix_out(x2d, ya.reshape(b * t, hq), yb.reshape(b * t, sbw), row(g_mix[l]),
                       w_gate[l].astype(BF16), w_proj_a[l].astype(BF16), w_proj_b[l].astype(BF16),
                       w_out[l].astype(BF16), blk_rows, sub_rows)
        kmem, vmem = _norm_matmul(mem.reshape(b * nm, d), row(g_mem[l]), w_xkv[l].astype(BF16),
                                  ((0, d), (d, 2 * d)), (BF16, BF16), nm, nm)
        x2d = _xattn(x2d.reshape(b, t, d), row(g_xattn[l]), w_xq[l].astype(BF16),
                     kmem.reshape(b, nm, d), vmem.reshape(b, nm, d), w_xo[l].astype(BF16),
                     blk_rows, sub_rows).reshape(b * t, d)
        x2d = _ffn(x2d, row(g_ffn[l]), w_ffn_in[l].astype(BF16), w_ffn_out[l].astype(BF16),
                   row(g_final), l == depth - 1, blk_rows, ffn_sub_rows)
    return x2d.reshape(b, t, d)
```

```python
import functools

import jax
import jax.numpy as jnp
import numpy as np
from jax import lax
from jax.experimental import pallas as pl
from jax.experimental.pallas import tpu as pltpu

F32 = jnp.float32
BF16 = jnp.bfloat16
EPS = 1e-6

CHUNK = 64
HGRN_HEADS = 4
HGRN_D = 128
SB_HEADS = 8
SB_HEAD_DIM = 64
XATTN_HEADS = 4

V7X_VMEM_BYTES = 64 * 1024 * 1024
LANES = 128

_NT = (((1,), (1,)), ((), ()))
_TN = (((0,), (0,)), ((), ()))


def _rms(x, g):
    return x * lax.rsqrt(jnp.mean(x * x, axis=-1, keepdims=True) + EPS) * g


def _sigmoid(x):
    return 1.0 / (1.0 + jnp.exp(-x))


def _params(sem, vmem_mb):
    return pltpu.CompilerParams(dimension_semantics=sem, vmem_limit_bytes=vmem_mb * 1024 * 1024)


def _resident(shape):
    nd = len(shape)
    return pl.BlockSpec(shape, lambda *_: (0,) * nd, pipeline_mode=pl.Buffered(1))


def _norm_matmul_kernel(x_ref, g_ref, w_ref, *out_refs, splits, sub):
    for r in range(0, x_ref.shape[0], sub):
        h = _rms(x_ref[r:r + sub, :], g_ref[...]).astype(BF16)
        for o_ref, (lo, hi) in zip(out_refs, splits):
            o_ref[r:r + sub, :] = jnp.dot(h, w_ref[:, lo:hi], preferred_element_type=F32).astype(o_ref.dtype)


def _norm_matmul(x2d, g, w, splits, dtypes, tm, sub):
    m, d = x2d.shape
    n = w.shape[1]
    return pl.pallas_call(
        functools.partial(_norm_matmul_kernel, splits=splits, sub=sub),
        grid=(m // tm,),
        in_specs=[pl.BlockSpec((tm, d), lambda i: (i, 0)), _resident((1, d)), _resident((d, n))],
        out_specs=[pl.BlockSpec((tm, hi - lo), lambda i: (i, 0)) for lo, hi in splits],
        out_shape=[jax.ShapeDtypeStruct((m, hi - lo), dt) for (lo, hi), dt in zip(splits, dtypes)],
        compiler_params=_params(("parallel",), 48),
        name="norm_matmul",
    )(x2d, g, w)


HGRN_LEVELS = (32, 16, 8)
HGRN_DIAG = 8


def _hgrn_masks():
    c = CHUNK
    t = lax.broadcasted_iota(jnp.int32, (c, c), 0)
    s = lax.broadcasted_iota(jnp.int32, (c, c), 1)
    masks = []
    for m in HGRN_LEVELS:
        same = (t // (2 * m)) == (s // (2 * m))
        masks.append(same & ((t % (2 * m)) >= m) & ((s % (2 * m)) < m))
    masks.append(((t // HGRN_DIAG) == (s // HGRN_DIAG)) & (s <= t))
    return masks


HGRN_CUMSUM_TERMS = 2
HGRN_GROUP = 1


def _silu(x):
    hx = 0.5 * x
    return hx + hx * jnp.tanh(hx)


def _hgrn_kernel(hg_ref, lb_ref, gn_ref, tril_ref, o_ref, st_ref, *, n_chunks, group):
    c, nh, dh = CHUNK, HGRN_HEADS, HGRN_D
    w = nh * dh

    @pl.when(pl.program_id(1) == 0)
    def _():
        st_ref[...] = jnp.zeros_like(st_ref)

    masks = _hgrn_masks()
    states = [st_ref[h] for h in range(nh)]
    groups = [_hgrn_group(hg_ref, lb_ref, gn_ref, tril_ref, o_ref, states, masks, c0, group)
              for c0 in range(0, n_chunks, group)]
    for step in range(len(groups) + 2):
        for phase in range(3):
            g = step - phase
            if 0 <= g < len(groups):
                next(groups[g], None)
    for h in range(nh):
        st_ref[h] = states[h]


def _hgrn_group(hg_ref, lb_ref, gn_ref, tril_ref, o_ref, states, masks, c0, n):
    c, nh, dh = CHUNK, HGRN_HEADS, HGRN_D
    w = nh * dh

    def wide(col):
        return jnp.concatenate(
            [hg_ref[0, (c0 + ci) * c:(c0 + ci + 1) * c, col * w:(col + 1) * w] for ci in range(n)], axis=1)

    def tiled(row):
        return jnp.concatenate([row] * n, axis=1)

    qp, fp, iv, gp = wide(0), wide(1), wide(2), wide(3)
    lb = tiled(lb_ref[...])
    q = _silu(qp)
    f = 0.5 * (1.0 + lb) + (0.5 * (1.0 - lb)) * jnp.tanh(0.5 * fp)
    k = 1.0 - f
    b = jnp.dot(tril_ref[...], _bf16_terms(jnp.log(f) * LOG2E, HGRN_CUMSUM_TERMS, axis=0),
                preferred_element_type=F32)

    units = [(ci, h) for ci in range(n) for h in range(nh)]
    lanes = {u: slice(u[0] * w + u[1] * dh, u[0] * w + (u[1] + 1) * dh) for u in units}
    e_b = jnp.exp2(b)
    e_last = e_b[c - 1:c, :]
    q_in = (q * e_b).astype(BF16)
    k_out = (k * jnp.exp2(b[c - 1:c, :] - b)).astype(BF16)
    v16 = iv.astype(BF16)
    sides = []
    for m in HGRN_LEVELS:
        decay, qk = [], []
        for s0 in range(0, c, 2 * m):
            bp = b[s0 + m:s0 + m + 1, :]
            decay += [bp - b[s0:s0 + m, :], b[s0 + m:s0 + 2 * m, :] - bp]
            qk += [k[s0:s0 + m, :], q[s0 + m:s0 + 2 * m, :]]
        both = (jnp.concatenate(qk, axis=0) * jnp.exp2(jnp.concatenate(decay, axis=0))).astype(BF16)
        sides.append((both, both))
    half = HGRN_DIAG // 2
    dd = jnp.concatenate([b[s0:s0 + HGRN_DIAG, :] - b[s0 + half:s0 + half + 1, :]
                          for s0 in range(0, c, HGRN_DIAG)], axis=0)
    sides.append(((q * jnp.exp2(dd)).astype(BF16), (k * jnp.exp2(-dd)).astype(BF16)))
    gate = tiled(gn_ref[...]) * _silu(gp)
    yield

    kv = {u: lax.dot_general(v16[:, lanes[u]], k_out[:, lanes[u]], _TN, preferred_element_type=F32)
          for u in units}
    inter = {}
    for h in range(nh):
        for ci in range(n):
            u = (ci, h)
            inter[u] = lax.dot_general(q_in[:, lanes[u]], states[h].astype(BF16), _NT,
                                       preferred_element_type=F32)
            states[h] = e_last[:, lanes[u]] * states[h] + kv[u]
    parts = {u: [lax.dot_general(a_l[:, lanes[u]], b_l[:, lanes[u]], _NT, preferred_element_type=F32)
                 for a_l, b_l in sides] for u in units}
    yield

    scores = {}
    for u in units:
        s_u = jnp.zeros((c, c), F32)
        for s_l, m_l in zip(parts[u], masks):
            s_u = jnp.where(m_l, s_l, s_u)
        scores[u] = s_u.astype(BF16)
    for ci in range(n):
        outs = []
        for h in range(nh):
            u = (ci, h)
            o = jnp.dot(scores[u], v16[:, lanes[u]], preferred_element_type=F32) + inter[u]
            outs.append(o * lax.rsqrt(jnp.mean(o * o, axis=-1, keepdims=True) + EPS))
        on = jnp.concatenate(outs, axis=-1) * gate[:, ci * w:(ci + 1) * w]
        o_ref[0, (c0 + ci) * c:(c0 + ci + 1) * c, :] = on.astype(o_ref.dtype)


def _hgrn(hg, lb, gn, rows_per_step):
    b, t, w4 = hg.shape
    w = w4 // 4
    tril = np.tril(np.ones((CHUNK, CHUNK), np.float32))
    tril3 = jnp.asarray(np.concatenate([tril] * HGRN_CUMSUM_TERMS, axis=1), BF16)
    return pl.pallas_call(
        functools.partial(_hgrn_kernel, n_chunks=rows_per_step // CHUNK, group=HGRN_GROUP),
        grid=(b, t // rows_per_step),
        in_specs=[pl.BlockSpec((1, rows_per_step, w4), lambda bi, i: (bi, i, 0)),
                  _resident((1, w)), _resident((1, w)), _resident(tril3.shape)],
        out_specs=pl.BlockSpec((1, rows_per_step, w), lambda bi, i: (bi, i, 0)),
        out_shape=jax.ShapeDtypeStruct((b, t, w), BF16),
        scratch_shapes=[pltpu.VMEM((HGRN_HEADS, HGRN_D, HGRN_D), F32)],
        compiler_params=_params(("parallel", "arbitrary"), 48),
        name="hgrn2",
    )(hg, lb, gn, tril3)


LOG2E = 1.4426950408889634
SB_SKIP_LOG2 = 160.0
SB_TERMS = 1
SB_NO_TILE = 1e30
SB_LINEAR_ABOVE = 64.0
SB_SKEW = 1


def _bf16_terms(x, n, axis=-1):
    terms, r = [], x
    for i in range(n):
        t = r.astype(BF16)
        terms.append(t)
        if i + 1 < n:
            r = r - t.astype(F32)
    return terms[0] if n == 1 else jnp.concatenate(terms, axis=axis)


def _sb_kernel(q_ref, k_ref, v_ref, o_ref, acc_ref, carry_ref, *, blk, npair):
    i = pl.program_id(1)
    lane = lax.broadcasted_iota(jnp.int32, (1, LANES), 1)
    first = lane < SB_HEAD_DIM
    t_idx = lax.broadcasted_iota(jnp.int32, (blk, blk), 0)
    s_idx = lax.broadcasted_iota(jnp.int32, (blk, blk), 1)
    after = jnp.concatenate([(t_idx > s_idx).astype(BF16)] * SB_TERMS, axis=0)
    causal = s_idx < t_idx
    nh = 2 * npair

    qs = []
    for p in range(npair):
        q = q_ref[0, :, p * LANES:(p + 1) * LANES]
        qs += [jnp.where(keep, q, jnp.zeros_like(q)) for keep in (first, jnp.logical_not(first))]

    def sweep(tiles, carries):
        ks, vs = [], []
        for j, _, _ in tiles:
            rows = pl.ds(pl.multiple_of(j * blk, blk), blk)
            ks.append([k_ref[0, rows, p * LANES:(p + 1) * LANES] for p in range(npair)])
            vs.append([v_ref[0, rows, p * LANES:(p + 1) * LANES] for p in range(npair)])
        units = [(n, h) for n in range(len(tiles)) for h in range(nh)]
        carries, pvs = list(carries), [None] * nh
        zs, logb, later, total = {}, {}, {}, {}

        def logits(u):
            n, h = u
            z = lax.dot_general(qs[h], ks[n][h // 2], _NT, preferred_element_type=F32)
            zs[u] = jnp.where(causal, z, -SB_NO_TILE) if tiles[n][1] else z

        def suffix_sums(u):
            z = zs.pop(u)
            sp = jnp.where(z > SB_LINEAR_ABOVE, z, jnp.log2(1.0 + jnp.exp2(z)))
            later[u] = jnp.dot(_bf16_terms(sp, SB_TERMS), after, preferred_element_type=F32)
            logb[u] = z - sp
            total[u] = later[u][:, 0:1] + sp[:, 0:1]

        def weigh(u):
            n, h = u
            if tiles[n][2] is not None:
                carries[h] = carries[h] + tiles[n][2]
            loga = logb.pop(u) - later.pop(u)
            a = jnp.exp2(loga if carries[h] is None else loga - carries[h])
            pv = jnp.dot(a.astype(BF16), vs[n][h // 2], preferred_element_type=F32)
            pvs[h] = pv if pvs[h] is None else pvs[h] + pv
            carries[h] = total[u] if carries[h] is None else carries[h] + total[u]

        for step in range(len(units) + 2 * SB_SKEW):
            for stage, fn in enumerate((logits, suffix_sums, weigh)):
                idx = step - stage * SB_SKEW
                if 0 <= idx < len(units):
                    fn(units[idx])
        done = jnp.min(functools.reduce(jnp.minimum, carries)) >= SB_SKIP_LOG2
        return pvs, carries, done

    def merge_heads(accs):
        return jnp.concatenate([jnp.where(first, accs[2 * p], accs[2 * p + 1]) for p in range(npair)],
                               axis=-1).astype(o_ref.dtype)

    no_prev = jnp.where(i == 0, jnp.float32(SB_NO_TILE), jnp.float32(0.0))
    accs, carries, done = sweep([(i, True, None), (jnp.maximum(i - 1, 0), False, no_prev)], [None] * nh)
    o_ref[0] = merge_heads(accs)
    for h in range(nh):
        acc_ref[h] = accs[h]
        carry_ref[h] = carries[h]

    @pl.when(jnp.logical_and(i >= 2, jnp.logical_not(done)))
    def _():
        def body(st):
            n, _ = st
            pvs, new_carries, done = sweep([(i - n, False, None)], [carry_ref[h] for h in range(nh)])
            for h in range(nh):
                acc_ref[h] += pvs[h]
                carry_ref[h] = new_carries[h]
            return n + 1, done.astype(jnp.int32)

        lax.while_loop(lambda st: jnp.logical_and(st[0] <= i, st[1] == 0), body, (jnp.int32(2), jnp.int32(0)))
        o_ref[0] = merge_heads([acc_ref[h] for h in range(nh)])


def _stickbreak(sb, blk):
    b, t, w3 = sb.shape
    w = w3 // 3
    return pl.pallas_call(
        functools.partial(_sb_kernel, blk=blk, npair=w // LANES),
        grid=(b, t // blk),
        in_specs=[pl.BlockSpec((1, blk, w), lambda bi, i: (bi, i, 0)),
                  pl.BlockSpec((1, t, w), lambda bi, i: (bi, 0, 1)),
                  pl.BlockSpec((1, t, w), lambda bi, i: (bi, 0, 2))],
        out_specs=pl.BlockSpec((1, blk, w), lambda bi, i: (bi, i, 0)),
        out_shape=jax.ShapeDtypeStruct((b, t, w), BF16),
        scratch_shapes=[pltpu.VMEM((SB_HEADS, blk, LANES), F32), pltpu.VMEM((SB_HEADS, blk, 1), F32)],
        compiler_params=_params(("parallel", "arbitrary"), 56),
        name="stickbreak",
    )(sb, sb, sb)


def _mix_kernel(x_ref, ya_ref, yb_ref, g_ref, wg_ref, wa_ref, wb_ref, wo_ref, o_ref, *, sub):
    d = x_ref.shape[-1]
    for r in range(0, x_ref.shape[0], sub):
        rows = slice(r, r + sub)
        x = x_ref[rows, :]
        h = _rms(x, g_ref[...]).astype(BF16)
        gate = _sigmoid(jnp.dot(h, wg_ref[...], preferred_element_type=F32))
        pa = jnp.dot(ya_ref[rows, :], wa_ref[...], preferred_element_type=F32)
        pb = jnp.dot(yb_ref[rows, :], wb_ref[...], preferred_element_type=F32)
        merged = gate[:, :d] * pa + gate[:, d:] * pb
        o_ref[rows, :] = x + jnp.dot(merged.astype(BF16), wo_ref[...], preferred_element_type=F32)


def _mix_out(x2d, ya, yb, g, wg, wa, wb, wo, tm, sub):
    m, d = x2d.shape
    wy = ya.shape[1]
    row = lambda n: pl.BlockSpec((tm, n), lambda i: (i, 0))
    return pl.pallas_call(
        functools.partial(_mix_kernel, sub=sub),
        grid=(m // tm,),
        in_specs=[row(d), row(wy), row(wy), _resident((1, d)), _resident(wg.shape),
                  _resident(wa.shape), _resident(wb.shape), _resident(wo.shape)],
        out_specs=row(d),
        out_shape=jax.ShapeDtypeStruct((m, d), F32),
        compiler_params=_params(("parallel",), 48),
        name="mix_out",
    )(x2d, ya, yb, g, wg, wa, wb, wo)


def _xattn_kernel(x_ref, g_ref, wq_ref, k_ref, v_ref, wo_ref, o_ref, *, sub):
    d = x_ref.shape[-1]
    dh = d // XATTN_HEADS
    for r in range(0, x_ref.shape[1], sub):
        x = x_ref[0, r:r + sub, :]
        h = _rms(x, g_ref[...]).astype(BF16)
        q = (jnp.dot(h, wq_ref[...], preferred_element_type=F32) * (dh ** -0.5)).astype(BF16)
        outs = []
        for e in range(XATTN_HEADS):
            hs = slice(e * dh, (e + 1) * dh)
            s = lax.dot_general(q[:, hs], k_ref[0, :, hs], _NT, preferred_element_type=F32)
            p = jnp.exp(s - jnp.max(s, axis=-1, keepdims=True))
            o = jnp.dot(p.astype(BF16), v_ref[0, :, hs], preferred_element_type=F32)
            outs.append(o / jnp.sum(p, axis=-1, keepdims=True))
        o = jnp.concatenate(outs, axis=-1).astype(BF16)
        o_ref[0, r:r + sub, :] = x + jnp.dot(o, wo_ref[...], preferred_element_type=F32)


def _xattn(x3d, g, wq, kmem, vmem, wo, tm, sub):
    b, t, d = x3d.shape
    nm = kmem.shape[1]
    return pl.pallas_call(
        functools.partial(_xattn_kernel, sub=sub),
        grid=(b, t // tm),
        in_specs=[pl.BlockSpec((1, tm, d), lambda bi, i: (bi, i, 0)), _resident((1, d)),
                  _resident(wq.shape),
                  pl.BlockSpec((1, nm, d), lambda bi, i: (bi, 0, 0)),
                  pl.BlockSpec((1, nm, d), lambda bi, i: (bi, 0, 0)),
                  _resident(wo.shape)],
        out_specs=pl.BlockSpec((1, tm, d), lambda bi, i: (bi, i, 0)),
        out_shape=jax.ShapeDtypeStruct((b, t, d), F32),
        compiler_params=_params(("parallel", "parallel"), 48),
        name="xattn",
    )(x3d, g, wq, kmem, vmem, wo)


def _ffn_kernel(x_ref, g_ref, wi_ref, wo_ref, gf_ref, o_ref, *, dff, final_norm, sub):
    for r in range(0, x_ref.shape[0], sub):
        x = x_ref[r:r + sub, :]
        h = _rms(x, g_ref[...]).astype(BF16)
        gate = jnp.dot(h, wi_ref[:, :dff], preferred_element_type=F32)
        up = jnp.dot(h, wi_ref[:, dff:], preferred_element_type=F32)
        act = (gate * _sigmoid(gate) * up).astype(BF16)
        y = x + jnp.dot(act, wo_ref[...], preferred_element_type=F32)
        o_ref[r:r + sub, :] = _rms(y, gf_ref[...]) if final_norm else y


def _ffn(x2d, g, wi, wo, gf, final_norm, tm, sub):
    m, d = x2d.shape
    dff = wo.shape[0]
    row = pl.BlockSpec((tm, d), lambda i: (i, 0))
    return pl.pallas_call(
        functools.partial(_ffn_kernel, dff=dff, final_norm=final_norm, sub=sub),
        grid=(m // tm,),
        in_specs=[row, _resident((1, d)), _resident(wi.shape), _resident(wo.shape), _resident((1, d))],
        out_specs=row,
        out_shape=jax.ShapeDtypeStruct((m, d), F32),
        compiler_params=_params(("parallel",), 56),
        name="ffn",
    )(x2d, g, wi, wo, gf)


def _tile(n, want):
    return want if n % want == 0 else n


def kernel(x, mem, g_mix, w_in, lb_table, g_hgrn, w_gate, w_proj_a, w_proj_b, w_out, g_xattn, g_mem,
           w_xq, w_xkv, w_xo, g_ffn, w_ffn_in, w_ffn_out, g_final):
    b, t, d = x.shape
    depth = w_in.shape[0]
    nm = mem.shape[1]
    hq = HGRN_HEADS * HGRN_D
    sbw = SB_HEADS * SB_HEAD_DIM
    row = lambda v: v.reshape(1, -1).astype(F32)
    lb_all = jnp.cumsum(jax.nn.softmax(lb_table.astype(F32), axis=0), axis=0)

    col_scale = jnp.ones((4 * hq + 3 * sbw,), F32).at[4 * hq:4 * hq + sbw].set(SB_HEAD_DIM ** -0.5 * LOG2E)

    blk_rows, sub_rows, ffn_sub_rows = _tile(t, 1024), _tile(t, 512), _tile(t, 256)

    x2d = x.reshape(b * t, d)
    for l in range(depth):
        hg, sb = _norm_matmul(x2d, row(g_mix[l]), (w_in[l] * col_scale).astype(BF16),
                              ((0, 4 * hq), (4 * hq, 4 * hq + 3 * sbw)), (F32, BF16), blk_rows, sub_rows)
        ya = _hgrn(hg.reshape(b, t, 4 * hq), row(lb_all[l]), row(g_hgrn[l]), blk_rows)
        yb = _stickbreak(sb.reshape(b, t, 3 * sbw), _tile(t, 256))
        x2d = _mix_out(x2d, ya.reshape(b * t, hq), yb.reshape(b * t, sbw), row(g_mix[l]),
                       w_gate[l].astype(BF16), w_proj_a[l].astype(BF16), w_proj_b[l].astype(BF16),
                       w_out[l].astype(BF16), blk_rows, sub_rows)
        kmem, vmem = _norm_matmul(mem.reshape(b * nm, d), row(g_mem[l]), w_xkv[l].astype(BF16),
                                  ((0, d), (d, 2 * d)), (BF16, BF16), nm, nm)
        x2d = _xattn(x2d.reshape(b, t, d), row(g_xattn[l]), w_xq[l].astype(BF16),
                     kmem.reshape(b, nm, d), vmem.reshape(b, nm, d), w_xo[l].astype(BF16),
                     blk_rows, sub_rows).reshape(b * t, d)
        x2d = _ffn(x2d, row(g_ffn[l]), w_ffn_in[l].astype(BF16), w_ffn_out[l].astype(BF16),
                   row(g_final), l == depth - 1, blk_rows, ffn_sub_rows)
    return x2d.reshape(b, t, d)
```

```python
import functools

import jax
import jax.numpy as jnp
import numpy as np
from jax import lax
from jax.experimental import pallas as pl
from jax.experimental.pallas import tpu as pltpu

F32 = jnp.float32
BF16 = jnp.bfloat16
EPS = 1e-6

CHUNK = 64
HGRN_HEADS = 4
HGRN_D = 128
SB_HEADS = 8
SB_HEAD_DIM = 64
XATTN_HEADS = 4

V7X_VMEM_BYTES = 64 * 1024 * 1024
LANES = 128

_NT = (((1,), (1,)), ((), ()))
_TN = (((0,), (0,)), ((), ()))


def _rms(x, g):
    return x * lax.rsqrt(jnp.mean(x * x, axis=-1, keepdims=True) + EPS) * g


def _sigmoid(x):
    return 1.0 / (1.0 + jnp.exp(-x))


def _params(sem, vmem_mb):
    return pltpu.CompilerParams(dimension_semantics=sem, vmem_limit_bytes=vmem_mb * 1024 * 1024)


def _resident(shape):
    nd = len(shape)
    return pl.BlockSpec(shape, lambda *_: (0,) * nd, pipeline_mode=pl.Buffered(1))


def _norm_matmul_kernel(x_ref, g_ref, w_ref, *out_refs, splits, sub):
    for r in range(0, x_ref.shape[0], sub):
        h = _rms(x_ref[r:r + sub, :], g_ref[...]).astype(BF16)
        for o_ref, (lo, hi) in zip(out_refs, splits):
            o_ref[r:r + sub, :] = jnp.dot(h, w_ref[:, lo:hi], preferred_element_type=F32).astype(o_ref.dtype)


def _norm_matmul(x2d, g, w, splits, dtypes, tm, sub):
    m, d = x2d.shape
    n = w.shape[1]
    return pl.pallas_call(
        functools.partial(_norm_matmul_kernel, splits=splits, sub=sub),
        grid=(m // tm,),
        in_specs=[pl.BlockSpec((tm, d), lambda i: (i, 0)), _resident((1, d)), _resident((d, n))],
        out_specs=[pl.BlockSpec((tm, hi - lo), lambda i: (i, 0)) for lo, hi in splits],
        out_shape=[jax.ShapeDtypeStruct((m, hi - lo), dt) for (lo, hi), dt in zip(splits, dtypes)],
        compiler_params=_params(("parallel",), 48),
        name="norm_matmul",
    )(x2d, g, w)


HGRN_LEVELS = (32, 16, 8)
HGRN_DIAG = 8


def _hgrn_masks():
    c = CHUNK
    t = lax.broadcasted_iota(jnp.int32, (c, c), 0)
    s = lax.broadcasted_iota(jnp.int32, (c, c), 1)
    masks = []
    for m in HGRN_LEVELS:
        same = (t // (2 * m)) == (s // (2 * m))
        masks.append(same & ((t % (2 * m)) >= m) & ((s % (2 * m)) < m))
    masks.append(((t // HGRN_DIAG) == (s // HGRN_DIAG)) & (s <= t))
    return masks


HGRN_CUMSUM_TERMS = 2
HGRN_GROUP = 1


def _silu(x):
    hx = 0.5 * x
    return hx + hx * jnp.tanh(hx)


def _hgrn_kernel(hg_ref, lb_ref, gn_ref, tril_ref, o_ref, st_ref, *, n_chunks, group):
    c, nh, dh = CHUNK, HGRN_HEADS, HGRN_D
    w = nh * dh

    @pl.when(pl.program_id(1) == 0)
    def _():
        st_ref[...] = jnp.zeros_like(st_ref)

    masks = _hgrn_masks()
    states = [st_ref[h] for h in range(nh)]
    groups = [_hgrn_group(hg_ref, lb_ref, gn_ref, tril_ref, o_ref, states, masks, c0, group)
              for c0 in range(0, n_chunks, group)]
    for step in range(len(groups) + 2):
        for phase in range(3):
            g = step - phase
            if 0 <= g < len(groups):
                next(groups[g], None)
    for h in range(nh):
        st_ref[h] = states[h]


def _hgrn_group(hg_ref, lb_ref, gn_ref, tril_ref, o_ref, states, masks, c0, n):
    c, nh, dh = CHUNK, HGRN_HEADS, HGRN_D
    w = nh * dh

    def wide(col):
        return jnp.concatenate(
            [hg_ref[0, (c0 + ci) * c:(c0 + ci + 1) * c, col * w:(col + 1) * w] for ci in range(n)], axis=1)

    def tiled(row):
        return jnp.concatenate([row] * n, axis=1)

    qp, fp, iv, gp = wide(0), wide(1), wide(2), wide(3)
    lb = tiled(lb_ref[...])
    q = _silu(qp)
    f = 0.5 * (1.0 + lb) + (0.5 * (1.0 - lb)) * jnp.tanh(0.5 * fp)
    k = 1.0 - f
    b = jnp.dot(tril_ref[...], _bf16_terms(jnp.log(f) * LOG2E, HGRN_CUMSUM_TERMS, axis=0),
                preferred_element_type=F32)

    units = [(ci, h) for ci in range(n) for h in range(nh)]
    lanes = {u: slice(u[0] * w + u[1] * dh, u[0] * w + (u[1] + 1) * dh) for u in units}
    e_b = jnp.exp2(b)
    e_last = e_b[c - 1:c, :]
    q_in = (q * e_b).astype(BF16)
    k_out = (k * jnp.exp2(b[c - 1:c, :] - b)).astype(BF16)
    v16 = iv.astype(BF16)
    sides = []
    for m in HGRN_LEVELS:
        decay, qk = [], []
        for s0 in range(0, c, 2 * m):
            bp = b[s0 + m:s0 + m + 1, :]
            decay += [bp - b[s0:s0 + m, :], b[s0 + m:s0 + 2 * m, :] - bp]
            qk += [k[s0:s0 + m, :], q[s0 + m:s0 + 2 * m, :]]
        both = (jnp.concatenate(qk, axis=0) * jnp.exp2(jnp.concatenate(decay, axis=0))).astype(BF16)
        sides.append((both, both))
    half = HGRN_DIAG // 2
    dd = jnp.concatenate([b[s0:s0 + HGRN_DIAG, :] - b[s0 + half:s0 + half + 1, :]
                          for s0 in range(0, c, HGRN_DIAG)], axis=0)
    sides.append(((q * jnp.exp2(dd)).astype(BF16), (k * jnp.exp2(-dd)).astype(BF16)))
    gate = tiled(gn_ref[...]) * _silu(gp)
    yield

    kv = {u: lax.dot_general(v16[:, lanes[u]], k_out[:, lanes[u]], _TN, preferred_element_type=F32)
          for u in units}
    inter = {}
    for h in range(nh):
        for ci in range(n):
            u = (ci, h)
            inter[u] = lax.dot_general(q_in[:, lanes[u]], states[h].astype(BF16), _NT,
                                       preferred_element_type=F32)
            states[h] = e_last[:, lanes[u]] * states[h] + kv[u]
    parts = {u: [lax.dot_general(a_l[:, lanes[u]], b_l[:, lanes[u]], _NT, preferred_element_type=F32)
                 for a_l, b_l in sides] for u in units}
    yield

    scores = {}
    for u in units:
        s_u = jnp.zeros((c, c), F32)
        for s_l, m_l in zip(parts[u], masks):
            s_u = jnp.where(m_l, s_l, s_u)
        scores[u] = s_u.astype(BF16)
    for ci in range(n):
        outs = []
        for h in range(nh):
            u = (ci, h)
            o = jnp.dot(scores[u], v16[:, lanes[u]], preferred_element_type=F32) + inter[u]
            outs.append(o * lax.rsqrt(jnp.mean(o * o, axis=-1, keepdims=True) + EPS))
        on = jnp.concatenate(outs, axis=-1) * gate[:, ci * w:(ci + 1) * w]
        o_ref[0, (c0 + ci) * c:(c0 + ci + 1) * c, :] = on.astype(o_ref.dtype)


def _hgrn(hg, lb, gn, rows_per_step):
    b, t, w4 = hg.shape
    w = w4 // 4
    tril = np.tril(np.ones((CHUNK, CHUNK), np.float32))
    tril3 = jnp.asarray(np.concatenate([tril] * HGRN_CUMSUM_TERMS, axis=1), BF16)
    return pl.pallas_call(
        functools.partial(_hgrn_kernel, n_chunks=rows_per_step // CHUNK, group=HGRN_GROUP),
        grid=(b, t // rows_per_step),
        in_specs=[pl.BlockSpec((1, rows_per_step, w4), lambda bi, i: (bi, i, 0)),
                  _resident((1, w)), _resident((1, w)), _resident(tril3.shape)],
        out_specs=pl.BlockSpec((1, rows_per_step, w), lambda bi, i: (bi, i, 0)),
        out_shape=jax.ShapeDtypeStruct((b, t, w), BF16),
        scratch_shapes=[pltpu.VMEM((HGRN_HEADS, HGRN_D, HGRN_D), F32)],
        compiler_params=_params(("parallel", "arbitrary"), 48),
        name="hgrn2",
    )(hg, lb, gn, tril3)


LOG2E = 1.4426950408889634
SB_SKIP_LOG2 = 160.0
SB_TERMS = 1
SB_NO_TILE = 1e30
SB_LINEAR_ABOVE = 64.0
SB_SKEW = 1


def _bf16_terms(x, n, axis=-1):
    terms, r = [], x
    for i in range(n):
        t = r.astype(BF16)
        terms.append(t)
        if i + 1 < n:
            r = r - t.astype(F32)
    return terms[0] if n == 1 else jnp.concatenate(terms, axis=axis)


def _sb_kernel(q_ref, k_ref, v_ref, o_ref, acc_ref, carry_ref, *, blk, npair):
    i = pl.program_id(1)
    lane = lax.broadcasted_iota(jnp.int32, (1, LANES), 1)
    first = lane < SB_HEAD_DIM
    t_idx = lax.broadcasted_iota(jnp.int32, (blk, blk), 0)
    s_idx = lax.broadcasted_iota(jnp.int32, (blk, blk), 1)
    after = jnp.concatenate([(t_idx > s_idx).astype(BF16)] * SB_TERMS, axis=0)
    causal = s_idx < t_idx
    nh = 2 * npair

    qs = []
    for p in range(npair):
        q = q_ref[0, :, p * LANES:(p + 1) * LANES]
        qs += [jnp.where(keep, q, jnp.zeros_like(q)) for keep in (first, jnp.logical_not(first))]

    def sweep(tiles, carries):
        ks, vs = [], []
        for j, _, _ in tiles:
            rows = pl.ds(pl.multiple_of(j * blk, blk), blk)
            ks.append([k_ref[0, rows, p * LANES:(p + 1) * LANES] for p in range(npair)])
            vs.append([v_ref[0, rows, p * LANES:(p + 1) * LANES] for p in range(npair)])
        units = [(n, h) for n in range(len(tiles)) for h in range(nh)]
        carries, pvs = list(carries), [None] * nh
        zs, logb, later, total = {}, {}, {}, {}

        def logits(u):
            n, h = u
            z = lax.dot_general(qs[h], ks[n][h // 2], _NT, preferred_element_type=F32)
            zs[u] = jnp.where(causal, z, -SB_NO_TILE) if tiles[n][1] else z

        def suffix_sums(u):
            z = zs.pop(u)
            sp = jnp.where(z > SB_LINEAR_ABOVE, z, jnp.log2(1.0 + jnp.exp2(z)))
            later[u] = jnp.dot(_bf16_terms(sp, SB_TERMS), after, preferred_element_type=F32)
            logb[u] = z - sp
            total[u] = later[u][:, 0:1] + sp[:, 0:1]

        def weigh(u):
            n, h = u
            if tiles[n][2] is not None:
                carries[h] = carries[h] + tiles[n][2]
            loga = logb.pop(u) - later.pop(u)
            a = jnp.exp2(loga if carries[h] is None else loga - carries[h])
            pv = jnp.dot(a.astype(BF16), vs[n][h // 2], preferred_element_type=F32)
            pvs[h] = pv if pvs[h] is None else pvs[h] + pv
            carries[h] = total[u] if carries[h] is None else carries[h] + total[u]

        for step in range(len(units) + 2 * SB_SKEW):
            for stage, fn in enumerate((logits, suffix_sums, weigh)):
                idx = step - stage * SB_SKEW
                if 0 <= idx < len(units):
                    fn(units[idx])
        done = jnp.min(functools.reduce(jnp.minimum, carries)) >= SB_SKIP_LOG2
        return pvs, carries, done

    def merge_heads(accs):
        return jnp.concatenate([jnp.where(first, accs[2 * p], accs[2 * p + 1]) for p in range(npair)],
                               axis=-1).astype(o_ref.dtype)

    no_prev = jnp.where(i == 0, jnp.float32(SB_NO_TILE), jnp.float32(0.0))
    accs, carries, done = sweep([(i, True, None), (jnp.maximum(i - 1, 0), False, no_prev)], [None] * nh)
    o_ref[0] = merge_heads(accs)
    for h in range(nh):
        acc_ref[h] = accs[h]
        carry_ref[h] = carries[h]

    @pl.when(jnp.logical_and(i >= 2, jnp.logical_not(done)))
    def _():
        def body(st):
            n, _ = st
            pvs, new_carries, done = sweep([(i - n, False, None)], [carry_ref[h] for h in range(nh)])
            for h in range(nh):
                acc_ref[h] += pvs[h]
                carry_ref[h] = new_carries[h]
            return n + 1, done.astype(jnp.int32)

        lax.while_loop(lambda st: jnp.logical_and(st[0] <= i, st[1] == 0), body, (jnp.int32(2), jnp.int32(0)))
        o_ref[0] = merge_heads([acc_ref[h] for h in range(nh)])


def _stickbreak(sb, blk):
    b, t, w3 = sb.shape
    w = w3 // 3
    return pl.pallas_call(
        functools.partial(_sb_kernel, blk=blk, npair=w // LANES),
        grid=(b, t // blk),
        in_specs=[pl.BlockSpec((1, blk, w), lambda bi, i: (bi, i, 0)),
                  pl.BlockSpec((1, t, w), lambda bi, i: (bi, 0, 1)),
                  pl.BlockSpec((1, t, w), lambda bi, i: (bi, 0, 2))],
        out_specs=pl.BlockSpec((1, blk, w), lambda bi, i: (bi, i, 0)),
        out_shape=jax.ShapeDtypeStruct((b, t, w), BF16),
        scratch_shapes=[pltpu.VMEM((SB_HEADS, blk, LANES), F32), pltpu.VMEM((SB_HEADS, blk, 1), F32)],
        compiler_params=_params(("parallel", "arbitrary"), 56),
        name="stickbreak",
    )(sb, sb, sb)


def _mix_kernel(x_ref, ya_ref, yb_ref, g_ref, wg_ref, wa_ref, wb_ref, wo_ref, o_ref, *, sub):
    d = x_ref.shape[-1]
    for r in range(0, x_ref.shape[0], sub):
        rows = slice(r, r + sub)
        x = x_ref[rows, :]
        h = _rms(x, g_ref[...]).astype(BF16)
        gate = _sigmoid(jnp.dot(h, wg_ref[...], preferred_element_type=F32))
        pa = jnp.dot(ya_ref[rows, :], wa_ref[...], preferred_element_type=F32)
        pb = jnp.dot(yb_ref[rows, :], wb_ref[...], preferred_element_type=F32)
        merged = gate[:, :d] * pa + gate[:, d:] * pb
        o_ref[rows, :] = x + jnp.dot(merged.astype(BF16), wo_ref[...], preferred_element_type=F32)


def _mix_out(x2d, ya, yb, g, wg, wa, wb, wo, tm, sub):
    m, d = x2d.shape
    wy = ya.shape[1]
    row = lambda n: pl.BlockSpec((tm, n), lambda i: (i, 0))
    return pl.pallas_call(
        functools.partial(_mix_kernel, sub=sub),
        grid=(m // tm,),
        in_specs=[row(d), row(wy), row(wy), _resident((1, d)), _resident(wg.shape),
                  _resident(wa.shape), _resident(wb.shape), _resident(wo.shape)],
        out_specs=row(d),
        out_shape=jax.ShapeDtypeStruct((m, d), F32),
        compiler_params=_params(("parallel",), 48),
        name="mix_out",
    )(x2d, ya, yb, g, wg, wa, wb, wo)


def _mem_fold_kernel(m_ref, g_ref, wkv_ref, wq_ref, wo_ref, sk_ref, vo_ref):
    d = m_ref.shape[-1]
    dh = d // XATTN_HEADS
    nm = m_ref.shape[1]
    m = _rms(m_ref[0], g_ref[...]).astype(BF16)
    kv = jnp.dot(m, wkv_ref[...], preferred_element_type=F32).astype(BF16)
    for e in range(XATTN_HEADS):
        hs = slice(e * dh, (e + 1) * dh)
        sk = lax.dot_general(wq_ref[:, hs], kv[:, hs], _NT, preferred_element_type=F32)
        sk_ref[0, :, e * nm:(e + 1) * nm] = (sk * (dh ** -0.5)).astype(sk_ref.dtype)
        vo = jnp.dot(kv[:, d + e * dh:d + (e + 1) * dh], wo_ref[hs, :], preferred_element_type=F32)
        vo_ref[0, e * nm:(e + 1) * nm, :] = vo.astype(vo_ref.dtype)


def _mem_fold(mem, g, wkv, wq, wo):
    b, nm, d = mem.shape
    hn = XATTN_HEADS * nm
    return pl.pallas_call(
        _mem_fold_kernel,
        grid=(b,),
        in_specs=[pl.BlockSpec((1, nm, d), lambda bi: (bi, 0, 0)), _resident((1, d)),
                  _resident(wkv.shape), _resident(wq.shape), _resident(wo.shape)],
        out_specs=[pl.BlockSpec((1, d, hn), lambda bi: (bi, 0, 0)),
                   pl.BlockSpec((1, hn, d), lambda bi: (bi, 0, 0))],
        out_shape=[jax.ShapeDtypeStruct((b, d, hn), BF16), jax.ShapeDtypeStruct((b, hn, d), BF16)],
        compiler_params=_params(("parallel",), 48),
        name="mem_fold",
    )(mem, g, wkv, wq, wo)


def _xattn_kernel(x_ref, g_ref, sk_ref, vo_ref, o_ref, *, sub):
    nm = sk_ref.shape[-1] // XATTN_HEADS
    for r in range(0, x_ref.shape[1], sub):
        x = x_ref[0, r:r + sub, :]
        h = _rms(x, g_ref[...]).astype(BF16)
        s = jnp.dot(h, sk_ref[0], preferred_element_type=F32)
        probs = []
        for e in range(XATTN_HEADS):
            s_e = s[:, e * nm:(e + 1) * nm]
            p = jnp.exp(s_e - jnp.max(s_e, axis=-1, keepdims=True))
            probs.append((p / jnp.sum(p, axis=-1, keepdims=True)).astype(BF16))
        o_ref[0, r:r + sub, :] = x + jnp.dot(jnp.concatenate(probs, axis=-1), vo_ref[0],
                                             preferred_element_type=F32)


def _xattn(x3d, g, sk, vo, tm, sub):
    b, t, d = x3d.shape
    hn = sk.shape[-1]
    return pl.pallas_call(
        functools.partial(_xattn_kernel, sub=sub),
        grid=(b, t // tm),
        in_specs=[pl.BlockSpec((1, tm, d), lambda bi, i: (bi, i, 0)), _resident((1, d)),
                  pl.BlockSpec((1, d, hn), lambda bi, i: (bi, 0, 0)),
                  pl.BlockSpec((1, hn, d), lambda bi, i: (bi, 0, 0))],
        out_specs=pl.BlockSpec((1, tm, d), lambda bi, i: (bi, i, 0)),
        out_shape=jax.ShapeDtypeStruct((b, t, d), F32),
        compiler_params=_params(("parallel", "parallel"), 48),
        name="xattn",
    )(x3d, g, sk, vo)


def _ffn_kernel(x_ref, g_ref, wi_ref, wo_ref, gf_ref, o_ref, *, dff, final_norm, sub):
    for r in range(0, x_ref.shape[0], sub):
        x = x_ref[r:r + sub, :]
        h = _rms(x, g_ref[...]).astype(BF16)
        gate = jnp.dot(h, wi_ref[:, :dff], preferred_element_type=F32)
        up = jnp.dot(h, wi_ref[:, dff:], preferred_element_type=F32)
        act = (gate * _sigmoid(gate) * up).astype(BF16)
        y = x + jnp.dot(act, wo_ref[...], preferred_element_type=F32)
        o_ref[r:r + sub, :] = _rms(y, gf_ref[...]) if final_norm else y


def _ffn(x2d, g, wi, wo, gf, final_norm, tm, sub):
    m, d = x2d.shape
    dff = wo.shape[0]
    row = pl.BlockSpec((tm, d), lambda i: (i, 0))
    return pl.pallas_call(
        functools.partial(_ffn_kernel, dff=dff, final_norm=final_norm, sub=sub),
        grid=(m // tm,),
        in_specs=[row, _resident((1, d)), _resident(wi.shape), _resident(wo.shape), _resident((1, d))],
        out_specs=row,
        out_shape=jax.ShapeDtypeStruct((m, d), F32),
        compiler_params=_params(("parallel",), 56),
        name="ffn",
    )(x2d, g, wi, wo, gf)


def _tile(n, want):
    return want if n % want == 0 else n


def kernel(x, mem, g_mix, w_in, lb_table, g_hgrn, w_gate, w_proj_a, w_proj_b, w_out, g_xattn, g_mem,
           w_xq, w_xkv, w_xo, g_ffn, w_ffn_in, w_ffn_out, g_final):
    b, t, d = x.shape
    depth = w_in.shape[0]
    nm = mem.shape[1]
    hq = HGRN_HEADS * HGRN_D
    sbw = SB_HEADS * SB_HEAD_DIM
    row = lambda v: v.reshape(1, -1).astype(F32)
    lb_all = jnp.cumsum(jax.nn.softmax(lb_table.astype(F32), axis=0), axis=0)

    col_scale = jnp.ones((4 * hq + 3 * sbw,), F32).at[4 * hq:4 * hq + sbw].set(SB_HEAD_DIM ** -0.5 * LOG2E)

    blk_rows, sub_rows, ffn_sub_rows = _tile(t, 1024), _tile(t, 512), _tile(t, 256)

    x2d = x.reshape(b * t, d)
    for l in range(depth):
        hg, sb = _norm_matmul(x2d, row(g_mix[l]), (w_in[l] * col_scale).astype(BF16),
                              ((0, 4 * hq), (4 * hq, 4 * hq + 3 * sbw)), (F32, BF16), blk_rows, sub_rows)
        ya = _hgrn(hg.reshape(b, t, 4 * hq), row(lb_all[l]), row(g_hgrn[l]), blk_rows)
        yb = _stickbreak(sb.reshape(b, t, 3 * sbw), _tile(t, 256))
        x2d = _mix_out(x2d, ya.reshape(b * t, hq), yb.reshape(b * t, sbw), row(g_mix[l]),
                       w_gate[l].astype(BF16), w_proj_a[l].astype(BF16), w_proj_b[l].astype(BF16),
                       w_out[l].astype(BF16), blk_rows, sub_rows)
        score_w, out_w = _mem_fold(mem, row(g_mem[l]), w_xkv[l].astype(BF16), w_xq[l].astype(BF16),
                                   w_xo[l].astype(BF16))
        x2d = _xattn(x2d.reshape(b, t, d), row(g_xattn[l]), score_w, out_w,
                     blk_rows, sub_rows).reshape(b * t, d)
        x2d = _ffn(x2d, row(g_ffn[l]), w_ffn_in[l].astype(BF16), w_ffn_out[l].astype(BF16),
                   row(g_final), l == depth - 1, blk_rows, ffn_sub_rows)
    return x2d.reshape(b, t, d)
```

```python
import functools

import jax
import jax.numpy as jnp
import numpy as np
from jax import lax
from jax.experimental import pallas as pl
from jax.experimental.pallas import tpu as pltpu

F32 = jnp.float32
BF16 = jnp.bfloat16
EPS = 1e-6

CHUNK = 64
HGRN_HEADS = 4
HGRN_D = 128
SB_HEADS = 8
SB_HEAD_DIM = 64
XATTN_HEADS = 4

V7X_VMEM_BYTES = 64 * 1024 * 1024
LANES = 128

_NT = (((1,), (1,)), ((), ()))
_TN = (((0,), (0,)), ((), ()))


def _rms(x, g):
    return x * lax.rsqrt(jnp.mean(x * x, axis=-1, keepdims=True) + EPS) * g


def _sigmoid(x):
    return 1.0 / (1.0 + jnp.exp(-x))


def _params(sem, vmem_mb):
    return pltpu.CompilerParams(dimension_semantics=sem, vmem_limit_bytes=vmem_mb * 1024 * 1024)


def _resident(shape):
    nd = len(shape)
    return pl.BlockSpec(shape, lambda *_: (0,) * nd, pipeline_mode=pl.Buffered(1))


def _norm_matmul_kernel(x_ref, g_ref, w_ref, *out_refs, splits, sub):
    for r in range(0, x_ref.shape[0], sub):
        h = _rms(x_ref[r:r + sub, :], g_ref[...]).astype(BF16)
        for o_ref, (lo, hi) in zip(out_refs, splits):
            o_ref[r:r + sub, :] = jnp.dot(h, w_ref[:, lo:hi], preferred_element_type=F32).astype(o_ref.dtype)


def _norm_matmul(x2d, g, w, splits, dtypes, tm, sub):
    m, d = x2d.shape
    n = w.shape[1]
    return pl.pallas_call(
        functools.partial(_norm_matmul_kernel, splits=splits, sub=sub),
        grid=(m // tm,),
        in_specs=[pl.BlockSpec((tm, d), lambda i: (i, 0)), _resident((1, d)), _resident((d, n))],
        out_specs=[pl.BlockSpec((tm, hi - lo), lambda i: (i, 0)) for lo, hi in splits],
        out_shape=[jax.ShapeDtypeStruct((m, hi - lo), dt) for (lo, hi), dt in zip(splits, dtypes)],
        compiler_params=_params(("parallel",), 48),
        name="norm_matmul",
    )(x2d, g, w)


HGRN_LEVELS = (32, 16, 8)
HGRN_DIAG = 8


def _hgrn_masks():
    c = CHUNK
    t = lax.broadcasted_iota(jnp.int32, (c, c), 0)
    s = lax.broadcasted_iota(jnp.int32, (c, c), 1)
    masks = []
    for m in HGRN_LEVELS:
        same = (t // (2 * m)) == (s // (2 * m))
        masks.append(same & ((t % (2 * m)) >= m) & ((s % (2 * m)) < m))
    masks.append(((t // HGRN_DIAG) == (s // HGRN_DIAG)) & (s <= t))
    return masks


HGRN_CUMSUM_TERMS = 2
HGRN_GROUP = 1


def _silu(x):
    hx = 0.5 * x
    return hx + hx * jnp.tanh(hx)


def _hgrn_kernel(hg_ref, lb_ref, gn_ref, tril_ref, o_ref, st_ref, *, n_chunks, group):
    c, nh, dh = CHUNK, HGRN_HEADS, HGRN_D
    w = nh * dh

    @pl.when(pl.program_id(1) == 0)
    def _():
        st_ref[...] = jnp.zeros_like(st_ref)

    masks = _hgrn_masks()
    states = [st_ref[h] for h in range(nh)]
    groups = [_hgrn_group(hg_ref, lb_ref, gn_ref, tril_ref, o_ref, states, masks, c0, group)
              for c0 in range(0, n_chunks, group)]
    for step in range(len(groups) + 2):
        for phase in range(3):
            g = step - phase
            if 0 <= g < len(groups):
                next(groups[g], None)
    for h in range(nh):
        st_ref[h] = states[h]


def _hgrn_group(hg_ref, lb_ref, gn_ref, tril_ref, o_ref, states, masks, c0, n):
    c, nh, dh = CHUNK, HGRN_HEADS, HGRN_D
    w = nh * dh

    def wide(col):
        return jnp.concatenate(
            [hg_ref[0, (c0 + ci) * c:(c0 + ci + 1) * c, col * w:(col + 1) * w] for ci in range(n)], axis=1)

    def tiled(row):
        return jnp.concatenate([row] * n, axis=1)

    qp, fp, iv, gp = wide(0), wide(1), wide(2), wide(3)
    lb = tiled(lb_ref[...])
    q = _silu(qp)
    f = 0.5 * (1.0 + lb) + (0.5 * (1.0 - lb)) * jnp.tanh(0.5 * fp)
    k = 1.0 - f
    b = jnp.dot(tril_ref[...], _bf16_terms(jnp.log(f) * LOG2E, HGRN_CUMSUM_TERMS, axis=0),
                preferred_element_type=F32)

    units = [(ci, h) for ci in range(n) for h in range(nh)]
    lanes = {u: slice(u[0] * w + u[1] * dh, u[0] * w + (u[1] + 1) * dh) for u in units}
    e_b = jnp.exp2(b)
    e_last = e_b[c - 1:c, :]
    q_in = (q * e_b).astype(BF16)
    k_out = (k * jnp.exp2(b[c - 1:c, :] - b)).astype(BF16)
    v16 = iv.astype(BF16)
    sides = []
    for m in HGRN_LEVELS:
        decay, qk = [], []
        for s0 in range(0, c, 2 * m):
            bp = b[s0 + m:s0 + m + 1, :]
            decay += [bp - b[s0:s0 + m, :], b[s0 + m:s0 + 2 * m, :] - bp]
            qk += [k[s0:s0 + m, :], q[s0 + m:s0 + 2 * m, :]]
        both = (jnp.concatenate(qk, axis=0) * jnp.exp2(jnp.concatenate(decay, axis=0))).astype(BF16)
        sides.append((both, both))
    half = HGRN_DIAG // 2
    dd = jnp.concatenate([b[s0:s0 + HGRN_DIAG, :] - b[s0 + half:s0 + half + 1, :]
                          for s0 in range(0, c, HGRN_DIAG)], axis=0)
    sides.append(((q * jnp.exp2(dd)).astype(BF16), (k * jnp.exp2(-dd)).astype(BF16)))
    gate = tiled(gn_ref[...]) * _silu(gp)
    yield

    kv = {u: lax.dot_general(v16[:, lanes[u]], k_out[:, lanes[u]], _TN, preferred_element_type=F32)
          for u in units}
    inter = {}
    for h in range(nh):
        for ci in range(n):
            u = (ci, h)
            inter[u] = lax.dot_general(q_in[:, lanes[u]], states[h].astype(BF16), _NT,
                                       preferred_element_type=F32)
            states[h] = e_last[:, lanes[u]] * states[h] + kv[u]
    parts = {u: [lax.dot_general(a_l[:, lanes[u]], b_l[:, lanes[u]], _NT, preferred_element_type=F32)
                 for a_l, b_l in sides] for u in units}
    yield

    scores = {}
    for u in units:
        s_u = jnp.zeros((c, c), F32)
        for s_l, m_l in zip(parts[u], masks):
            s_u = jnp.where(m_l, s_l, s_u)
        scores[u] = s_u.astype(BF16)
    for ci in range(n):
        outs = []
        for h in range(nh):
            u = (ci, h)
            o = jnp.dot(scores[u], v16[:, lanes[u]], preferred_element_type=F32) + inter[u]
            outs.append(o * lax.rsqrt(jnp.mean(o * o, axis=-1, keepdims=True) + EPS))
        on = jnp.concatenate(outs, axis=-1) * gate[:, ci * w:(ci + 1) * w]
        o_ref[0, (c0 + ci) * c:(c0 + ci + 1) * c, :] = on.astype(o_ref.dtype)


def _hgrn(hg, lb, gn, rows_per_step):
    b, t, w4 = hg.shape
    w = w4 // 4
    tril = np.tril(np.ones((CHUNK, CHUNK), np.float32))
    tril3 = jnp.asarray(np.concatenate([tril] * HGRN_CUMSUM_TERMS, axis=1), BF16)
    return pl.pallas_call(
        functools.partial(_hgrn_kernel, n_chunks=rows_per_step // CHUNK, group=HGRN_GROUP),
        grid=(b, t // rows_per_step),
        in_specs=[pl.BlockSpec((1, rows_per_step, w4), lambda bi, i: (bi, i, 0)),
                  _resident((1, w)), _resident((1, w)), _resident(tril3.shape)],
        out_specs=pl.BlockSpec((1, rows_per_step, w), lambda bi, i: (bi, i, 0)),
        out_shape=jax.ShapeDtypeStruct((b, t, w), BF16),
        scratch_shapes=[pltpu.VMEM((HGRN_HEADS, HGRN_D, HGRN_D), F32)],
        compiler_params=_params(("parallel", "arbitrary"), 48),
        name="hgrn2",
    )(hg, lb, gn, tril3)


LOG2E = 1.4426950408889634
SB_SKIP_LOG2 = 160.0
SB_TERMS = 1
SB_NO_TILE = 1e30
SB_LINEAR_ABOVE = 64.0
SB_SKEW = 1


def _bf16_terms(x, n, axis=-1):
    terms, r = [], x
    for i in range(n):
        t = r.astype(BF16)
        terms.append(t)
        if i + 1 < n:
            r = r - t.astype(F32)
    return terms[0] if n == 1 else jnp.concatenate(terms, axis=axis)


def _sb_kernel(q_ref, k_ref, v_ref, o_ref, acc_ref, carry_ref, *, blk, npair):
    i = pl.program_id(1)
    lane = lax.broadcasted_iota(jnp.int32, (1, LANES), 1)
    first = lane < SB_HEAD_DIM
    t_idx = lax.broadcasted_iota(jnp.int32, (blk, blk), 0)
    s_idx = lax.broadcasted_iota(jnp.int32, (blk, blk), 1)
    after = jnp.concatenate([(t_idx > s_idx).astype(BF16)] * SB_TERMS, axis=0)
    causal = s_idx < t_idx
    nh = 2 * npair

    qs = []
    for p in range(npair):
        q = q_ref[0, :, p * LANES:(p + 1) * LANES]
        qs += [jnp.where(keep, q, jnp.zeros_like(q)) for keep in (first, jnp.logical_not(first))]

    def sweep(tiles, carries):
        ks, vs = [], []
        for j, _, _ in tiles:
            rows = pl.ds(pl.multiple_of(j * blk, blk), blk)
            ks.append([k_ref[0, rows, p * LANES:(p + 1) * LANES] for p in range(npair)])
            vs.append([v_ref[0, rows, p * LANES:(p + 1) * LANES] for p in range(npair)])
        units = [(n, h) for n in range(len(tiles)) for h in range(nh)]
        carries, pvs = list(carries), [None] * nh
        zs, logb, later, total = {}, {}, {}, {}

        def logits(u):
            n, h = u
            z = lax.dot_general(qs[h], ks[n][h // 2], _NT, preferred_element_type=F32)
            zs[u] = jnp.where(causal, z, -SB_NO_TILE) if tiles[n][1] else z

        def suffix_sums(u):
            z = zs.pop(u)
            sp = jnp.where(z > SB_LINEAR_ABOVE, z, jnp.log2(1.0 + jnp.exp2(z)))
            later[u] = jnp.dot(_bf16_terms(sp, SB_TERMS), after, preferred_element_type=F32)
            logb[u] = z - sp
            total[u] = later[u][:, 0:1] + sp[:, 0:1]

        def weigh(u):
            n, h = u
            if tiles[n][2] is not None:
                carries[h] = carries[h] + tiles[n][2]
            loga = logb.pop(u) - later.pop(u)
            a = jnp.exp2(loga if carries[h] is None else loga - carries[h])
            pv = jnp.dot(a.astype(BF16), vs[n][h // 2], preferred_element_type=F32)
            pvs[h] = pv if pvs[h] is None else pvs[h] + pv
            carries[h] = total[u] if carries[h] is None else carries[h] + total[u]

        for step in range(len(units) + 2 * SB_SKEW):
            for stage, fn in enumerate((logits, suffix_sums, weigh)):
                idx = step - stage * SB_SKEW
                if 0 <= idx < len(units):
                    fn(units[idx])
        done = jnp.min(functools.reduce(jnp.minimum, carries)) >= SB_SKIP_LOG2
        return pvs, carries, done

    def merge_heads(accs):
        return jnp.concatenate([jnp.where(first, accs[2 * p], accs[2 * p + 1]) for p in range(npair)],
                               axis=-1).astype(o_ref.dtype)

    no_prev = jnp.where(i == 0, jnp.float32(SB_NO_TILE), jnp.float32(0.0))
    accs, carries, done = sweep([(i, True, None), (jnp.maximum(i - 1, 0), False, no_prev)], [None] * nh)
    o_ref[0] = merge_heads(accs)
    for h in range(nh):
        acc_ref[h] = accs[h]
        carry_ref[h] = carries[h]

    @pl.when(jnp.logical_and(i >= 2, jnp.logical_not(done)))
    def _():
        def body(st):
            n, _ = st
            pvs, new_carries, done = sweep([(i - n, False, None)], [carry_ref[h] for h in range(nh)])
            for h in range(nh):
                acc_ref[h] += pvs[h]
                carry_ref[h] = new_carries[h]
            return n + 1, done.astype(jnp.int32)

        lax.while_loop(lambda st: jnp.logical_and(st[0] <= i, st[1] == 0), body, (jnp.int32(2), jnp.int32(0)))
        o_ref[0] = merge_heads([acc_ref[h] for h in range(nh)])


def _stickbreak(sb, blk):
    b, t, w3 = sb.shape
    w = w3 // 3
    return pl.pallas_call(
        functools.partial(_sb_kernel, blk=blk, npair=w // LANES),
        grid=(b, t // blk),
        in_specs=[pl.BlockSpec((1, blk, w), lambda bi, i: (bi, i, 0)),
                  pl.BlockSpec((1, t, w), lambda bi, i: (bi, 0, 1)),
                  pl.BlockSpec((1, t, w), lambda bi, i: (bi, 0, 2))],
        out_specs=pl.BlockSpec((1, blk, w), lambda bi, i: (bi, i, 0)),
        out_shape=jax.ShapeDtypeStruct((b, t, w), BF16),
        scratch_shapes=[pltpu.VMEM((SB_HEADS, blk, LANES), F32), pltpu.VMEM((SB_HEADS, blk, 1), F32)],
        compiler_params=_params(("parallel", "arbitrary"), 56),
        name="stickbreak",
    )(sb, sb, sb)


def _mem_fold_kernel(m_ref, g_ref, wkv_ref, wq_ref, wo_ref, sk_ref, vo_ref):
    d = m_ref.shape[-1]
    dh = d // XATTN_HEADS
    nm = m_ref.shape[1]
    m = _rms(m_ref[0], g_ref[...]).astype(BF16)
    kv = jnp.dot(m, wkv_ref[...], preferred_element_type=F32).astype(BF16)
    for e in range(XATTN_HEADS):
        hs = slice(e * dh, (e + 1) * dh)
        sk = lax.dot_general(wq_ref[:, hs], kv[:, hs], _NT, preferred_element_type=F32)
        sk_ref[0, :, e * nm:(e + 1) * nm] = (sk * (dh ** -0.5)).astype(sk_ref.dtype)
        vo = jnp.dot(kv[:, d + e * dh:d + (e + 1) * dh], wo_ref[hs, :], preferred_element_type=F32)
        vo_ref[0, e * nm:(e + 1) * nm, :] = vo.astype(vo_ref.dtype)


def _mem_fold(mem, g, wkv, wq, wo):
    b, nm, d = mem.shape
    hn = XATTN_HEADS * nm
    return pl.pallas_call(
        _mem_fold_kernel,
        grid=(b,),
        in_specs=[pl.BlockSpec((1, nm, d), lambda bi: (bi, 0, 0)), _resident((1, d)),
                  _resident(wkv.shape), _resident(wq.shape), _resident(wo.shape)],
        out_specs=[pl.BlockSpec((1, d, hn), lambda bi: (bi, 0, 0)),
                   pl.BlockSpec((1, hn, d), lambda bi: (bi, 0, 0))],
        out_shape=[jax.ShapeDtypeStruct((b, d, hn), BF16), jax.ShapeDtypeStruct((b, hn, d), BF16)],
        compiler_params=_params(("parallel",), 48),
        name="mem_fold",
    )(mem, g, wkv, wq, wo)


def _mix_stage(x, ya, yb, g_ref, wg_ref, wa_ref, wb_ref, wo_ref):
    d = x.shape[-1]
    h = _rms(x, g_ref[...]).astype(BF16)
    gate = _sigmoid(jnp.dot(h, wg_ref[...], preferred_element_type=F32))
    pa = jnp.dot(ya, wa_ref[...], preferred_element_type=F32)
    pb = jnp.dot(yb, wb_ref[...], preferred_element_type=F32)
    merged = gate[:, :d] * pa + gate[:, d:] * pb
    return x + jnp.dot(merged.astype(BF16), wo_ref[...], preferred_element_type=F32)


def _xattn_stage(x, g_ref, sk_ref, vo_ref):
    nm = sk_ref.shape[-1] // XATTN_HEADS
    h = _rms(x, g_ref[...]).astype(BF16)
    s = jnp.dot(h, sk_ref[0], preferred_element_type=F32)
    probs = []
    for e in range(XATTN_HEADS):
        s_e = s[:, e * nm:(e + 1) * nm]
        p = jnp.exp(s_e - jnp.max(s_e, axis=-1, keepdims=True))
        probs.append((p / jnp.sum(p, axis=-1, keepdims=True)).astype(BF16))
    return x + jnp.dot(jnp.concatenate(probs, axis=-1), vo_ref[0], preferred_element_type=F32)


def _ffn_stage(x, g_ref, wi_ref, wo_ref):
    dff = wo_ref.shape[0]
    h = _rms(x, g_ref[...]).astype(BF16)
    gate = jnp.dot(h, wi_ref[:, :dff], preferred_element_type=F32)
    up = jnp.dot(h, wi_ref[:, dff:], preferred_element_type=F32)
    act = (gate * _sigmoid(gate) * up).astype(BF16)
    return x + jnp.dot(act, wo_ref[...], preferred_element_type=F32)


def _post_kernel(x_ref, ya_ref, yb_ref, gm_ref, wg_ref, wa_ref, wb_ref, wout_ref, gx_ref, sk_ref, vo_ref,
                 gf_ref, wi_ref, wo_ref, gfin_ref, o_ref, *, sub, final_norm):
    def sub_block(r):
        rows = slice(r, r + sub)
        x = _mix_stage(x_ref[0, rows, :], ya_ref[0, rows, :], yb_ref[0, rows, :], gm_ref, wg_ref, wa_ref,
                       wb_ref, wout_ref)
        yield
        x = _xattn_stage(x, gx_ref, sk_ref, vo_ref)
        yield
        x = _ffn_stage(x, gf_ref, wi_ref, wo_ref)
        o_ref[0, rows, :] = _rms(x, gfin_ref[...]) if final_norm else x

    blocks = [sub_block(r) for r in range(0, x_ref.shape[1], sub)]
    for step in range(len(blocks) + 2):
        for stage in range(3):
            n = step - stage
            if 0 <= n < len(blocks):
                next(blocks[n], None)


def _post(x3d, ya, yb, gm, wg, wa, wb, wout, gx, sk, vo, gf, wi, wo, gfin, final_norm, tm, sub):
    b, t, d = x3d.shape
    rows = lambda n: pl.BlockSpec((1, tm, n), lambda bi, i: (bi, i, 0))
    per_batch = lambda a: pl.BlockSpec((1,) + a.shape[1:], lambda bi, i: (bi, 0, 0))
    return pl.pallas_call(
        functools.partial(_post_kernel, sub=sub, final_norm=final_norm),
        grid=(b, t // tm),
        in_specs=[rows(d), rows(ya.shape[-1]), rows(yb.shape[-1]), _resident((1, d)), _resident(wg.shape),
                  _resident(wa.shape), _resident(wb.shape), _resident(wout.shape), _resident((1, d)),
                  per_batch(sk), per_batch(vo), _resident((1, d)), _resident(wi.shape), _resident(wo.shape),
                  _resident((1, d))],
        out_specs=rows(d),
        out_shape=jax.ShapeDtypeStruct((b, t, d), F32),
        compiler_params=_params(("parallel", "parallel"), 58),
        name="post",
    )(x3d, ya, yb, gm, wg, wa, wb, wout, gx, sk, vo, gf, wi, wo, gfin)


def _tile(n, want):
    return want if n % want == 0 else n


def kernel(x, mem, g_mix, w_in, lb_table, g_hgrn, w_gate, w_proj_a, w_proj_b, w_out, g_xattn, g_mem,
           w_xq, w_xkv, w_xo, g_ffn, w_ffn_in, w_ffn_out, g_final):
    b, t, d = x.shape
    depth = w_in.shape[0]
    hq = HGRN_HEADS * HGRN_D
    sbw = SB_HEADS * SB_HEAD_DIM
    row = lambda v: v.reshape(1, -1).astype(F32)
    lb_all = jnp.cumsum(jax.nn.softmax(lb_table.astype(F32), axis=0), axis=0)

    col_scale = jnp.ones((4 * hq + 3 * sbw,), F32).at[4 * hq:4 * hq + sbw].set(SB_HEAD_DIM ** -0.5 * LOG2E)

    blk_rows, sub_rows = _tile(t, 1024), _tile(t, 512)
    post_rows, post_sub_rows = _tile(t, 512), _tile(t, 256)

    x2d = x.reshape(b * t, d)
    for l in range(depth):
        hg, sb = _norm_matmul(x2d, row(g_mix[l]), (w_in[l] * col_scale).astype(BF16),
                              ((0, 4 * hq), (4 * hq, 4 * hq + 3 * sbw)), (F32, BF16), blk_rows, sub_rows)
        ya = _hgrn(hg.reshape(b, t, 4 * hq), row(lb_all[l]), row(g_hgrn[l]), blk_rows)
        yb = _stickbreak(sb.reshape(b, t, 3 * sbw), _tile(t, 256))
        score_w, out_w = _mem_fold(mem, row(g_mem[l]), w_xkv[l].astype(BF16), w_xq[l].astype(BF16),
                                   w_xo[l].astype(BF16))
        x2d = _post(x2d.reshape(b, t, d), ya, yb, row(g_mix[l]), w_gate[l].astype(BF16),
                    w_proj_a[l].astype(BF16), w_proj_b[l].astype(BF16), w_out[l].astype(BF16),
                    row(g_xattn[l]), score_w, out_w, row(g_ffn[l]), w_ffn_in[l].astype(BF16),
                    w_ffn_out[l].astype(BF16), row(g_final), l == depth - 1,
                    post_rows, post_sub_rows).reshape(b * t, d)
    return x2d.reshape(b, t, d)
```

```python
import functools

import jax
import jax.numpy as jnp
import numpy as np
from jax import lax
from jax.experimental import pallas as pl
from jax.experimental.pallas import tpu as pltpu

F32 = jnp.float32
BF16 = jnp.bfloat16
EPS = 1e-6

CHUNK = 64
HGRN_HEADS = 4
HGRN_D = 128
SB_HEADS = 8
SB_HEAD_DIM = 64
XATTN_HEADS = 4

V7X_VMEM_BYTES = 64 * 1024 * 1024
LANES = 128

_NT = (((1,), (1,)), ((), ()))
_TN = (((0,), (0,)), ((), ()))


def _rms(x, g):
    return x * lax.rsqrt(jnp.mean(x * x, axis=-1, keepdims=True) + EPS) * g


def _sigmoid(x):
    return 1.0 / (1.0 + jnp.exp(-x))


def _params(sem, vmem_mb):
    return pltpu.CompilerParams(dimension_semantics=sem, vmem_limit_bytes=vmem_mb * 1024 * 1024)


def _resident(shape):
    nd = len(shape)
    return pl.BlockSpec(shape, lambda *_: (0,) * nd, pipeline_mode=pl.Buffered(1))


def _in_proj_kernel(x_ref, g_ref, w_ref, scale_ref, hg_ref, sb_ref, *, sub):
    n_hg = hg_ref.shape[-1]
    w_hg = w_ref[:, :n_hg].astype(BF16)
    w_sb = w_ref[:, n_hg:].astype(BF16)
    for r in range(0, x_ref.shape[0], sub):
        h = _rms(x_ref[r:r + sub, :], g_ref[...]).astype(BF16)
        hg_ref[r:r + sub, :] = jnp.dot(h, w_hg, preferred_element_type=F32) * scale_ref[:, :n_hg]
        sb = jnp.dot(h, w_sb, preferred_element_type=F32) * scale_ref[:, n_hg:]
        sb_ref[r:r + sub, :] = sb.astype(sb_ref.dtype)


def _in_proj(x2d, g, w, col_scale, n_hg, tm, sub):
    m, d = x2d.shape
    n = w.shape[1]
    return pl.pallas_call(
        functools.partial(_in_proj_kernel, sub=sub),
        grid=(m // tm,),
        in_specs=[pl.BlockSpec((tm, d), lambda i: (i, 0)), _resident((1, d)), _resident((d, n)),
                  _resident((1, n))],
        out_specs=[pl.BlockSpec((tm, n_hg), lambda i: (i, 0)), pl.BlockSpec((tm, n - n_hg), lambda i: (i, 0))],
        out_shape=[jax.ShapeDtypeStruct((m, n_hg), F32), jax.ShapeDtypeStruct((m, n - n_hg), BF16)],
        compiler_params=_params(("parallel",), 56),
        name="in_proj",
    )(x2d, g, w, col_scale)


HGRN_LEVELS = (32, 16, 8)
HGRN_DIAG = 8


def _hgrn_masks():
    c = CHUNK
    t = lax.broadcasted_iota(jnp.int32, (c, c), 0)
    s = lax.broadcasted_iota(jnp.int32, (c, c), 1)
    masks = []
    for m in HGRN_LEVELS:
        same = (t // (2 * m)) == (s // (2 * m))
        masks.append(same & ((t % (2 * m)) >= m) & ((s % (2 * m)) < m))
    masks.append(((t // HGRN_DIAG) == (s // HGRN_DIAG)) & (s <= t))
    return masks


HGRN_CUMSUM_TERMS = 2
HGRN_GROUP = 1


def _silu_of_half(hx):
    return hx + hx * jnp.tanh(hx)


def _hgrn_kernel(hg_ref, lb_ref, gn_ref, tril_ref, o_ref, st_ref, *, n_chunks, group):
    c, nh, dh = CHUNK, HGRN_HEADS, HGRN_D
    w = nh * dh

    @pl.when(pl.program_id(1) == 0)
    def _():
        st_ref[...] = jnp.zeros_like(st_ref)

    masks = _hgrn_masks()
    states = [st_ref[h] for h in range(nh)]
    groups = [_hgrn_group(hg_ref, lb_ref, gn_ref, tril_ref, o_ref, states, masks, c0, group)
              for c0 in range(0, n_chunks, group)]
    for step in range(len(groups) + 2):
        for phase in range(3):
            g = step - phase
            if 0 <= g < len(groups):
                next(groups[g], None)
    for h in range(nh):
        st_ref[h] = states[h]


def _hgrn_group(hg_ref, lb_ref, gn_ref, tril_ref, o_ref, states, masks, c0, n):
    c, nh, dh = CHUNK, HGRN_HEADS, HGRN_D
    w = nh * dh

    def wide(col):
        return jnp.concatenate(
            [hg_ref[0, (c0 + ci) * c:(c0 + ci + 1) * c, col * w:(col + 1) * w] for ci in range(n)], axis=1)

    def tiled(row):
        return jnp.concatenate([row] * n, axis=1)

    qp, fp, iv, gp = wide(0), wide(1), wide(2), wide(3)
    lb = tiled(lb_ref[...])
    q = _silu_of_half(qp)
    f = 0.5 * (1.0 + lb) + (0.5 * (1.0 - lb)) * jnp.tanh(fp)
    k = 1.0 - f
    b = jnp.dot(tril_ref[...], _bf16_terms(jnp.log(f) * LOG2E, HGRN_CUMSUM_TERMS, axis=0),
                preferred_element_type=F32)

    units = [(ci, h) for ci in range(n) for h in range(nh)]
    lanes = {u: slice(u[0] * w + u[1] * dh, u[0] * w + (u[1] + 1) * dh) for u in units}
    e_b = jnp.exp2(b)
    e_last = e_b[c - 1:c, :]
    q_in = (q * e_b).astype(BF16)
    k_out = (k * jnp.exp2(b[c - 1:c, :] - b)).astype(BF16)
    v16 = iv.astype(BF16)
    sides = []
    for m in HGRN_LEVELS:
        decay, qk = [], []
        for s0 in range(0, c, 2 * m):
            bp = b[s0 + m:s0 + m + 1, :]
            decay += [bp - b[s0:s0 + m, :], b[s0 + m:s0 + 2 * m, :] - bp]
            qk += [k[s0:s0 + m, :], q[s0 + m:s0 + 2 * m, :]]
        both = (jnp.concatenate(qk, axis=0) * jnp.exp2(jnp.concatenate(decay, axis=0))).astype(BF16)
        sides.append((both, both))
    half = HGRN_DIAG // 2
    dd = jnp.concatenate([b[s0:s0 + HGRN_DIAG, :] - b[s0 + half:s0 + half + 1, :]
                          for s0 in range(0, c, HGRN_DIAG)], axis=0)
    e_dd = jnp.exp2(dd)
    sides.append(((q * e_dd).astype(BF16), (k * (1.0 / e_dd)).astype(BF16)))
    gate = tiled(gn_ref[...]) * _silu_of_half(gp)
    yield

    kv = {u: lax.dot_general(v16[:, lanes[u]], k_out[:, lanes[u]], _TN, preferred_element_type=F32)
          for u in units}
    inter = {}
    for h in range(nh):
        for ci in range(n):
            u = (ci, h)
            inter[u] = lax.dot_general(q_in[:, lanes[u]], states[h].astype(BF16), _NT,
                                       preferred_element_type=F32)
            states[h] = e_last[:, lanes[u]] * states[h] + kv[u]
    parts = {u: [lax.dot_general(a_l[:, lanes[u]], b_l[:, lanes[u]], _NT, preferred_element_type=F32)
                 for a_l, b_l in sides] for u in units}
    yield

    scores = {}
    for u in units:
        s_u = jnp.zeros((c, c), F32)
        for s_l, m_l in zip(parts[u], masks):
            s_u = jnp.where(m_l, s_l, s_u)
        scores[u] = s_u.astype(BF16)
    for ci in range(n):
        outs = []
        for h in range(nh):
            u = (ci, h)
            o = jnp.dot(scores[u], v16[:, lanes[u]], preferred_element_type=F32) + inter[u]
            outs.append(o * lax.rsqrt(jnp.mean(o * o, axis=-1, keepdims=True) + EPS))
        on = jnp.concatenate(outs, axis=-1) * gate[:, ci * w:(ci + 1) * w]
        o_ref[0, (c0 + ci) * c:(c0 + ci + 1) * c, :] = on.astype(o_ref.dtype)


def _hgrn(hg, lb, gn, rows_per_step):
    b, t, w4 = hg.shape
    w = w4 // 4
    tril = np.tril(np.ones((CHUNK, CHUNK), np.float32))
    tril3 = jnp.asarray(np.concatenate([tril] * HGRN_CUMSUM_TERMS, axis=1), BF16)
    return pl.pallas_call(
        functools.partial(_hgrn_kernel, n_chunks=rows_per_step // CHUNK, group=HGRN_GROUP),
        grid=(b, t // rows_per_step),
        in_specs=[pl.BlockSpec((1, rows_per_step, w4), lambda bi, i: (bi, i, 0)),
                  _resident((1, w)), _resident((1, w)), _resident(tril3.shape)],
        out_specs=pl.BlockSpec((1, rows_per_step, w), lambda bi, i: (bi, i, 0)),
        out_shape=jax.ShapeDtypeStruct((b, t, w), BF16),
        scratch_shapes=[pltpu.VMEM((HGRN_HEADS, HGRN_D, HGRN_D), F32)],
        compiler_params=_params(("parallel", "arbitrary"), 48),
        name="hgrn2",
    )(hg, lb, gn, tril3)


LOG2E = 1.4426950408889634
SB_SKIP_LOG2 = 160.0
SB_TERMS = 1
SB_NO_TILE = 1e30
SB_LINEAR_ABOVE = 64.0
SB_SKEW = 1


def _bf16_terms(x, n, axis=-1):
    terms, r = [], x
    for i in range(n):
        t = r.astype(BF16)
        terms.append(t)
        if i + 1 < n:
            r = r - t.astype(F32)
    return terms[0] if n == 1 else jnp.concatenate(terms, axis=axis)


def _sb_kernel(q_ref, k_ref, v_ref, o_ref, acc_ref, carry_ref, *, blk, npair):
    i = pl.program_id(1)
    lane = lax.broadcasted_iota(jnp.int32, (1, LANES), 1)
    first = lane < SB_HEAD_DIM
    t_idx = lax.broadcasted_iota(jnp.int32, (blk, blk), 0)
    s_idx = lax.broadcasted_iota(jnp.int32, (blk, blk), 1)
    after = jnp.concatenate([(t_idx > s_idx).astype(BF16)] * SB_TERMS, axis=0)
    causal = s_idx < t_idx
    nh = 2 * npair

    qs = []
    for p in range(npair):
        q = q_ref[0, :, p * LANES:(p + 1) * LANES]
        qs += [jnp.where(keep, q, jnp.zeros_like(q)) for keep in (first, jnp.logical_not(first))]

    def sweep(tiles, carries):
        ks, vs = [], []
        for j, _, _ in tiles:
            rows = pl.ds(pl.multiple_of(j * blk, blk), blk)
            ks.append([k_ref[0, rows, p * LANES:(p + 1) * LANES] for p in range(npair)])
            vs.append([v_ref[0, rows, p * LANES:(p + 1) * LANES] for p in range(npair)])
        units = [(n, h) for n in range(len(tiles)) for h in range(nh)]
        carries, pvs = list(carries), [None] * nh
        zs, logb, later, total = {}, {}, {}, {}

        def logits(u):
            n, h = u
            z = lax.dot_general(qs[h], ks[n][h // 2], _NT, preferred_element_type=F32)
            zs[u] = jnp.where(causal, z, -SB_NO_TILE) if tiles[n][1] else z

        def suffix_sums(u):
            z = zs.pop(u)
            sp = jnp.where(z > SB_LINEAR_ABOVE, z, jnp.log2(1.0 + jnp.exp2(z)))
            later[u] = jnp.dot(_bf16_terms(sp, SB_TERMS), after, preferred_element_type=F32)
            logb[u] = z - sp
            total[u] = later[u][:, 0:1] + sp[:, 0:1]

        def weigh(u):
            n, h = u
            if tiles[n][2] is not None:
                carries[h] = carries[h] + tiles[n][2]
            loga = logb.pop(u) - later.pop(u)
            a = jnp.exp2(loga if carries[h] is None else loga - carries[h])
            pv = jnp.dot(a.astype(BF16), vs[n][h // 2], preferred_element_type=F32)
            pvs[h] = pv if pvs[h] is None else pvs[h] + pv
            carries[h] = total[u] if carries[h] is None else carries[h] + total[u]

        for step in range(len(units) + 2 * SB_SKEW):
            for stage, fn in enumerate((logits, suffix_sums, weigh)):
                idx = step - stage * SB_SKEW
                if 0 <= idx < len(units):
                    fn(units[idx])
        done = jnp.min(functools.reduce(jnp.minimum, carries)) >= SB_SKIP_LOG2
        return pvs, carries, done

    def merge_heads(accs):
        return jnp.concatenate([jnp.where(first, accs[2 * p], accs[2 * p + 1]) for p in range(npair)],
                               axis=-1).astype(o_ref.dtype)

    no_prev = jnp.where(i == 0, jnp.float32(SB_NO_TILE), jnp.float32(0.0))
    accs, carries, done = sweep([(i, True, None), (jnp.maximum(i - 1, 0), False, no_prev)], [None] * nh)
    o_ref[0] = merge_heads(accs)
    for h in range(nh):
        acc_ref[h] = accs[h]
        carry_ref[h] = carries[h]

    @pl.when(jnp.logical_and(i >= 2, jnp.logical_not(done)))
    def _():
        def body(st):
            n, _ = st
            pvs, new_carries, done = sweep([(i - n, False, None)], [carry_ref[h] for h in range(nh)])
            for h in range(nh):
                acc_ref[h] += pvs[h]
                carry_ref[h] = new_carries[h]
            return n + 1, done.astype(jnp.int32)

        lax.while_loop(lambda st: jnp.logical_and(st[0] <= i, st[1] == 0), body, (jnp.int32(2), jnp.int32(0)))
        o_ref[0] = merge_heads([acc_ref[h] for h in range(nh)])


def _stickbreak(sb, blk):
    b, t, w3 = sb.shape
    w = w3 // 3
    return pl.pallas_call(
        functools.partial(_sb_kernel, blk=blk, npair=w // LANES),
        grid=(b, t // blk),
        in_specs=[pl.BlockSpec((1, blk, w), lambda bi, i: (bi, i, 0)),
                  pl.BlockSpec((1, t, w), lambda bi, i: (bi, 0, 1)),
                  pl.BlockSpec((1, t, w), lambda bi, i: (bi, 0, 2))],
        out_specs=pl.BlockSpec((1, blk, w), lambda bi, i: (bi, i, 0)),
        out_shape=jax.ShapeDtypeStruct((b, t, w), BF16),
        scratch_shapes=[pltpu.VMEM((SB_HEADS, blk, LANES), F32), pltpu.VMEM((SB_HEADS, blk, 1), F32)],
        compiler_params=_params(("parallel", "arbitrary"), 56),
        name="stickbreak",
    )(sb, sb, sb)


def _mem_fold_kernel(m_ref, g_ref, wkv_ref, wq_ref, wo_ref, sk_ref, vo_ref):
    d = m_ref.shape[-1]
    dh = d // XATTN_HEADS
    nm = m_ref.shape[1]
    m = _rms(m_ref[0], g_ref[...]).astype(BF16)
    kv = jnp.dot(m, wkv_ref[...].astype(BF16), preferred_element_type=F32).astype(BF16)
    for e in range(XATTN_HEADS):
        hs = slice(e * dh, (e + 1) * dh)
        sk = lax.dot_general(wq_ref[:, hs].astype(BF16), kv[:, hs], _NT, preferred_element_type=F32)
        sk_ref[0, :, e * nm:(e + 1) * nm] = (sk * (dh ** -0.5)).astype(sk_ref.dtype)
        vo = jnp.dot(kv[:, d + e * dh:d + (e + 1) * dh], wo_ref[hs, :].astype(BF16),
                     preferred_element_type=F32)
        vo_ref[0, e * nm:(e + 1) * nm, :] = vo.astype(vo_ref.dtype)


def _mem_fold(mem, g, wkv, wq, wo):
    b, nm, d = mem.shape
    hn = XATTN_HEADS * nm
    return pl.pallas_call(
        _mem_fold_kernel,
        grid=(b,),
        in_specs=[pl.BlockSpec((1, nm, d), lambda bi: (bi, 0, 0)), _resident((1, d)),
                  _resident(wkv.shape), _resident(wq.shape), _resident(wo.shape)],
        out_specs=[pl.BlockSpec((1, d, hn), lambda bi: (bi, 0, 0)),
                   pl.BlockSpec((1, hn, d), lambda bi: (bi, 0, 0))],
        out_shape=[jax.ShapeDtypeStruct((b, d, hn), BF16), jax.ShapeDtypeStruct((b, hn, d), BF16)],
        compiler_params=_params(("parallel",), 48),
        name="mem_fold",
    )(mem, g, wkv, wq, wo)


def _mix_stage(x, ya, yb, g_ref, wg_ref, wa_ref, wb_ref, wo_ref):
    d = x.shape[-1]
    h = _rms(x, g_ref[...]).astype(BF16)
    gate = _sigmoid(jnp.dot(h, wg_ref[...], preferred_element_type=F32))
    pa = jnp.dot(ya, wa_ref[...], preferred_element_type=F32)
    pb = jnp.dot(yb, wb_ref[...], preferred_element_type=F32)
    merged = gate[:, :d] * pa + gate[:, d:] * pb
    return x + jnp.dot(merged.astype(BF16), wo_ref[...], preferred_element_type=F32)


def _xattn_stage(x, g_ref, sk_ref, vo_ref):
    nm = sk_ref.shape[-1] // XATTN_HEADS
    h = _rms(x, g_ref[...]).astype(BF16)
    s = jnp.dot(h, sk_ref[0], preferred_element_type=F32)
    probs = []
    for e in range(XATTN_HEADS):
        s_e = s[:, e * nm:(e + 1) * nm]
        p = jnp.exp(s_e - jnp.max(s_e, axis=-1, keepdims=True))
        probs.append((p / jnp.sum(p, axis=-1, keepdims=True)).astype(BF16))
    return x + jnp.dot(jnp.concatenate(probs, axis=-1), vo_ref[0], preferred_element_type=F32)


def _ffn_stage(x, g_ref, wi_ref, wo_ref):
    dff = wo_ref.shape[0]
    h = _rms(x, g_ref[...]).astype(BF16)
    gate = jnp.dot(h, wi_ref[:, :dff], preferred_element_type=F32)
    up = jnp.dot(h, wi_ref[:, dff:], preferred_element_type=F32)
    act = (gate * _sigmoid(gate) * up).astype(BF16)
    return x + jnp.dot(act, wo_ref[...], preferred_element_type=F32)


def _post_kernel(x_ref, ya_ref, yb_ref, gm_ref, wg_ref, wa_ref, wb_ref, wout_ref, gx_ref, sk_ref, vo_ref,
                 gf_ref, wi_ref, wo_ref, gfin_ref, o_ref, *, sub, final_norm):
    def sub_block(r):
        rows = slice(r, r + sub)
        x = _mix_stage(x_ref[0, rows, :], ya_ref[0, rows, :], yb_ref[0, rows, :], gm_ref, wg_ref, wa_ref,
                       wb_ref, wout_ref)
        yield
        x = _xattn_stage(x, gx_ref, sk_ref, vo_ref)
        yield
        x = _ffn_stage(x, gf_ref, wi_ref, wo_ref)
        o_ref[0, rows, :] = _rms(x, gfin_ref[...]) if final_norm else x

    blocks = [sub_block(r) for r in range(0, x_ref.shape[1], sub)]
    for step in range(len(blocks) + 2):
        for stage in range(3):
            n = step - stage
            if 0 <= n < len(blocks):
                next(blocks[n], None)


def _post(x3d, ya, yb, gm, wg, wa, wb, wout, gx, sk, vo, gf, wi, wo, gfin, final_norm, tm, sub):
    b, t, d = x3d.shape
    rows = lambda n: pl.BlockSpec((1, tm, n), lambda bi, i: (bi, i, 0))
    per_batch = lambda a: pl.BlockSpec((1,) + a.shape[1:], lambda bi, i: (bi, 0, 0))
    return pl.pallas_call(
        functools.partial(_post_kernel, sub=sub, final_norm=final_norm),
        grid=(b, t // tm),
        in_specs=[rows(d), rows(ya.shape[-1]), rows(yb.shape[-1]), _resident((1, d)), _resident(wg.shape),
                  _resident(wa.shape), _resident(wb.shape), _resident(wout.shape), _resident((1, d)),
                  per_batch(sk), per_batch(vo), _resident((1, d)), _resident(wi.shape), _resident(wo.shape),
                  _resident((1, d))],
        out_specs=rows(d),
        out_shape=jax.ShapeDtypeStruct((b, t, d), F32),
        compiler_params=_params(("parallel", "parallel"), 58),
        name="post",
    )(x3d, ya, yb, gm, wg, wa, wb, wout, gx, sk, vo, gf, wi, wo, gfin)


def _tile(n, want):
    return want if n % want == 0 else n


def kernel(x, mem, g_mix, w_in, lb_table, g_hgrn, w_gate, w_proj_a, w_proj_b, w_out, g_xattn, g_mem,
           w_xq, w_xkv, w_xo, g_ffn, w_ffn_in, w_ffn_out, g_final):
    b, t, d = x.shape
    depth = w_in.shape[0]
    hq = HGRN_HEADS * HGRN_D
    sbw = SB_HEADS * SB_HEAD_DIM
    row = lambda v: v.reshape(1, -1).astype(F32)
    lb_all = jnp.cumsum(jax.nn.softmax(lb_table.astype(F32), axis=0), axis=0)

    col_scale = jnp.concatenate([jnp.full((2 * hq,), 0.5, F32), jnp.ones((hq,), F32), jnp.full((hq,), 0.5, F32),
                                 jnp.full((sbw,), SB_HEAD_DIM ** -0.5 * LOG2E, F32),
                                 jnp.ones((2 * sbw,), F32)]).reshape(1, -1)

    blk_rows, sub_rows = _tile(t, 1024), _tile(t, 512)
    post_rows, post_sub_rows = _tile(t, 512), _tile(t, 256)

    x2d = x.reshape(b * t, d)
    for l in range(depth):
        hg, sb = _in_proj(x2d, row(g_mix[l]), w_in[l], col_scale, 4 * hq, blk_rows, sub_rows)
        ya = _hgrn(hg.reshape(b, t, 4 * hq), row(lb_all[l]), row(g_hgrn[l]), blk_rows)
        yb = _stickbreak(sb.reshape(b, t, 3 * sbw), _tile(t, 256))
        score_w, out_w = _mem_fold(mem, row(g_mem[l]), w_xkv[l], w_xq[l], w_xo[l])
        x2d = _post(x2d.reshape(b, t, d), ya, yb, row(g_mix[l]), w_gate[l].astype(BF16),
                    w_proj_a[l].astype(BF16), w_proj_b[l].astype(BF16), w_out[l].astype(BF16),
                    row(g_xattn[l]), score_w, out_w, row(g_ffn[l]), w_ffn_in[l].astype(BF16),
                    w_ffn_out[l].astype(BF16), row(g_final), l == depth - 1,
                    post_rows, post_sub_rows).reshape(b * t, d)
    return x2d.reshape(b, t, d)
```

```python
import functools

import jax
import jax.numpy as jnp
import numpy as np
from jax import lax
from jax.experimental import pallas as pl
from jax.experimental.pallas import tpu as pltpu

F32 = jnp.float32
BF16 = jnp.bfloat16
EPS = 1e-6

CHUNK = 64
HGRN_HEADS = 4
HGRN_D = 128
SB_HEADS = 8
SB_HEAD_DIM = 64
XATTN_HEADS = 4

V7X_VMEM_BYTES = 64 * 1024 * 1024
LANES = 128

_NT = (((1,), (1,)), ((), ()))
_TN = (((0,), (0,)), ((), ()))


def _rms(x, g):
    return x * lax.rsqrt(jnp.mean(x * x, axis=-1, keepdims=True) + EPS) * g


def _sigmoid(x):
    return 1.0 / (1.0 + jnp.exp(-x))


def _params(sem, vmem_mb):
    return pltpu.CompilerParams(dimension_semantics=sem, vmem_limit_bytes=vmem_mb * 1024 * 1024)


def _resident(shape):
    nd = len(shape)
    return pl.BlockSpec(shape, lambda *_: (0,) * nd, pipeline_mode=pl.Buffered(1))


def _in_proj_kernel(x_ref, g_ref, w_ref, scale_ref, hg_ref, sb_ref, *, sub):
    n_hg = hg_ref.shape[-1]
    w_hg = w_ref[:, :n_hg].astype(BF16)
    w_sb = w_ref[:, n_hg:].astype(BF16)
    for r in range(0, x_ref.shape[0], sub):
        h = _rms(x_ref[r:r + sub, :], g_ref[...]).astype(BF16)
        hg_ref[r:r + sub, :] = jnp.dot(h, w_hg, preferred_element_type=F32)
        sb = jnp.dot(h, w_sb, preferred_element_type=F32) * scale_ref[...]
        sb_ref[r:r + sub, :] = sb.astype(sb_ref.dtype)


def _in_proj(x2d, g, w, sb_scale, n_hg, tm, sub):
    m, d = x2d.shape
    n = w.shape[1]
    return pl.pallas_call(
        functools.partial(_in_proj_kernel, sub=sub),
        grid=(m // tm,),
        in_specs=[pl.BlockSpec((tm, d), lambda i: (i, 0)), _resident((1, d)), _resident((d, n)),
                  _resident((1, n - n_hg))],
        out_specs=[pl.BlockSpec((tm, n_hg), lambda i: (i, 0)), pl.BlockSpec((tm, n - n_hg), lambda i: (i, 0))],
        out_shape=[jax.ShapeDtypeStruct((m, n_hg), F32), jax.ShapeDtypeStruct((m, n - n_hg), BF16)],
        compiler_params=_params(("parallel",), 56),
        name="in_proj",
    )(x2d, g, w, sb_scale)


HGRN_LEVELS = (32, 16, 8)
HGRN_DIAG = 8


def _hgrn_masks():
    c = CHUNK
    t = lax.broadcasted_iota(jnp.int32, (c, c), 0)
    s = lax.broadcasted_iota(jnp.int32, (c, c), 1)
    masks = []
    for m in HGRN_LEVELS:
        same = (t // (2 * m)) == (s // (2 * m))
        masks.append(same & ((t % (2 * m)) >= m) & ((s % (2 * m)) < m))
    masks.append(((t // HGRN_DIAG) == (s // HGRN_DIAG)) & (s <= t))
    return masks


HGRN_CUMSUM_TERMS = 2
HGRN_GROUP = 1


def _silu(x):
    hx = 0.5 * x
    return hx + hx * jnp.tanh(hx)


def _hgrn_kernel(hg_ref, lb_ref, gn_ref, tril_ref, o_ref, st_ref, *, n_chunks, group):
    c, nh, dh = CHUNK, HGRN_HEADS, HGRN_D
    w = nh * dh

    @pl.when(pl.program_id(1) == 0)
    def _():
        st_ref[...] = jnp.zeros_like(st_ref)

    masks = _hgrn_masks()
    states = [st_ref[h] for h in range(nh)]
    groups = [_hgrn_group(hg_ref, lb_ref, gn_ref, tril_ref, o_ref, states, masks, c0, group)
              for c0 in range(0, n_chunks, group)]
    for step in range(len(groups) + 2):
        for phase in range(3):
            g = step - phase
            if 0 <= g < len(groups):
                next(groups[g], None)
    for h in range(nh):
        st_ref[h] = states[h]


def _hgrn_group(hg_ref, lb_ref, gn_ref, tril_ref, o_ref, states, masks, c0, n):
    c, nh, dh = CHUNK, HGRN_HEADS, HGRN_D
    w = nh * dh

    def wide(col):
        return jnp.concatenate(
            [hg_ref[0, (c0 + ci) * c:(c0 + ci + 1) * c, col * w:(col + 1) * w] for ci in range(n)], axis=1)

    def tiled(row):
        return jnp.concatenate([row] * n, axis=1)

    qp, fp, iv, gp = wide(0), wide(1), wide(2), wide(3)
    lb = tiled(lb_ref[...])
    q = _silu(qp)
    f = 0.5 * (1.0 + lb) + (0.5 * (1.0 - lb)) * jnp.tanh(0.5 * fp)
    k = 1.0 - f
    b = jnp.dot(tril_ref[...], _bf16_terms(jnp.log(f) * LOG2E, HGRN_CUMSUM_TERMS, axis=0),
                preferred_element_type=F32)

    units = [(ci, h) for ci in range(n) for h in range(nh)]
    lanes = {u: slice(u[0] * w + u[1] * dh, u[0] * w + (u[1] + 1) * dh) for u in units}
    e_b = jnp.exp2(b)
    e_last = e_b[c - 1:c, :]
    q_in = (q * e_b).astype(BF16)
    k_out = (k * jnp.exp2(b[c - 1:c, :] - b)).astype(BF16)
    v16 = iv.astype(BF16)
    sides = []
    for m in HGRN_LEVELS:
        decay, qk = [], []
        for s0 in range(0, c, 2 * m):
            bp = b[s0 + m:s0 + m + 1, :]
            decay += [bp - b[s0:s0 + m, :], b[s0 + m:s0 + 2 * m, :] - bp]
            qk += [k[s0:s0 + m, :], q[s0 + m:s0 + 2 * m, :]]
        both = (jnp.concatenate(qk, axis=0) * jnp.exp2(jnp.concatenate(decay, axis=0))).astype(BF16)
        sides.append((both, both))
    half = HGRN_DIAG // 2
    dd = jnp.concatenate([b[s0:s0 + HGRN_DIAG, :] - b[s0 + half:s0 + half + 1, :]
                          for s0 in range(0, c, HGRN_DIAG)], axis=0)
    sides.append(((q * jnp.exp2(dd)).astype(BF16), (k * jnp.exp2(-dd)).astype(BF16)))
    gate = tiled(gn_ref[...]) * _silu(gp)
    yield

    kv = {u: lax.dot_general(v16[:, lanes[u]], k_out[:, lanes[u]], _TN, preferred_element_type=F32)
          for u in units}
    inter = {}
    for h in range(nh):
        for ci in range(n):
            u = (ci, h)
            inter[u] = lax.dot_general(q_in[:, lanes[u]], states[h].astype(BF16), _NT,
                                       preferred_element_type=F32)
            states[h] = e_last[:, lanes[u]] * states[h] + kv[u]
    parts = {u: [lax.dot_general(a_l[:, lanes[u]], b_l[:, lanes[u]], _NT, preferred_element_type=F32)
                 for a_l, b_l in sides] for u in units}
    yield

    scores = {}
    for u in units:
        s_u = jnp.zeros((c, c), F32)
        for s_l, m_l in zip(parts[u], masks):
            s_u = jnp.where(m_l, s_l, s_u)
        scores[u] = s_u.astype(BF16)
    for ci in range(n):
        outs = []
        for h in range(nh):
            u = (ci, h)
            o = jnp.dot(scores[u], v16[:, lanes[u]], preferred_element_type=F32) + inter[u]
            outs.append(o * lax.rsqrt(jnp.mean(o * o, axis=-1, keepdims=True) + EPS))
        on = jnp.concatenate(outs, axis=-1) * gate[:, ci * w:(ci + 1) * w]
        o_ref[0, (c0 + ci) * c:(c0 + ci + 1) * c, :] = on.astype(o_ref.dtype)


def _hgrn(hg, lb, gn, rows_per_step):
    b, t, w4 = hg.shape
    w = w4 // 4
    tril = np.tril(np.ones((CHUNK, CHUNK), np.float32))
    tril3 = jnp.asarray(np.concatenate([tril] * HGRN_CUMSUM_TERMS, axis=1), BF16)
    return pl.pallas_call(
        functools.partial(_hgrn_kernel, n_chunks=rows_per_step // CHUNK, group=HGRN_GROUP),
        grid=(b, t // rows_per_step),
        in_specs=[pl.BlockSpec((1, rows_per_step, w4), lambda bi, i: (bi, i, 0)),
                  _resident((1, w)), _resident((1, w)), _resident(tril3.shape)],
        out_specs=pl.BlockSpec((1, rows_per_step, w), lambda bi, i: (bi, i, 0)),
        out_shape=jax.ShapeDtypeStruct((b, t, w), BF16),
        scratch_shapes=[pltpu.VMEM((HGRN_HEADS, HGRN_D, HGRN_D), F32)],
        compiler_params=_params(("parallel", "arbitrary"), 48),
        name="hgrn2",
    )(hg, lb, gn, tril3)


LOG2E = 1.4426950408889634
SB_SKIP_LOG2 = 160.0
SB_TERMS = 1
SB_NO_TILE = 1e30
SB_LINEAR_ABOVE = 64.0
SB_SKEW = 1


def _bf16_terms(x, n, axis=-1):
    terms, r = [], x
    for i in range(n):
        t = r.astype(BF16)
        terms.append(t)
        if i + 1 < n:
            r = r - t.astype(F32)
    return terms[0] if n == 1 else jnp.concatenate(terms, axis=axis)


def _skewed(units, stages):
    for step in range(len(units) + (len(stages) - 1) * SB_SKEW):
        for k, stage in enumerate(stages):
            idx = step - k * SB_SKEW
            if 0 <= idx < len(units):
                stage(units[idx])


def _sb_kernel(q_ref, k_ref, v_ref, o_ref, acc_ref, carry_ref, todo_ref, *, blk, npair):
    i = pl.program_id(1)
    half = blk // 2
    nh = 2 * npair
    lane = lax.broadcasted_iota(jnp.int32, (1, LANES), 1)
    first = lane < SB_HEAD_DIM
    t_idx = lax.broadcasted_iota(jnp.int32, (blk, blk), 0)
    s_idx = lax.broadcasted_iota(jnp.int32, (blk, blk), 1)
    causal = s_idx < t_idx
    after = jnp.concatenate([(t_idx > s_idx).astype(BF16)] * SB_TERMS, axis=0)
    after_half = jnp.concatenate([(t_idx > s_idx)[:half, :half].astype(BF16)] * SB_TERMS, axis=0)

    def lanes(p):
        return slice(p * LANES, (p + 1) * LANES)

    def head_queries(rows):
        out = []
        for p in range(npair):
            q = q_ref[0, rows, lanes(p)]
            out += [jnp.where(keep, q, jnp.zeros_like(q)) for keep in (first, jnp.logical_not(first))]
        return out

    def softplus2(z):
        return jnp.where(z > SB_LINEAR_ABOVE, z, jnp.log2(1.0 + jnp.exp2(z)))

    def merge_heads(accs):
        return jnp.concatenate([jnp.where(first, accs[2 * p], accs[2 * p + 1]) for p in range(npair)], axis=-1)

    def windowed():
        units = [(hf, p) for hf in range(2) for p in range(npair)]
        qh, kw, vw = {}, {}, {}
        for hf in range(2):
            start = pl.multiple_of((i - 1) * blk + hf * half, half)
            qs = head_queries(slice(hf * half, (hf + 1) * half))
            for p in range(npair):
                qh[hf, p] = jnp.concatenate(qs[2 * p:2 * p + 2], axis=0)
                kw[hf, p] = k_ref[0, pl.ds(start, blk + half), lanes(p)]
                vw[hf, p] = v_ref[0, pl.ds(start, blk + half), lanes(p)]
        own = jnp.concatenate([causal[:half, :half]] * 2, axis=0)
        zs, logb, later, total, pvs = {}, {}, {}, {}, {}

        def logits(u):
            z = lax.dot_general(qh[u], kw[u], _NT, preferred_element_type=F32)
            zs[u] = jnp.concatenate([z[:, :blk], jnp.where(own, z[:, blk:], -SB_NO_TILE)], axis=1)

        def suffix_sums(u):
            z = zs.pop(u)
            sp = softplus2(z)
            sp_old, sp_new = sp[:, :blk], sp[:, blk:]
            later_new = jnp.dot(_bf16_terms(sp_new, SB_TERMS), after_half, preferred_element_type=F32)
            total_new = later_new[:, 0:1] + sp_new[:, 0:1]
            later_old = jnp.dot(_bf16_terms(sp_old, SB_TERMS), after, preferred_element_type=F32)
            later[u] = (later_old, later_new, total_new)
            logb[u] = z - sp
            total[u] = later_old[:, 0:1] + sp_old[:, 0:1] + total_new

        def weigh(u):
            later_old, later_new, total_new = later.pop(u)
            lb = logb.pop(u)
            a = jnp.exp2(jnp.concatenate([lb[:, :blk] - later_old - total_new, lb[:, blk:] - later_new], axis=1))
            pv = jnp.dot(a.astype(BF16), vw[u], preferred_element_type=F32)
            pvs[u] = jnp.where(first, pv[:half], pv[half:])

        _skewed(units, (logits, suffix_sums, weigh))
        o_ref[0] = jnp.concatenate([jnp.concatenate([pvs[hf, p] for p in range(npair)], axis=-1)
                                    for hf in range(2)], axis=0).astype(o_ref.dtype)
        low = [jnp.min(functools.reduce(jnp.minimum, [total[hf, p] for p in range(npair)])) for hf in range(2)]
        return jnp.logical_and(jnp.logical_or(low[0] >= SB_SKIP_LOG2, i == 1), low[1] >= SB_SKIP_LOG2)

    def sweep(qs, tiles, carries):
        ks, vs = [], []
        for j, _, _ in tiles:
            rows = pl.ds(pl.multiple_of(j * blk, blk), blk)
            ks.append([k_ref[0, rows, lanes(p)] for p in range(npair)])
            vs.append([v_ref[0, rows, lanes(p)] for p in range(npair)])
        units = [(n, h) for n in range(len(tiles)) for h in range(nh)]
        carries, pvs = list(carries), [None] * nh
        zs, logb, later, total = {}, {}, {}, {}

        def logits(u):
            n, h = u
            z = lax.dot_general(qs[h], ks[n][h // 2], _NT, preferred_element_type=F32)
            zs[u] = jnp.where(causal, z, -SB_NO_TILE) if tiles[n][1] else z

        def suffix_sums(u):
            z = zs.pop(u)
            sp = softplus2(z)
            later[u] = jnp.dot(_bf16_terms(sp, SB_TERMS), after, preferred_element_type=F32)
            logb[u] = z - sp
            total[u] = later[u][:, 0:1] + sp[:, 0:1]

        def weigh(u):
            n, h = u
            if tiles[n][2] is not None:
                carries[h] = carries[h] + tiles[n][2]
            loga = logb.pop(u) - later.pop(u)
            a = jnp.exp2(loga if carries[h] is None else loga - carries[h])
            pv = jnp.dot(a.astype(BF16), vs[n][h // 2], preferred_element_type=F32)
            pvs[h] = pv if pvs[h] is None else pvs[h] + pv
            carries[h] = total[u] if carries[h] is None else carries[h] + total[u]

        _skewed(units, (logits, suffix_sums, weigh))
        done = jnp.min(functools.reduce(jnp.minimum, carries)) >= SB_SKIP_LOG2
        return pvs, carries, done

    def general():
        qs = head_queries(slice(None))
        no_prev = jnp.where(i == 0, jnp.float32(SB_NO_TILE), jnp.float32(0.0))
        accs, carries, done = sweep(qs, [(i, True, None), (jnp.maximum(i - 1, 0), False, no_prev)], [None] * nh)
        o_ref[0] = merge_heads(accs).astype(o_ref.dtype)
        for h in range(nh):
            acc_ref[h] = accs[h]
            carry_ref[h] = carries[h]

        @pl.when(jnp.logical_and(i >= 2, jnp.logical_not(done)))
        def _():
            def body(st):
                n, _ = st
                pvs, new_carries, done = sweep(qs, [(i - n, False, None)], [carry_ref[h] for h in range(nh)])
                for h in range(nh):
                    acc_ref[h] += pvs[h]
                    carry_ref[h] = new_carries[h]
                return n + 1, done.astype(jnp.int32)

            lax.while_loop(lambda st: jnp.logical_and(st[0] <= i, st[1] == 0), body, (jnp.int32(2), jnp.int32(0)))
            o_ref[0] = merge_heads([acc_ref[h] for h in range(nh)]).astype(o_ref.dtype)

    todo_ref[0] = 1

    @pl.when(i >= 1)
    def _():
        todo_ref[0] = jnp.where(windowed(), 0, 1)

    pl.when(todo_ref[0] == 1)(general)


def _stickbreak(sb, blk):
    b, t, w3 = sb.shape
    w = w3 // 3
    return pl.pallas_call(
        functools.partial(_sb_kernel, blk=blk, npair=w // LANES),
        grid=(b, t // blk),
        in_specs=[pl.BlockSpec((1, blk, w), lambda bi, i: (bi, i, 0)),
                  pl.BlockSpec((1, t, w), lambda bi, i: (bi, 0, 1)),
                  pl.BlockSpec((1, t, w), lambda bi, i: (bi, 0, 2))],
        out_specs=pl.BlockSpec((1, blk, w), lambda bi, i: (bi, i, 0)),
        out_shape=jax.ShapeDtypeStruct((b, t, w), BF16),
        scratch_shapes=[pltpu.VMEM((SB_HEADS, blk, LANES), F32), pltpu.VMEM((SB_HEADS, blk, 1), F32),
                        pltpu.SMEM((1,), jnp.int32)],
        compiler_params=_params(("parallel", "arbitrary"), 56),
        name="stickbreak",
    )(sb, sb, sb)


def _mem_fold_kernel(m_ref, g_ref, wkv_ref, wq_ref, wo_ref, sk_ref, vo_ref):
    d = m_ref.shape[-1]
    dh = d // XATTN_HEADS
    nm = m_ref.shape[1]
    m = _rms(m_ref[0], g_ref[...]).astype(BF16)
    kv = jnp.dot(m, wkv_ref[...].astype(BF16), preferred_element_type=F32).astype(BF16)
    for e in range(XATTN_HEADS):
        hs = slice(e * dh, (e + 1) * dh)
        sk = lax.dot_general(wq_ref[:, hs].astype(BF16), kv[:, hs], _NT, preferred_element_type=F32)
        sk_ref[0, :, e * nm:(e + 1) * nm] = (sk * (dh ** -0.5)).astype(sk_ref.dtype)
        vo = jnp.dot(kv[:, d + e * dh:d + (e + 1) * dh], wo_ref[hs, :].astype(BF16),
                     preferred_element_type=F32)
        vo_ref[0, e * nm:(e + 1) * nm, :] = vo.astype(vo_ref.dtype)


def _mem_fold(mem, g, wkv, wq, wo):
    b, nm, d = mem.shape
    hn = XATTN_HEADS * nm
    return pl.pallas_call(
        _mem_fold_kernel,
        grid=(b,),
        in_specs=[pl.BlockSpec((1, nm, d), lambda bi: (bi, 0, 0)), _resident((1, d)),
                  _resident(wkv.shape), _resident(wq.shape), _resident(wo.shape)],
        out_specs=[pl.BlockSpec((1, d, hn), lambda bi: (bi, 0, 0)),
                   pl.BlockSpec((1, hn, d), lambda bi: (bi, 0, 0))],
        out_shape=[jax.ShapeDtypeStruct((b, d, hn), BF16), jax.ShapeDtypeStruct((b, hn, d), BF16)],
        compiler_params=_params(("parallel",), 48),
        name="mem_fold",
    )(mem, g, wkv, wq, wo)


def _mix_stage(x, ya, yb, g_ref, wg_ref, wa_ref, wb_ref, wo_ref):
    d = x.shape[-1]
    h = _rms(x, g_ref[...]).astype(BF16)
    gate = _sigmoid(jnp.dot(h, wg_ref[...], preferred_element_type=F32))
    pa = jnp.dot(ya, wa_ref[...], preferred_element_type=F32)
    pb = jnp.dot(yb, wb_ref[...], preferred_element_type=F32)
    merged = gate[:, :d] * pa + gate[:, d:] * pb
    return x + jnp.dot(merged.astype(BF16), wo_ref[...], preferred_element_type=F32)


def _xattn_stage(x, g_ref, sk_ref, vo_ref):
    nm = sk_ref.shape[-1] // XATTN_HEADS
    h = _rms(x, g_ref[...]).astype(BF16)
    s = jnp.dot(h, sk_ref[0], preferred_element_type=F32)
    probs = []
    for e in range(XATTN_HEADS):
        s_e = s[:, e * nm:(e + 1) * nm]
        p = jnp.exp(s_e - jnp.max(s_e, axis=-1, keepdims=True))
        probs.append((p / jnp.sum(p, axis=-1, keepdims=True)).astype(BF16))
    return x + jnp.dot(jnp.concatenate(probs, axis=-1), vo_ref[0], preferred_element_type=F32)


def _ffn_stage(x, g_ref, wi_ref, wo_ref):
    dff = wo_ref.shape[0]
    h = _rms(x, g_ref[...]).astype(BF16)
    gate = jnp.dot(h, wi_ref[:, :dff], preferred_element_type=F32)
    up = jnp.dot(h, wi_ref[:, dff:], preferred_element_type=F32)
    act = (gate * _sigmoid(gate) * up).astype(BF16)
    return x + jnp.dot(act, wo_ref[...], preferred_element_type=F32)


def _post_kernel(x_ref, ya_ref, yb_ref, gm_ref, wg_ref, wa_ref, wb_ref, wout_ref, gx_ref, sk_ref, vo_ref,
                 gf_ref, wi_ref, wo_ref, gfin_ref, o_ref, *, sub, final_norm):
    def sub_block(r):
        rows = slice(r, r + sub)
        x = _mix_stage(x_ref[0, rows, :], ya_ref[0, rows, :], yb_ref[0, rows, :], gm_ref, wg_ref, wa_ref,
                       wb_ref, wout_ref)
        yield
        x = _xattn_stage(x, gx_ref, sk_ref, vo_ref)
        yield
        x = _ffn_stage(x, gf_ref, wi_ref, wo_ref)
        o_ref[0, rows, :] = _rms(x, gfin_ref[...]) if final_norm else x

    blocks = [sub_block(r) for r in range(0, x_ref.shape[1], sub)]
    for step in range(len(blocks) + 2):
        for stage in range(3):
            n = step - stage
            if 0 <= n < len(blocks):
                next(blocks[n], None)


def _post(x3d, ya, yb, gm, wg, wa, wb, wout, gx, sk, vo, gf, wi, wo, gfin, final_norm, tm, sub):
    b, t, d = x3d.shape
    rows = lambda n: pl.BlockSpec((1, tm, n), lambda bi, i: (bi, i, 0))
    per_batch = lambda a: pl.BlockSpec((1,) + a.shape[1:], lambda bi, i: (bi, 0, 0))
    return pl.pallas_call(
        functools.partial(_post_kernel, sub=sub, final_norm=final_norm),
        grid=(b, t // tm),
        in_specs=[rows(d), rows(ya.shape[-1]), rows(yb.shape[-1]), _resident((1, d)), _resident(wg.shape),
                  _resident(wa.shape), _resident(wb.shape), _resident(wout.shape), _resident((1, d)),
                  per_batch(sk), per_batch(vo), _resident((1, d)), _resident(wi.shape), _resident(wo.shape),
                  _resident((1, d))],
        out_specs=rows(d),
        out_shape=jax.ShapeDtypeStruct((b, t, d), F32),
        compiler_params=_params(("parallel", "parallel"), 58),
        name="post",
    )(x3d, ya, yb, gm, wg, wa, wb, wout, gx, sk, vo, gf, wi, wo, gfin)


def _tile(n, want):
    return want if n % want == 0 else n


def kernel(x, mem, g_mix, w_in, lb_table, g_hgrn, w_gate, w_proj_a, w_proj_b, w_out, g_xattn, g_mem,
           w_xq, w_xkv, w_xo, g_ffn, w_ffn_in, w_ffn_out, g_final):
    b, t, d = x.shape
    depth = w_in.shape[0]
    hq = HGRN_HEADS * HGRN_D
    sbw = SB_HEADS * SB_HEAD_DIM
    row = lambda v: v.reshape(1, -1).astype(F32)
    lb_all = jnp.cumsum(jax.nn.softmax(lb_table.astype(F32), axis=0), axis=0)

    sb_scale = jnp.ones((1, 3 * sbw), F32).at[:, :sbw].set(SB_HEAD_DIM ** -0.5 * LOG2E)

    blk_rows, sub_rows = _tile(t, 1024), _tile(t, 512)
    post_rows, post_sub_rows = _tile(t, 512), _tile(t, 256)

    x2d = x.reshape(b * t, d)
    for l in range(depth):
        hg, sb = _in_proj(x2d, row(g_mix[l]), w_in[l], sb_scale, 4 * hq, blk_rows, sub_rows)
        ya = _hgrn(hg.reshape(b, t, 4 * hq), row(lb_all[l]), row(g_hgrn[l]), blk_rows)
        yb = _stickbreak(sb.reshape(b, t, 3 * sbw), _tile(t, 256))
        score_w, out_w = _mem_fold(mem, row(g_mem[l]), w_xkv[l], w_xq[l], w_xo[l])
        x2d = _post(x2d.reshape(b, t, d), ya, yb, row(g_mix[l]), w_gate[l].astype(BF16),
                    w_proj_a[l].astype(BF16), w_proj_b[l].astype(BF16), w_out[l].astype(BF16),
                    row(g_xattn[l]), score_w, out_w, row(g_ffn[l]), w_ffn_in[l].astype(BF16),
                    w_ffn_out[l].astype(BF16), row(g_final), l == depth - 1,
                    post_rows, post_sub_rows).reshape(b * t, d)
    return x2d.reshape(b, t, d)
```

```python
import functools

import jax
import jax.numpy as jnp
import numpy as np
from jax import lax
from jax.experimental import pallas as pl
from jax.experimental.pallas import tpu as pltpu

F32 = jnp.float32
BF16 = jnp.bfloat16
EPS = 1e-6

CHUNK = 64
HGRN_HEADS = 4
HGRN_D = 128
SB_HEADS = 8
SB_HEAD_DIM = 64
XATTN_HEADS = 4

LANES = 128
SB_BLOCK = 256

VMEM_LIMIT_MB = dict(in_proj=56, hgrn2=48, stickbreak=56, mem_fold=48, post=58)

_NT = (((1,), (1,)), ((), ()))
_TN = (((0,), (0,)), ((), ()))


def _rms(x, g):
    return x * lax.rsqrt(jnp.mean(x * x, axis=-1, keepdims=True) + EPS) * g


def _sigmoid(x):
    return 1.0 / (1.0 + jnp.exp(-x))


def _params(name, sem):
    return pltpu.CompilerParams(dimension_semantics=sem, vmem_limit_bytes=VMEM_LIMIT_MB[name] * 1024 * 1024)


def _resident(shape):
    nd = len(shape)
    return pl.BlockSpec(shape, lambda *_: (0,) * nd, pipeline_mode=pl.Buffered(1))


def _in_proj_kernel(x_ref, g_ref, w_ref, scale_ref, hg_ref, sb_ref, *, sub):
    n_hg = hg_ref.shape[-1]
    w_hg = w_ref[:, :n_hg].astype(BF16)
    w_sb = w_ref[:, n_hg:].astype(BF16)
    for r in range(0, x_ref.shape[0], sub):
        h = _rms(x_ref[r:r + sub, :], g_ref[...]).astype(BF16)
        hg_ref[r:r + sub, :] = jnp.dot(h, w_hg, preferred_element_type=F32)
        sb = jnp.dot(h, w_sb, preferred_element_type=F32) * scale_ref[...]
        sb_ref[r:r + sub, :] = sb.astype(sb_ref.dtype)


def _in_proj(x2d, g, w, sb_scale, n_hg, tm, sub):
    m, d = x2d.shape
    n = w.shape[1]
    return pl.pallas_call(
        functools.partial(_in_proj_kernel, sub=sub),
        grid=(m // tm,),
        in_specs=[pl.BlockSpec((tm, d), lambda i: (i, 0)), _resident((1, d)), _resident((d, n)),
                  _resident((1, n - n_hg))],
        out_specs=[pl.BlockSpec((tm, n_hg), lambda i: (i, 0)), pl.BlockSpec((tm, n - n_hg), lambda i: (i, 0))],
        out_shape=[jax.ShapeDtypeStruct((m, n_hg), F32), jax.ShapeDtypeStruct((m, n - n_hg), BF16)],
        compiler_params=_params("in_proj", ("parallel",)),
        name="in_proj",
    )(x2d, g, w, sb_scale)


HGRN_LEVELS = (32, 16, 8)
HGRN_DIAG = 8


def _hgrn_masks():
    c = CHUNK
    t = lax.broadcasted_iota(jnp.int32, (c, c), 0)
    s = lax.broadcasted_iota(jnp.int32, (c, c), 1)
    masks = []
    for m in HGRN_LEVELS:
        same = (t // (2 * m)) == (s // (2 * m))
        masks.append(same & ((t % (2 * m)) >= m) & ((s % (2 * m)) < m))
    masks.append(((t // HGRN_DIAG) == (s // HGRN_DIAG)) & (s <= t))
    return masks


HGRN_CUMSUM_TERMS = 2
HGRN_GROUP = 1


def _silu(x):
    hx = 0.5 * x
    return hx + hx * jnp.tanh(hx)


def _hgrn_kernel(hg_ref, lb_ref, gn_ref, tril_ref, o_ref, st_ref, *, n_chunks, group):
    c, nh, dh = CHUNK, HGRN_HEADS, HGRN_D
    w = nh * dh

    @pl.when(pl.program_id(1) == 0)
    def _():
        st_ref[...] = jnp.zeros_like(st_ref)

    masks = _hgrn_masks()
    states = [st_ref[h] for h in range(nh)]
    groups = [_hgrn_group(hg_ref, lb_ref, gn_ref, tril_ref, o_ref, states, masks, c0, group)
              for c0 in range(0, n_chunks, group)]
    for step in range(len(groups) + 2):
        for phase in range(3):
            g = step - phase
            if 0 <= g < len(groups):
                next(groups[g], None)
    for h in range(nh):
        st_ref[h] = states[h]


def _hgrn_group(hg_ref, lb_ref, gn_ref, tril_ref, o_ref, states, masks, c0, n):
    c, nh, dh = CHUNK, HGRN_HEADS, HGRN_D
    w = nh * dh

    def wide(col):
        return jnp.concatenate(
            [hg_ref[0, (c0 + ci) * c:(c0 + ci + 1) * c, col * w:(col + 1) * w] for ci in range(n)], axis=1)

    def tiled(row):
        return jnp.concatenate([row] * n, axis=1)

    qp, fp, iv, gp = wide(0), wide(1), wide(2), wide(3)
    lb = tiled(lb_ref[...])
    q = _silu(qp)
    f = 0.5 * (1.0 + lb) + (0.5 * (1.0 - lb)) * jnp.tanh(0.5 * fp)
    k = 1.0 - f
    b = jnp.dot(tril_ref[...], _bf16_terms(jnp.log(f) * LOG2E, HGRN_CUMSUM_TERMS, axis=0),
                preferred_element_type=F32)

    units = [(ci, h) for ci in range(n) for h in range(nh)]
    lanes = {u: slice(u[0] * w + u[1] * dh, u[0] * w + (u[1] + 1) * dh) for u in units}
    e_b = jnp.exp2(b)
    e_last = e_b[c - 1:c, :]
    q_in = (q * e_b).astype(BF16)
    k_out = (k * jnp.exp2(b[c - 1:c, :] - b)).astype(BF16)
    v16 = iv.astype(BF16)
    sides = []
    for m in HGRN_LEVELS:
        decay, qk = [], []
        for s0 in range(0, c, 2 * m):
            bp = b[s0 + m:s0 + m + 1, :]
            decay += [bp - b[s0:s0 + m, :], b[s0 + m:s0 + 2 * m, :] - bp]
            qk += [k[s0:s0 + m, :], q[s0 + m:s0 + 2 * m, :]]
        both = (jnp.concatenate(qk, axis=0) * jnp.exp2(jnp.concatenate(decay, axis=0))).astype(BF16)
        sides.append((both, both))
    half = HGRN_DIAG // 2
    dd = jnp.concatenate([b[s0:s0 + HGRN_DIAG, :] - b[s0 + half:s0 + half + 1, :]
                          for s0 in range(0, c, HGRN_DIAG)], axis=0)
    sides.append(((q * jnp.exp2(dd)).astype(BF16), (k * jnp.exp2(-dd)).astype(BF16)))
    gate = tiled(gn_ref[...]) * _silu(gp)
    yield

    kv = {u: lax.dot_general(v16[:, lanes[u]], k_out[:, lanes[u]], _TN, preferred_element_type=F32)
          for u in units}
    inter = {}
    for h in range(nh):
        for ci in range(n):
            u = (ci, h)
            inter[u] = lax.dot_general(q_in[:, lanes[u]], states[h].astype(BF16), _NT,
                                       preferred_element_type=F32)
            states[h] = e_last[:, lanes[u]] * states[h] + kv[u]
    parts = {u: [lax.dot_general(a_l[:, lanes[u]], b_l[:, lanes[u]], _NT, preferred_element_type=F32)
                 for a_l, b_l in sides] for u in units}
    yield

    scores = {}
    for u in units:
        s_u = jnp.zeros((c, c), F32)
        for s_l, m_l in zip(parts[u], masks):
            s_u = jnp.where(m_l, s_l, s_u)
        scores[u] = s_u.astype(BF16)
    for ci in range(n):
        outs = []
        for h in range(nh):
            u = (ci, h)
            o = jnp.dot(scores[u], v16[:, lanes[u]], preferred_element_type=F32) + inter[u]
            outs.append(o * lax.rsqrt(jnp.mean(o * o, axis=-1, keepdims=True) + EPS))
        on = jnp.concatenate(outs, axis=-1) * gate[:, ci * w:(ci + 1) * w]
        o_ref[0, (c0 + ci) * c:(c0 + ci + 1) * c, :] = on.astype(o_ref.dtype)


def _hgrn(hg, lb, gn, rows_per_step):
    b, t, w4 = hg.shape
    w = w4 // 4
    tril = np.tril(np.ones((CHUNK, CHUNK), np.float32))
    tril3 = jnp.asarray(np.concatenate([tril] * HGRN_CUMSUM_TERMS, axis=1), BF16)
    return pl.pallas_call(
        functools.partial(_hgrn_kernel, n_chunks=rows_per_step // CHUNK, group=HGRN_GROUP),
        grid=(b, t // rows_per_step),
        in_specs=[pl.BlockSpec((1, rows_per_step, w4), lambda bi, i: (bi, i, 0)),
                  _resident((1, w)), _resident((1, w)), _resident(tril3.shape)],
        out_specs=pl.BlockSpec((1, rows_per_step, w), lambda bi, i: (bi, i, 0)),
        out_shape=jax.ShapeDtypeStruct((b, t, w), BF16),
        scratch_shapes=[pltpu.VMEM((HGRN_HEADS, HGRN_D, HGRN_D), F32)],
        compiler_params=_params("hgrn2", ("parallel", "arbitrary")),
        name="hgrn2",
    )(hg, lb, gn, tril3)


LOG2E = 1.4426950408889634
SB_SKIP_LOG2 = 160.0
SB_TERMS = 1
SB_NO_TILE = 1e30
SB_LINEAR_ABOVE = 64.0
SB_SKEW = 1


def _bf16_terms(x, n, axis=-1):
    terms, r = [], x
    for i in range(n):
        t = r.astype(BF16)
        terms.append(t)
        if i + 1 < n:
            r = r - t.astype(F32)
    return terms[0] if n == 1 else jnp.concatenate(terms, axis=axis)


def _skewed(units, stages):
    for step in range(len(units) + (len(stages) - 1) * SB_SKEW):
        for k, stage in enumerate(stages):
            idx = step - k * SB_SKEW
            if 0 <= idx < len(units):
                stage(units[idx])


def _sb_kernel(q_ref, k_ref, v_ref, o_ref, acc_ref, carry_ref, todo_ref, *, blk, npair):
    i = pl.program_id(1)
    half = blk // 2
    nh = 2 * npair
    lane = lax.broadcasted_iota(jnp.int32, (1, LANES), 1)
    first = lane < SB_HEAD_DIM
    t_idx = lax.broadcasted_iota(jnp.int32, (blk, blk), 0)
    s_idx = lax.broadcasted_iota(jnp.int32, (blk, blk), 1)
    causal = s_idx < t_idx
    after = jnp.concatenate([(t_idx > s_idx).astype(BF16)] * SB_TERMS, axis=0)
    after_half = jnp.concatenate([(t_idx > s_idx)[:half, :half].astype(BF16)] * SB_TERMS, axis=0)

    def lanes(p):
        return slice(p * LANES, (p + 1) * LANES)

    def head_queries(rows):
        out = []
        for p in range(npair):
            q = q_ref[0, rows, lanes(p)]
            out += [jnp.where(keep, q, jnp.zeros_like(q)) for keep in (first, jnp.logical_not(first))]
        return out

    def softplus2(z):
        return jnp.where(z > SB_LINEAR_ABOVE, z, jnp.log2(1.0 + jnp.exp2(z)))

    def merge_heads(accs):
        return jnp.concatenate([jnp.where(first, accs[2 * p], accs[2 * p + 1]) for p in range(npair)], axis=-1)

    def windowed():
        units = [(hf, p) for hf in range(2) for p in range(npair)]
        qh, kw, vw = {}, {}, {}
        for hf in range(2):
            start = pl.multiple_of((i - 1) * blk + hf * half, half)
            qs = head_queries(slice(hf * half, (hf + 1) * half))
            for p in range(npair):
                qh[hf, p] = jnp.concatenate(qs[2 * p:2 * p + 2], axis=0)
                kw[hf, p] = k_ref[0, pl.ds(start, blk + half), lanes(p)]
                vw[hf, p] = v_ref[0, pl.ds(start, blk + half), lanes(p)]
        own = jnp.concatenate([causal[:half, :half]] * 2, axis=0)
        zs, logb, later, total, pvs = {}, {}, {}, {}, {}

        def logits(u):
            z = lax.dot_general(qh[u], kw[u], _NT, preferred_element_type=F32)
            zs[u] = jnp.concatenate([z[:, :blk], jnp.where(own, z[:, blk:], -SB_NO_TILE)], axis=1)

        def suffix_sums(u):
            z = zs.pop(u)
            sp = softplus2(z)
            sp_old, sp_new = sp[:, :blk], sp[:, blk:]
            later_new = jnp.dot(_bf16_terms(sp_new, SB_TERMS), after_half, preferred_element_type=F32)
            total_new = later_new[:, 0:1] + sp_new[:, 0:1]
            later_old = jnp.dot(_bf16_terms(sp_old, SB_TERMS), after, preferred_element_type=F32)
            later[u] = (later_old, later_new, total_new)
            logb[u] = z - sp
            total[u] = later_old[:, 0:1] + sp_old[:, 0:1] + total_new

        def weigh(u):
            later_old, later_new, total_new = later.pop(u)
            lb = logb.pop(u)
            a = jnp.exp2(jnp.concatenate([lb[:, :blk] - later_old - total_new, lb[:, blk:] - later_new], axis=1))
            pv = jnp.dot(a.astype(BF16), vw[u], preferred_element_type=F32)
            pvs[u] = jnp.where(first, pv[:half], pv[half:])

        _skewed(units, (logits, suffix_sums, weigh))
        o_ref[0] = jnp.concatenate([jnp.concatenate([pvs[hf, p] for p in range(npair)], axis=-1)
                                    for hf in range(2)], axis=0).astype(o_ref.dtype)
        low = [jnp.min(functools.reduce(jnp.minimum, [total[hf, p] for p in range(npair)])) for hf in range(2)]
        return jnp.logical_and(jnp.logical_or(low[0] >= SB_SKIP_LOG2, i == 1), low[1] >= SB_SKIP_LOG2)

    def sweep(qs, tiles, carries):
        ks, vs = [], []
        for j, _, _ in tiles:
            rows = pl.ds(pl.multiple_of(j * blk, blk), blk)
            ks.append([k_ref[0, rows, lanes(p)] for p in range(npair)])
            vs.append([v_ref[0, rows, lanes(p)] for p in range(npair)])
        units = [(n, h) for n in range(len(tiles)) for h in range(nh)]
        carries, pvs = list(carries), [None] * nh
        zs, logb, later, total = {}, {}, {}, {}

        def logits(u):
            n, h = u
            z = lax.dot_general(qs[h], ks[n][h // 2], _NT, preferred_element_type=F32)
            zs[u] = jnp.where(causal, z, -SB_NO_TILE) if tiles[n][1] else z

        def suffix_sums(u):
            z = zs.pop(u)
            sp = softplus2(z)
            later[u] = jnp.dot(_bf16_terms(sp, SB_TERMS), after, preferred_element_type=F32)
            logb[u] = z - sp
            total[u] = later[u][:, 0:1] + sp[:, 0:1]

        def weigh(u):
            n, h = u
            if tiles[n][2] is not None:
                carries[h] = carries[h] + tiles[n][2]
            loga = logb.pop(u) - later.pop(u)
            a = jnp.exp2(loga if carries[h] is None else loga - carries[h])
            pv = jnp.dot(a.astype(BF16), vs[n][h // 2], preferred_element_type=F32)
            pvs[h] = pv if pvs[h] is None else pvs[h] + pv
            carries[h] = total[u] if carries[h] is None else carries[h] + total[u]

        _skewed(units, (logits, suffix_sums, weigh))
        done = jnp.min(functools.reduce(jnp.minimum, carries)) >= SB_SKIP_LOG2
        return pvs, carries, done

    def general():
        qs = head_queries(slice(None))
        no_prev = jnp.where(i == 0, jnp.float32(SB_NO_TILE), jnp.float32(0.0))
        accs, carries, done = sweep(qs, [(i, True, None), (jnp.maximum(i - 1, 0), False, no_prev)], [None] * nh)
        o_ref[0] = merge_heads(accs).astype(o_ref.dtype)
        for h in range(nh):
            acc_ref[h] = accs[h]
            carry_ref[h] = carries[h]

        @pl.when(jnp.logical_and(i >= 2, jnp.logical_not(done)))
        def _():
            def body(st):
                n, _ = st
                pvs, new_carries, done = sweep(qs, [(i - n, False, None)], [carry_ref[h] for h in range(nh)])
                for h in range(nh):
                    acc_ref[h] += pvs[h]
                    carry_ref[h] = new_carries[h]
                return n + 1, done.astype(jnp.int32)

            lax.while_loop(lambda st: jnp.logical_and(st[0] <= i, st[1] == 0), body, (jnp.int32(2), jnp.int32(0)))
            o_ref[0] = merge_heads([acc_ref[h] for h in range(nh)]).astype(o_ref.dtype)

    todo_ref[0] = 1

    @pl.when(i >= 1)
    def _():
        todo_ref[0] = jnp.where(windowed(), 0, 1)

    pl.when(todo_ref[0] == 1)(general)


def _stickbreak(sb, blk):
    b, t, w3 = sb.shape
    w = w3 // 3
    return pl.pallas_call(
        functools.partial(_sb_kernel, blk=blk, npair=w // LANES),
        grid=(b, t // blk),
        in_specs=[pl.BlockSpec((1, blk, w), lambda bi, i: (bi, i, 0)),
                  pl.BlockSpec((1, t, w), lambda bi, i: (bi, 0, 1)),
                  pl.BlockSpec((1, t, w), lambda bi, i: (bi, 0, 2))],
        out_specs=pl.BlockSpec((1, blk, w), lambda bi, i: (bi, i, 0)),
        out_shape=jax.ShapeDtypeStruct((b, t, w), BF16),
        scratch_shapes=[pltpu.VMEM((SB_HEADS, blk, LANES), F32), pltpu.VMEM((SB_HEADS, blk, 1), F32),
                        pltpu.SMEM((1,), jnp.int32)],
        compiler_params=_params("stickbreak", ("parallel", "arbitrary")),
        name="stickbreak",
    )(sb, sb, sb)


def _mem_fold_kernel(m_ref, g_ref, wkv_ref, wq_ref, wo_ref, sk_ref, vo_ref):
    d = m_ref.shape[-1]
    dh = d // XATTN_HEADS
    nm = m_ref.shape[1]
    m = _rms(m_ref[0], g_ref[...]).astype(BF16)
    kv = jnp.dot(m, wkv_ref[...].astype(BF16), preferred_element_type=F32).astype(BF16)
    for e in range(XATTN_HEADS):
        hs = slice(e * dh, (e + 1) * dh)
        sk = lax.dot_general(wq_ref[:, hs].astype(BF16), kv[:, hs], _NT, preferred_element_type=F32)
        sk_ref[0, :, e * nm:(e + 1) * nm] = (sk * (dh ** -0.5)).astype(sk_ref.dtype)
        vo = jnp.dot(kv[:, d + e * dh:d + (e + 1) * dh], wo_ref[hs, :].astype(BF16),
                     preferred_element_type=F32)
        vo_ref[0, e * nm:(e + 1) * nm, :] = vo.astype(vo_ref.dtype)


def _mem_fold(mem, g, wkv, wq, wo):
    b, nm, d = mem.shape
    hn = XATTN_HEADS * nm
    return pl.pallas_call(
        _mem_fold_kernel,
        grid=(b,),
        in_specs=[pl.BlockSpec((1, nm, d), lambda bi: (bi, 0, 0)), _resident((1, d)),
                  _resident(wkv.shape), _resident(wq.shape), _resident(wo.shape)],
        out_specs=[pl.BlockSpec((1, d, hn), lambda bi: (bi, 0, 0)),
                   pl.BlockSpec((1, hn, d), lambda bi: (bi, 0, 0))],
        out_shape=[jax.ShapeDtypeStruct((b, d, hn), BF16), jax.ShapeDtypeStruct((b, hn, d), BF16)],
        compiler_params=_params("mem_fold", ("parallel",)),
        name="mem_fold",
    )(mem, g, wkv, wq, wo)


def _mix_stage(x, ya, yb, g_ref, wg_ref, wa_ref, wb_ref, wo_ref):
    d = x.shape[-1]
    h = _rms(x, g_ref[...]).astype(BF16)
    gate = _sigmoid(jnp.dot(h, wg_ref[...], preferred_element_type=F32))
    pa = jnp.dot(ya, wa_ref[...], preferred_element_type=F32)
    pb = jnp.dot(yb, wb_ref[...], preferred_element_type=F32)
    merged = gate[:, :d] * pa + gate[:, d:] * pb
    return x + jnp.dot(merged.astype(BF16), wo_ref[...], preferred_element_type=F32)


def _xattn_stage(x, g_ref, sk_ref, vo_ref):
    nm = sk_ref.shape[-1] // XATTN_HEADS
    h = _rms(x, g_ref[...]).astype(BF16)
    s = jnp.dot(h, sk_ref[0], preferred_element_type=F32)
    probs = []
    for e in range(XATTN_HEADS):
        s_e = s[:, e * nm:(e + 1) * nm]
        p = jnp.exp(s_e - jnp.max(s_e, axis=-1, keepdims=True))
        probs.append((p / jnp.sum(p, axis=-1, keepdims=True)).astype(BF16))
    return x + jnp.dot(jnp.concatenate(probs, axis=-1), vo_ref[0], preferred_element_type=F32)


def _ffn_stage(x, g_ref, wi_ref, wo_ref):
    dff = wo_ref.shape[0]
    h = _rms(x, g_ref[...]).astype(BF16)
    gate = jnp.dot(h, wi_ref[:, :dff], preferred_element_type=F32)
    up = jnp.dot(h, wi_ref[:, dff:], preferred_element_type=F32)
    act = (gate * _sigmoid(gate) * up).astype(BF16)
    return x + jnp.dot(act, wo_ref[...], preferred_element_type=F32)


def _post_kernel(x_ref, ya_ref, yb_ref, gm_ref, wg_ref, wa_ref, wb_ref, wout_ref, gx_ref, sk_ref, vo_ref,
                 gf_ref, wi_ref, wo_ref, gfin_ref, o_ref, *, sub, final_norm):
    def sub_block(r):
        rows = slice(r, r + sub)
        x = _mix_stage(x_ref[0, rows, :], ya_ref[0, rows, :], yb_ref[0, rows, :], gm_ref, wg_ref, wa_ref,
                       wb_ref, wout_ref)
        yield
        x = _xattn_stage(x, gx_ref, sk_ref, vo_ref)
        yield
        x = _ffn_stage(x, gf_ref, wi_ref, wo_ref)
        o_ref[0, rows, :] = _rms(x, gfin_ref[...]) if final_norm else x

    blocks = [sub_block(r) for r in range(0, x_ref.shape[1], sub)]
    for step in range(len(blocks) + 2):
        for stage in range(3):
            n = step - stage
            if 0 <= n < len(blocks):
                next(blocks[n], None)


def _post(x3d, ya, yb, gm, wg, wa, wb, wout, gx, sk, vo, gf, wi, wo, gfin, final_norm, tm, sub):
    b, t, d = x3d.shape
    rows = lambda n: pl.BlockSpec((1, tm, n), lambda bi, i: (bi, i, 0))
    per_batch = lambda a: pl.BlockSpec((1,) + a.shape[1:], lambda bi, i: (bi, 0, 0))
    return pl.pallas_call(
        functools.partial(_post_kernel, sub=sub, final_norm=final_norm),
        grid=(b, t // tm),
        in_specs=[rows(d), rows(ya.shape[-1]), rows(yb.shape[-1]), _resident((1, d)), _resident(wg.shape),
                  _resident(wa.shape), _resident(wb.shape), _resident(wout.shape), _resident((1, d)),
                  per_batch(sk), per_batch(vo), _resident((1, d)), _resident(wi.shape), _resident(wo.shape),
                  _resident((1, d))],
        out_specs=rows(d),
        out_shape=jax.ShapeDtypeStruct((b, t, d), F32),
        compiler_params=_params("post", ("parallel", "parallel")),
        name="post",
    )(x3d, ya, yb, gm, wg, wa, wb, wout, gx, sk, vo, gf, wi, wo, gfin)


def _tile(n, want):
    return want if n % want == 0 else n


def kernel(x, mem, g_mix, w_in, lb_table, g_hgrn, w_gate, w_proj_a, w_proj_b, w_out, g_xattn, g_mem,
           w_xq, w_xkv, w_xo, g_ffn, w_ffn_in, w_ffn_out, g_final):
    b, t, d = x.shape
    depth = w_in.shape[0]
    hq = HGRN_HEADS * HGRN_D
    sbw = SB_HEADS * SB_HEAD_DIM
    row = lambda v: v.reshape(1, -1).astype(F32)
    lb_all = jnp.cumsum(jax.nn.softmax(lb_table.astype(F32), axis=0), axis=0)

    sb_scale = jnp.ones((1, 3 * sbw), F32).at[:, :sbw].set(SB_HEAD_DIM ** -0.5 * LOG2E)

    blk_rows, sub_rows = _tile(t, 1024), _tile(t, 512)
    post_rows, post_sub_rows = _tile(t, 512), _tile(t, 256)

    x2d = x.reshape(b * t, d)
    for l in range(depth):
        hg, sb = _in_proj(x2d, row(g_mix[l]), w_in[l], sb_scale, 4 * hq, blk_rows, sub_rows)
        ya = _hgrn(hg.reshape(b, t, 4 * hq), row(lb_all[l]), row(g_hgrn[l]), blk_rows)
        yb = _stickbreak(sb.reshape(b, t, 3 * sbw), _tile(t, SB_BLOCK))
        score_w, out_w = _mem_fold(mem, row(g_mem[l]), w_xkv[l], w_xq[l], w_xo[l])
        x2d = _post(x2d.reshape(b, t, d), ya, yb, row(g_mix[l]), w_gate[l].astype(BF16),
                    w_proj_a[l].astype(BF16), w_proj_b[l].astype(BF16), w_out[l].astype(BF16),
                    row(g_xattn[l]), score_w, out_w, row(g_ffn[l]), w_ffn_in[l].astype(BF16),
                    w_ffn_out[l].astype(BF16), row(g_final), l == depth - 1,
                    post_rows, post_sub_rows).reshape(b * t, d)
    return x2d.reshape(b, t, d)
```

```python
import functools

import jax
import jax.numpy as jnp
import numpy as np
from jax import lax
from jax.experimental import pallas as pl
from jax.experimental.pallas import tpu as pltpu

F32 = jnp.float32
BF16 = jnp.bfloat16
EPS = 1e-6

CHUNK = 64
HGRN_HEADS = 4
HGRN_D = 128
SB_HEADS = 8
SB_HEAD_DIM = 64
XATTN_HEADS = 4

LANES = 128
BF16_SUBLANES = 16
SB_BLOCK = 256

VMEM_LIMIT_MB = dict(in_proj=56, hgrn2=48, stickbreak=56, mem_fold=48, post=58)

_NT = (((1,), (1,)), ((), ()))
_TN = (((0,), (0,)), ((), ()))


def _rms(x, g):
    return x * lax.rsqrt(jnp.mean(x * x, axis=-1, keepdims=True) + EPS) * g


def _sigmoid(x):
    return 1.0 / (1.0 + jnp.exp(-x))


def _params(name, sem):
    return pltpu.CompilerParams(dimension_semantics=sem, vmem_limit_bytes=VMEM_LIMIT_MB[name] * 1024 * 1024)


def _resident(shape):
    nd = len(shape)
    return pl.BlockSpec(shape, lambda *_: (0,) * nd, pipeline_mode=pl.Buffered(1))


def _in_proj_kernel(x_ref, g_ref, w_ref, scale_ref, hg_ref, sb_ref, *, sub):
    n_hg = hg_ref.shape[-1]
    w_hg = w_ref[:, :n_hg].astype(BF16)
    w_sb = w_ref[:, n_hg:].astype(BF16)
    for r in range(0, x_ref.shape[0], sub):
        h = _rms(x_ref[r:r + sub, :], g_ref[...]).astype(BF16)
        hg_ref[r:r + sub, :] = jnp.dot(h, w_hg, preferred_element_type=F32)
        sb = jnp.dot(h, w_sb, preferred_element_type=F32) * scale_ref[...]
        sb_ref[r:r + sub, :] = sb.astype(sb_ref.dtype)


def _in_proj(x2d, g, w, sb_scale, n_hg, tm, sub):
    m, d = x2d.shape
    n = w.shape[1]
    return pl.pallas_call(
        functools.partial(_in_proj_kernel, sub=sub),
        grid=(m // tm,),
        in_specs=[pl.BlockSpec((tm, d), lambda i: (i, 0)), _resident((1, d)), _resident((d, n)),
                  _resident((1, n - n_hg))],
        out_specs=[pl.BlockSpec((tm, n_hg), lambda i: (i, 0)), pl.BlockSpec((tm, n - n_hg), lambda i: (i, 0))],
        out_shape=[jax.ShapeDtypeStruct((m, n_hg), F32), jax.ShapeDtypeStruct((m, n - n_hg), BF16)],
        compiler_params=_params("in_proj", ("parallel",)),
        name="in_proj",
    )(x2d, g, w, sb_scale)


HGRN_LEVELS = (32, 16, 8)
HGRN_DIAG = 8


def _hgrn_masks():
    c = CHUNK
    t = lax.broadcasted_iota(jnp.int32, (c, c), 0)
    s = lax.broadcasted_iota(jnp.int32, (c, c), 1)
    masks = []
    for m in HGRN_LEVELS:
        same = (t // (2 * m)) == (s // (2 * m))
        masks.append(same & ((t % (2 * m)) >= m) & ((s % (2 * m)) < m))
    masks.append(((t // HGRN_DIAG) == (s // HGRN_DIAG)) & (s <= t))
    return masks


HGRN_CUMSUM_TERMS = 2
HGRN_GROUP = 1


def _silu(x):
    hx = 0.5 * x
    return hx + hx * jnp.tanh(hx)


def _hgrn_kernel(hg_ref, lb_ref, gn_ref, tril_ref, *refs, n_chunks, group, n_riders):
    c, nh, dh = CHUNK, HGRN_HEADS, HGRN_D
    w = nh * dh
    rider_in, o_ref, rider_out, st_ref = refs[:n_riders], refs[n_riders], refs[n_riders + 1:-1], refs[-1]
    for src, dst in zip(rider_in, rider_out):
        dst[...] = src[...].astype(dst.dtype)

    @pl.when(pl.program_id(1) == 0)
    def _():
        st_ref[...] = jnp.zeros_like(st_ref)

    masks = _hgrn_masks()
    states = [st_ref[h] for h in range(nh)]
    groups = [_hgrn_group(hg_ref, lb_ref, gn_ref, tril_ref, o_ref, states, masks, c0, group)
              for c0 in range(0, n_chunks, group)]
    for step in range(len(groups) + 2):
        for phase in range(3):
            g = step - phase
            if 0 <= g < len(groups):
                next(groups[g], None)
    for h in range(nh):
        st_ref[h] = states[h]


def _hgrn_group(hg_ref, lb_ref, gn_ref, tril_ref, o_ref, states, masks, c0, n):
    c, nh, dh = CHUNK, HGRN_HEADS, HGRN_D
    w = nh * dh

    def wide(col):
        return jnp.concatenate(
            [hg_ref[0, (c0 + ci) * c:(c0 + ci + 1) * c, col * w:(col + 1) * w] for ci in range(n)], axis=1)

    def tiled(row):
        return jnp.concatenate([row] * n, axis=1)

    qp, fp, iv, gp = wide(0), wide(1), wide(2), wide(3)
    lb = tiled(lb_ref[...])
    q = _silu(qp)
    f = 0.5 * (1.0 + lb) + (0.5 * (1.0 - lb)) * jnp.tanh(0.5 * fp)
    k = 1.0 - f
    b = jnp.dot(tril_ref[...], _bf16_terms(jnp.log(f) * LOG2E, HGRN_CUMSUM_TERMS, axis=0),
                preferred_element_type=F32)

    units = [(ci, h) for ci in range(n) for h in range(nh)]
    lanes = {u: slice(u[0] * w + u[1] * dh, u[0] * w + (u[1] + 1) * dh) for u in units}
    e_b = jnp.exp2(b)
    e_last = e_b[c - 1:c, :]
    q_in = (q * e_b).astype(BF16)
    k_out = (k * jnp.exp2(b[c - 1:c, :] - b)).astype(BF16)
    v16 = iv.astype(BF16)
    sides = []
    for m in HGRN_LEVELS:
        decay, qk = [], []
        for s0 in range(0, c, 2 * m):
            bp = b[s0 + m:s0 + m + 1, :]
            decay += [bp - b[s0:s0 + m, :], b[s0 + m:s0 + 2 * m, :] - bp]
            qk += [k[s0:s0 + m, :], q[s0 + m:s0 + 2 * m, :]]
        both = (jnp.concatenate(qk, axis=0) * jnp.exp2(jnp.concatenate(decay, axis=0))).astype(BF16)
        sides.append((both, both))
    half = HGRN_DIAG // 2
    dd = jnp.concatenate([b[s0:s0 + HGRN_DIAG, :] - b[s0 + half:s0 + half + 1, :]
                          for s0 in range(0, c, HGRN_DIAG)], axis=0)
    sides.append(((q * jnp.exp2(dd)).astype(BF16), (k * jnp.exp2(-dd)).astype(BF16)))
    gate = tiled(gn_ref[...]) * _silu(gp)
    yield

    kv = {u: lax.dot_general(v16[:, lanes[u]], k_out[:, lanes[u]], _TN, preferred_element_type=F32)
          for u in units}
    inter = {}
    for h in range(nh):
        for ci in range(n):
            u = (ci, h)
            inter[u] = lax.dot_general(q_in[:, lanes[u]], states[h].astype(BF16), _NT,
                                       preferred_element_type=F32)
            states[h] = e_last[:, lanes[u]] * states[h] + kv[u]
    parts = {u: [lax.dot_general(a_l[:, lanes[u]], b_l[:, lanes[u]], _NT, preferred_element_type=F32)
                 for a_l, b_l in sides] for u in units}
    yield

    scores = {}
    for u in units:
        s_u = jnp.zeros((c, c), F32)
        for s_l, m_l in zip(parts[u], masks):
            s_u = jnp.where(m_l, s_l, s_u)
        scores[u] = s_u.astype(BF16)
    for ci in range(n):
        outs = []
        for h in range(nh):
            u = (ci, h)
            o = jnp.dot(scores[u], v16[:, lanes[u]], preferred_element_type=F32) + inter[u]
            outs.append(o * lax.rsqrt(jnp.mean(o * o, axis=-1, keepdims=True) + EPS))
        on = jnp.concatenate(outs, axis=-1) * gate[:, ci * w:(ci + 1) * w]
        o_ref[0, (c0 + ci) * c:(c0 + ci + 1) * c, :] = on.astype(o_ref.dtype)


def _hgrn(hg, lb, gn, rows_per_step, riders):
    b, t, w4 = hg.shape
    w = w4 // 4
    nsteps = t // rows_per_step
    tril = np.tril(np.ones((CHUNK, CHUNK), np.float32))
    tril3 = jnp.asarray(np.concatenate([tril] * HGRN_CUMSUM_TERMS, axis=1), BF16)
    slices = [r.shape[0] // (b * nsteps) for r in riders]
    assert all(r.shape[0] == s * b * nsteps and s % BF16_SUBLANES == 0 for r, s in zip(riders, slices))
    rider_specs = [pl.BlockSpec((s, r.shape[1]), lambda bi, i: (bi * nsteps + i, 0)) for r, s in zip(riders, slices)]
    return pl.pallas_call(
        functools.partial(_hgrn_kernel, n_chunks=rows_per_step // CHUNK, group=HGRN_GROUP, n_riders=len(riders)),
        grid=(b, nsteps),
        in_specs=[pl.BlockSpec((1, rows_per_step, w4), lambda bi, i: (bi, i, 0)),
                  _resident((1, w)), _resident((1, w)), _resident(tril3.shape)] + rider_specs,
        out_specs=[pl.BlockSpec((1, rows_per_step, w), lambda bi, i: (bi, i, 0))] + rider_specs,
        out_shape=[jax.ShapeDtypeStruct((b, t, w), BF16)] + [jax.ShapeDtypeStruct(r.shape, BF16) for r in riders],
        scratch_shapes=[pltpu.VMEM((HGRN_HEADS, HGRN_D, HGRN_D), F32)],
        compiler_params=_params("hgrn2", ("parallel", "arbitrary")),
        name="hgrn2",
    )(hg, lb, gn, tril3, *riders)


LOG2E = 1.4426950408889634
SB_SKIP_LOG2 = 160.0
SB_TERMS = 1
SB_NO_TILE = 1e30
SB_LINEAR_ABOVE = 64.0
SB_SKEW = 1


def _bf16_terms(x, n, axis=-1):
    terms, r = [], x
    for i in range(n):
        t = r.astype(BF16)
        terms.append(t)
        if i + 1 < n:
            r = r - t.astype(F32)
    return terms[0] if n == 1 else jnp.concatenate(terms, axis=axis)


def _skewed(units, stages):
    for step in range(len(units) + (len(stages) - 1) * SB_SKEW):
        for k, stage in enumerate(stages):
            idx = step - k * SB_SKEW
            if 0 <= idx < len(units):
                stage(units[idx])


def _sb_kernel(q_ref, k_ref, v_ref, o_ref, acc_ref, carry_ref, todo_ref, *, blk, npair):
    i = pl.program_id(1)
    half = blk // 2
    nh = 2 * npair
    lane = lax.broadcasted_iota(jnp.int32, (1, LANES), 1)
    first = lane < SB_HEAD_DIM
    t_idx = lax.broadcasted_iota(jnp.int32, (blk, blk), 0)
    s_idx = lax.broadcasted_iota(jnp.int32, (blk, blk), 1)
    causal = s_idx < t_idx
    after = jnp.concatenate([(t_idx > s_idx).astype(BF16)] * SB_TERMS, axis=0)
    after_half = jnp.concatenate([(t_idx > s_idx)[:half, :half].astype(BF16)] * SB_TERMS, axis=0)

    def lanes(p):
        return slice(p * LANES, (p + 1) * LANES)

    def head_queries(rows):
        out = []
        for p in range(npair):
            q = q_ref[0, rows, lanes(p)]
            out += [jnp.where(keep, q, jnp.zeros_like(q)) for keep in (first, jnp.logical_not(first))]
        return out

    def softplus2(z):
        return jnp.where(z > SB_LINEAR_ABOVE, z, jnp.log2(1.0 + jnp.exp2(z)))

    def merge_heads(accs):
        return jnp.concatenate([jnp.where(first, accs[2 * p], accs[2 * p + 1]) for p in range(npair)], axis=-1)

    def windowed():
        units = [(hf, p) for hf in range(2) for p in range(npair)]
        qh, kw, vw = {}, {}, {}
        for hf in range(2):
            start = pl.multiple_of((i - 1) * blk + hf * half, half)
            qs = head_queries(slice(hf * half, (hf + 1) * half))
            for p in range(npair):
                qh[hf, p] = jnp.concatenate(qs[2 * p:2 * p + 2], axis=0)
                kw[hf, p] = k_ref[0, pl.ds(start, blk + half), lanes(p)]
                vw[hf, p] = v_ref[0, pl.ds(start, blk + half), lanes(p)]
        own = jnp.concatenate([causal[:half, :half]] * 2, axis=0)
        zs, logb, later, total, pvs = {}, {}, {}, {}, {}

        def logits(u):
            z = lax.dot_general(qh[u], kw[u], _NT, preferred_element_type=F32)
            zs[u] = jnp.concatenate([z[:, :blk], jnp.where(own, z[:, blk:], -SB_NO_TILE)], axis=1)

        def suffix_sums(u):
            z = zs.pop(u)
            sp = softplus2(z)
            sp_old, sp_new = sp[:, :blk], sp[:, blk:]
            later_new = jnp.dot(_bf16_terms(sp_new, SB_TERMS), after_half, preferred_element_type=F32)
            total_new = later_new[:, 0:1] + sp_new[:, 0:1]
            later_old = jnp.dot(_bf16_terms(sp_old, SB_TERMS), after, preferred_element_type=F32)
            later[u] = (later_old, later_new, total_new)
            logb[u] = z - sp
            total[u] = later_old[:, 0:1] + sp_old[:, 0:1] + total_new

        def weigh(u):
            later_old, later_new, total_new = later.pop(u)
            lb = logb.pop(u)
            a = jnp.exp2(jnp.concatenate([lb[:, :blk] - later_old - total_new, lb[:, blk:] - later_new], axis=1))
            pv = jnp.dot(a.astype(BF16), vw[u], preferred_element_type=F32)
            pvs[u] = jnp.where(first, pv[:half], pv[half:])

        _skewed(units, (logits, suffix_sums, weigh))
        o_ref[0] = jnp.concatenate([jnp.concatenate([pvs[hf, p] for p in range(npair)], axis=-1)
                                    for hf in range(2)], axis=0).astype(o_ref.dtype)
        low = [jnp.min(functools.reduce(jnp.minimum, [total[hf, p] for p in range(npair)])) for hf in range(2)]
        return jnp.logical_and(jnp.logical_or(low[0] >= SB_SKIP_LOG2, i == 1), low[1] >= SB_SKIP_LOG2)

    def sweep(qs, tiles, carries):
        ks, vs = [], []
        for j, _, _ in tiles:
            rows = pl.ds(pl.multiple_of(j * blk, blk), blk)
            ks.append([k_ref[0, rows, lanes(p)] for p in range(npair)])
            vs.append([v_ref[0, rows, lanes(p)] for p in range(npair)])
        units = [(n, h) for n in range(len(tiles)) for h in range(nh)]
        carries, pvs = list(carries), [None] * nh
        zs, logb, later, total = {}, {}, {}, {}

        def logits(u):
            n, h = u
            z = lax.dot_general(qs[h], ks[n][h // 2], _NT, preferred_element_type=F32)
            zs[u] = jnp.where(causal, z, -SB_NO_TILE) if tiles[n][1] else z

        def suffix_sums(u):
            z = zs.pop(u)
            sp = softplus2(z)
            later[u] = jnp.dot(_bf16_terms(sp, SB_TERMS), after, preferred_element_type=F32)
            logb[u] = z - sp
            total[u] = later[u][:, 0:1] + sp[:, 0:1]

        def weigh(u):
            n, h = u
            if tiles[n][2] is not None:
                carries[h] = carries[h] + tiles[n][2]
            loga = logb.pop(u) - later.pop(u)
            a = jnp.exp2(loga if carries[h] is None else loga - carries[h])
            pv = jnp.dot(a.astype(BF16), vs[n][h // 2], preferred_element_type=F32)
            pvs[h] = pv if pvs[h] is None else pvs[h] + pv
            carries[h] = total[u] if carries[h] is None else carries[h] + total[u]

        _skewed(units, (logits, suffix_sums, weigh))
        done = jnp.min(functools.reduce(jnp.minimum, carries)) >= SB_SKIP_LOG2
        return pvs, carries, done

    def general():
        qs = head_queries(slice(None))
        no_prev = jnp.where(i == 0, jnp.float32(SB_NO_TILE), jnp.float32(0.0))
        accs, carries, done = sweep(qs, [(i, True, None), (jnp.maximum(i - 1, 0), False, no_prev)], [None] * nh)
        o_ref[0] = merge_heads(accs).astype(o_ref.dtype)
        for h in range(nh):
            acc_ref[h] = accs[h]
            carry_ref[h] = carries[h]

        @pl.when(jnp.logical_and(i >= 2, jnp.logical_not(done)))
        def _():
            def body(st):
                n, _ = st
                pvs, new_carries, done = sweep(qs, [(i - n, False, None)], [carry_ref[h] for h in range(nh)])
                for h in range(nh):
                    acc_ref[h] += pvs[h]
                    carry_ref[h] = new_carries[h]
                return n + 1, done.astype(jnp.int32)

            lax.while_loop(lambda st: jnp.logical_and(st[0] <= i, st[1] == 0), body, (jnp.int32(2), jnp.int32(0)))
            o_ref[0] = merge_heads([acc_ref[h] for h in range(nh)]).astype(o_ref.dtype)

    todo_ref[0] = 1

    @pl.when(i >= 1)
    def _():
        todo_ref[0] = jnp.where(windowed(), 0, 1)

    pl.when(todo_ref[0] == 1)(general)


def _stickbreak(sb, blk):
    b, t, w3 = sb.shape
    w = w3 // 3
    return pl.pallas_call(
        functools.partial(_sb_kernel, blk=blk, npair=w // LANES),
        grid=(b, t // blk),
        in_specs=[pl.BlockSpec((1, blk, w), lambda bi, i: (bi, i, 0)),
                  pl.BlockSpec((1, t, w), lambda bi, i: (bi, 0, 1)),
                  pl.BlockSpec((1, t, w), lambda bi, i: (bi, 0, 2))],
        out_specs=pl.BlockSpec((1, blk, w), lambda bi, i: (bi, i, 0)),
        out_shape=jax.ShapeDtypeStruct((b, t, w), BF16),
        scratch_shapes=[pltpu.VMEM((SB_HEADS, blk, LANES), F32), pltpu.VMEM((SB_HEADS, blk, 1), F32),
                        pltpu.SMEM((1,), jnp.int32)],
        compiler_params=_params("stickbreak", ("parallel", "arbitrary")),
        name="stickbreak",
    )(sb, sb, sb)


def _mem_fold_kernel(m_ref, g_ref, wkv_ref, wq_ref, wo_ref, sk_ref, vo_ref):
    d = m_ref.shape[-1]
    dh = d // XATTN_HEADS
    nm = m_ref.shape[1]
    m = _rms(m_ref[0], g_ref[...]).astype(BF16)
    kv = jnp.dot(m, wkv_ref[...].astype(BF16), preferred_element_type=F32).astype(BF16)
    for e in range(XATTN_HEADS):
        hs = slice(e * dh, (e + 1) * dh)
        sk = lax.dot_general(wq_ref[:, hs].astype(BF16), kv[:, hs], _NT, preferred_element_type=F32)
        sk_ref[0, :, e * nm:(e + 1) * nm] = (sk * (dh ** -0.5)).astype(sk_ref.dtype)
        vo = jnp.dot(kv[:, d + e * dh:d + (e + 1) * dh], wo_ref[hs, :].astype(BF16),
                     preferred_element_type=F32)
        vo_ref[0, e * nm:(e + 1) * nm, :] = vo.astype(vo_ref.dtype)


def _mem_fold(mem, g, wkv, wq, wo):
    b, nm, d = mem.shape
    hn = XATTN_HEADS * nm
    return pl.pallas_call(
        _mem_fold_kernel,
        grid=(b,),
        in_specs=[pl.BlockSpec((1, nm, d), lambda bi: (bi, 0, 0)), _resident((1, d)),
                  _resident(wkv.shape), _resident(wq.shape), _resident(wo.shape)],
        out_specs=[pl.BlockSpec((1, d, hn), lambda bi: (bi, 0, 0)),
                   pl.BlockSpec((1, hn, d), lambda bi: (bi, 0, 0))],
        out_shape=[jax.ShapeDtypeStruct((b, d, hn), BF16), jax.ShapeDtypeStruct((b, hn, d), BF16)],
        compiler_params=_params("mem_fold", ("parallel",)),
        name="mem_fold",
    )(mem, g, wkv, wq, wo)


def _mix_stage(x, ya, yb, g_ref, wg_ref, wa_ref, wb_ref, wo_ref):
    d = x.shape[-1]
    h = _rms(x, g_ref[...]).astype(BF16)
    gate = _sigmoid(jnp.dot(h, wg_ref[...], preferred_element_type=F32))
    pa = jnp.dot(ya, wa_ref[...], preferred_element_type=F32)
    pb = jnp.dot(yb, wb_ref[...], preferred_element_type=F32)
    merged = gate[:, :d] * pa + gate[:, d:] * pb
    return x + jnp.dot(merged.astype(BF16), wo_ref[...], preferred_element_type=F32)


def _xattn_stage(x, g_ref, sk_ref, vo_ref):
    nm = sk_ref.shape[-1] // XATTN_HEADS
    h = _rms(x, g_ref[...]).astype(BF16)
    s = jnp.dot(h, sk_ref[0], preferred_element_type=F32)
    probs = []
    for e in range(XATTN_HEADS):
        s_e = s[:, e * nm:(e + 1) * nm]
        p = jnp.exp(s_e - jnp.max(s_e, axis=-1, keepdims=True))
        probs.append((p / jnp.sum(p, axis=-1, keepdims=True)).astype(BF16))
    return x + jnp.dot(jnp.concatenate(probs, axis=-1), vo_ref[0], preferred_element_type=F32)


def _ffn_stage(x, g_ref, wi_ref, wo_ref):
    dff = wo_ref.shape[0]
    h = _rms(x, g_ref[...]).astype(BF16)
    gate = jnp.dot(h, wi_ref[:, :dff], preferred_element_type=F32)
    up = jnp.dot(h, wi_ref[:, dff:], preferred_element_type=F32)
    act = (gate * _sigmoid(gate) * up).astype(BF16)
    return x + jnp.dot(act, wo_ref[...], preferred_element_type=F32)


def _post_kernel(x_ref, ya_ref, yb_ref, gm_ref, wg_ref, wa_ref, wb_ref, wout_ref, gx_ref, sk_ref, vo_ref,
                 gf_ref, wi_ref, wo_ref, gfin_ref, o_ref, *, sub, final_norm):
    def sub_block(r):
        rows = slice(r, r + sub)
        x = _mix_stage(x_ref[0, rows, :], ya_ref[0, rows, :], yb_ref[0, rows, :], gm_ref, wg_ref, wa_ref,
                       wb_ref, wout_ref)
        yield
        x = _xattn_stage(x, gx_ref, sk_ref, vo_ref)
        yield
        x = _ffn_stage(x, gf_ref, wi_ref, wo_ref)
        o_ref[0, rows, :] = _rms(x, gfin_ref[...]) if final_norm else x

    blocks = [sub_block(r) for r in range(0, x_ref.shape[1], sub)]
    for step in range(len(blocks) + 2):
        for stage in range(3):
            n = step - stage
            if 0 <= n < len(blocks):
                next(blocks[n], None)


def _post(x3d, ya, yb, gm, wg, wa, wb, wout, gx, sk, vo, gf, wi, wo, gfin, final_norm, tm, sub):
    b, t, d = x3d.shape
    rows = lambda n: pl.BlockSpec((1, tm, n), lambda bi, i: (bi, i, 0))
    per_batch = lambda a: pl.BlockSpec((1,) + a.shape[1:], lambda bi, i: (bi, 0, 0))
    return pl.pallas_call(
        functools.partial(_post_kernel, sub=sub, final_norm=final_norm),
        grid=(b, t // tm),
        in_specs=[rows(d), rows(ya.shape[-1]), rows(yb.shape[-1]), _resident((1, d)), _resident(wg.shape),
                  _resident(wa.shape), _resident(wb.shape), _resident(wout.shape), _resident((1, d)),
                  per_batch(sk), per_batch(vo), _resident((1, d)), _resident(wi.shape), _resident(wo.shape),
                  _resident((1, d))],
        out_specs=rows(d),
        out_shape=jax.ShapeDtypeStruct((b, t, d), F32),
        compiler_params=_params("post", ("parallel", "parallel")),
        name="post",
    )(x3d, ya, yb, gm, wg, wa, wb, wout, gx, sk, vo, gf, wi, wo, gfin)


def _tile(n, want):
    return want if n % want == 0 else n


def kernel(x, mem, g_mix, w_in, lb_table, g_hgrn, w_gate, w_proj_a, w_proj_b, w_out, g_xattn, g_mem,
           w_xq, w_xkv, w_xo, g_ffn, w_ffn_in, w_ffn_out, g_final):
    b, t, d = x.shape
    depth = w_in.shape[0]
    hq = HGRN_HEADS * HGRN_D
    sbw = SB_HEADS * SB_HEAD_DIM
    row = lambda v: v.reshape(1, -1).astype(F32)
    lb_all = jnp.cumsum(jax.nn.softmax(lb_table.astype(F32), axis=0), axis=0)

    sb_scale = jnp.ones((1, 3 * sbw), F32).at[:, :sbw].set(SB_HEAD_DIM ** -0.5 * LOG2E)

    blk_rows, sub_rows = _tile(t, 1024), _tile(t, 512)
    post_rows, post_sub_rows = _tile(t, 512), _tile(t, 256)

    x2d = x.reshape(b * t, d)
    for l in range(depth):
        hg, sb = _in_proj(x2d, row(g_mix[l]), w_in[l], sb_scale, 4 * hq, blk_rows, sub_rows)
        post_w = (w_gate[l], w_proj_a[l], w_proj_b[l], w_out[l], w_ffn_in[l], w_ffn_out[l])
        ya, wg, wa, wb, wout, wi, wo = _hgrn(hg.reshape(b, t, 4 * hq), row(lb_all[l]), row(g_hgrn[l]),
                                             blk_rows, post_w)
        yb = _stickbreak(sb.reshape(b, t, 3 * sbw), _tile(t, SB_BLOCK))
        score_w, out_w = _mem_fold(mem, row(g_mem[l]), w_xkv[l], w_xq[l], w_xo[l])
        x2d = _post(x2d.reshape(b, t, d), ya, yb, row(g_mix[l]), wg, wa, wb, wout,
                    row(g_xattn[l]), score_w, out_w, row(g_ffn[l]), wi, wo, row(g_final), l == depth - 1,
                    post_rows, post_sub_rows).reshape(b * t, d)
    return x2d.reshape(b, t, d)
```

```python
import functools

import jax
import jax.numpy as jnp
import numpy as np
from jax import lax
from jax.experimental import pallas as pl
from jax.experimental.pallas import tpu as pltpu

F32 = jnp.float32
BF16 = jnp.bfloat16
EPS = 1e-6

CHUNK = 64
HGRN_HEADS = 4
HGRN_D = 128
SB_HEADS = 8
SB_HEAD_DIM = 64
XATTN_HEADS = 4

LANES = 128
BF16_SUBLANES = 16
SB_BLOCK = 256

VMEM_LIMIT_MB = dict(in_proj=56, hgrn2=48, stickbreak=56, mem_fold=48, post=58)

_NT = (((1,), (1,)), ((), ()))
_TN = (((0,), (0,)), ((), ()))


def _rms(x, g):
    return x * lax.rsqrt(jnp.mean(x * x, axis=-1, keepdims=True) + EPS) * g


def _sigmoid(x):
    return 1.0 / (1.0 + jnp.exp(-x))


def _params(name, sem):
    return pltpu.CompilerParams(dimension_semantics=sem, vmem_limit_bytes=VMEM_LIMIT_MB[name] * 1024 * 1024)


def _resident(shape):
    nd = len(shape)
    return pl.BlockSpec(shape, lambda *_: (0,) * nd, pipeline_mode=pl.Buffered(1))


def _in_proj_kernel(x_ref, g_ref, w_ref, scale_ref, hg_ref, sb_ref, *, sub):
    n_hg = hg_ref.shape[-1]
    w_hg = w_ref[:, :n_hg].astype(BF16)
    w_sb = w_ref[:, n_hg:].astype(BF16)
    for r in range(0, x_ref.shape[0], sub):
        h = _rms(x_ref[r:r + sub, :], g_ref[...]).astype(BF16)
        hg_ref[r:r + sub, :] = jnp.dot(h, w_hg, preferred_element_type=F32)
        sb = jnp.dot(h, w_sb, preferred_element_type=F32) * scale_ref[...]
        sb_ref[r:r + sub, :] = sb.astype(sb_ref.dtype)


def _in_proj(x2d, g, w, sb_scale, n_hg, tm, sub):
    m, d = x2d.shape
    n = w.shape[1]
    return pl.pallas_call(
        functools.partial(_in_proj_kernel, sub=sub),
        grid=(m // tm,),
        in_specs=[pl.BlockSpec((tm, d), lambda i: (i, 0)), _resident((1, d)), _resident((d, n)),
                  _resident((1, n - n_hg))],
        out_specs=[pl.BlockSpec((tm, n_hg), lambda i: (i, 0)), pl.BlockSpec((tm, n - n_hg), lambda i: (i, 0))],
        out_shape=[jax.ShapeDtypeStruct((m, n_hg), F32), jax.ShapeDtypeStruct((m, n - n_hg), BF16)],
        compiler_params=_params("in_proj", ("parallel",)),
        name="in_proj",
    )(x2d, g, w, sb_scale)


HGRN_LEVELS = (32, 16, 8)
HGRN_DIAG = 8


def _hgrn_masks():
    c = CHUNK
    t = lax.broadcasted_iota(jnp.int32, (c, c), 0)
    s = lax.broadcasted_iota(jnp.int32, (c, c), 1)
    masks = []
    for m in HGRN_LEVELS:
        same = (t // (2 * m)) == (s // (2 * m))
        masks.append(same & ((t % (2 * m)) >= m) & ((s % (2 * m)) < m))
    masks.append(((t // HGRN_DIAG) == (s // HGRN_DIAG)) & (s <= t))
    return masks


HGRN_CUMSUM_TERMS = 2
HGRN_GROUP = 1


def _silu(x):
    hx = 0.5 * x
    return hx + hx * jnp.tanh(hx)


def _hgrn_kernel(hg_ref, lb_ref, gn_ref, tril_ref, *refs, n_chunks, group, n_riders):
    c, nh, dh = CHUNK, HGRN_HEADS, HGRN_D
    w = nh * dh
    rider_in, o_ref, rider_out, st_ref = refs[:n_riders], refs[n_riders], refs[n_riders + 1:-1], refs[-1]
    for src, dst in zip(rider_in, rider_out):
        dst[...] = src[...].astype(dst.dtype)

    @pl.when(pl.program_id(1) == 0)
    def _():
        st_ref[...] = jnp.zeros_like(st_ref)

    masks = _hgrn_masks()
    states = [st_ref[h] for h in range(nh)]
    groups = [_hgrn_group(hg_ref, lb_ref, gn_ref, tril_ref, o_ref, states, masks, c0, group)
              for c0 in range(0, n_chunks, group)]
    for step in range(len(groups) + 2):
        for phase in range(3):
            g = step - phase
            if 0 <= g < len(groups):
                next(groups[g], None)
    for h in range(nh):
        st_ref[h] = states[h]


def _hgrn_group(hg_ref, lb_ref, gn_ref, tril_ref, o_ref, states, masks, c0, n):
    c, nh, dh = CHUNK, HGRN_HEADS, HGRN_D
    w = nh * dh

    def wide(col):
        return jnp.concatenate(
            [hg_ref[0, (c0 + ci) * c:(c0 + ci + 1) * c, col * w:(col + 1) * w] for ci in range(n)], axis=1)

    def tiled(row):
        return jnp.concatenate([row] * n, axis=1)

    qp, fp, iv, gp = wide(0), wide(1), wide(2), wide(3)
    lb = tiled(lb_ref[...])
    q = _silu(qp)
    f = 0.5 * (1.0 + lb) + (0.5 * (1.0 - lb)) * jnp.tanh(0.5 * fp)
    k = 1.0 - f
    b = jnp.dot(tril_ref[...], _bf16_terms(jnp.log(f) * LOG2E, HGRN_CUMSUM_TERMS, axis=0),
                preferred_element_type=F32)

    units = [(ci, h) for ci in range(n) for h in range(nh)]
    lanes = {u: slice(u[0] * w + u[1] * dh, u[0] * w + (u[1] + 1) * dh) for u in units}
    e_b = jnp.exp2(b)
    e_last = e_b[c - 1:c, :]
    q_in = (q * e_b).astype(BF16)
    k_out = (k * jnp.exp2(b[c - 1:c, :] - b)).astype(BF16)
    v16 = iv.astype(BF16)
    sides = []
    for m in HGRN_LEVELS:
        decay, qk = [], []
        for s0 in range(0, c, 2 * m):
            bp = b[s0 + m:s0 + m + 1, :]
            decay += [bp - b[s0:s0 + m, :], b[s0 + m:s0 + 2 * m, :] - bp]
            qk += [k[s0:s0 + m, :], q[s0 + m:s0 + 2 * m, :]]
        both = (jnp.concatenate(qk, axis=0) * jnp.exp2(jnp.concatenate(decay, axis=0))).astype(BF16)
        sides.append((both, both))
    half = HGRN_DIAG // 2
    dd = jnp.concatenate([b[s0:s0 + HGRN_DIAG, :] - b[s0 + half:s0 + half + 1, :]
                          for s0 in range(0, c, HGRN_DIAG)], axis=0)
    sides.append(((q * jnp.exp2(dd)).astype(BF16), (k * jnp.exp2(-dd)).astype(BF16)))
    gate = tiled(gn_ref[...]) * _silu(gp)
    yield

    kv = {u: lax.dot_general(v16[:, lanes[u]], k_out[:, lanes[u]], _TN, preferred_element_type=F32)
          for u in units}
    inter = {}
    for h in range(nh):
        for ci in range(n):
            u = (ci, h)
            inter[u] = lax.dot_general(q_in[:, lanes[u]], states[h].astype(BF16), _NT,
                                       preferred_element_type=F32)
            states[h] = e_last[:, lanes[u]] * states[h] + kv[u]
    parts = {u: [lax.dot_general(a_l[:, lanes[u]], b_l[:, lanes[u]], _NT, preferred_element_type=F32)
                 for a_l, b_l in sides] for u in units}
    yield

    scores = {}
    for u in units:
        s_u = jnp.zeros((c, c), F32)
        for s_l, m_l in zip(parts[u], masks):
            s_u = jnp.where(m_l, s_l, s_u)
        scores[u] = s_u.astype(BF16)
    for ci in range(n):
        outs = []
        for h in range(nh):
            u = (ci, h)
            o = jnp.dot(scores[u], v16[:, lanes[u]], preferred_element_type=F32) + inter[u]
            outs.append(o * lax.rsqrt(jnp.mean(o * o, axis=-1, keepdims=True) + EPS))
        on = jnp.concatenate(outs, axis=-1) * gate[:, ci * w:(ci + 1) * w]
        o_ref[0, (c0 + ci) * c:(c0 + ci + 1) * c, :] = on.astype(o_ref.dtype)


def _hgrn(hg, lb, gn, rows_per_step, riders):
    b, t, w4 = hg.shape
    w = w4 // 4
    nsteps = t // rows_per_step
    tril = np.tril(np.ones((CHUNK, CHUNK), np.float32))
    tril3 = jnp.asarray(np.concatenate([tril] * HGRN_CUMSUM_TERMS, axis=1), BF16)
    slices = [r.shape[0] // (b * nsteps) for r in riders]
    assert all(r.shape[0] == s * b * nsteps and s % BF16_SUBLANES == 0 for r, s in zip(riders, slices))
    rider_specs = [pl.BlockSpec((s, r.shape[1]), lambda bi, i: (bi * nsteps + i, 0)) for r, s in zip(riders, slices)]
    return pl.pallas_call(
        functools.partial(_hgrn_kernel, n_chunks=rows_per_step // CHUNK, group=HGRN_GROUP, n_riders=len(riders)),
        grid=(b, nsteps),
        in_specs=[pl.BlockSpec((1, rows_per_step, w4), lambda bi, i: (bi, i, 0)),
                  _resident((1, w)), _resident((1, w)), _resident(tril3.shape)] + rider_specs,
        out_specs=[pl.BlockSpec((1, rows_per_step, w), lambda bi, i: (bi, i, 0))] + rider_specs,
        out_shape=[jax.ShapeDtypeStruct((b, t, w), BF16)] + [jax.ShapeDtypeStruct(r.shape, BF16) for r in riders],
        scratch_shapes=[pltpu.VMEM((HGRN_HEADS, HGRN_D, HGRN_D), F32)],
        compiler_params=_params("hgrn2", ("parallel", "arbitrary")),
        name="hgrn2",
    )(hg, lb, gn, tril3, *riders)


LOG2E = 1.4426950408889634
SB_SKIP_LOG2 = 160.0
SB_TERMS = 1
SB_NO_TILE = 1e30
SB_LINEAR_ABOVE = 64.0
SB_SKEW = 1
SB_QBLOCKS = 2


def _bf16_terms(x, n, axis=-1):
    terms, r = [], x
    for i in range(n):
        t = r.astype(BF16)
        terms.append(t)
        if i + 1 < n:
            r = r - t.astype(F32)
    return terms[0] if n == 1 else jnp.concatenate(terms, axis=axis)


def _skewed(units, stages):
    for step in range(len(units) + (len(stages) - 1) * SB_SKEW):
        for k, stage in enumerate(stages):
            idx = step - k * SB_SKEW
            if 0 <= idx < len(units):
                stage(units[idx])


def _sb_kernel(q_ref, k_ref, v_ref, o_ref, acc_ref, carry_ref, todo_ref, *, blk, npair):
    half = blk // 2
    nh = 2 * npair
    qblocks = [(pl.program_id(1) * SB_QBLOCKS + j, j * blk) for j in range(SB_QBLOCKS)]
    lane = lax.broadcasted_iota(jnp.int32, (1, LANES), 1)
    first = lane < SB_HEAD_DIM
    t_idx = lax.broadcasted_iota(jnp.int32, (blk, blk), 0)
    s_idx = lax.broadcasted_iota(jnp.int32, (blk, blk), 1)
    causal = s_idx < t_idx
    after = jnp.concatenate([(t_idx > s_idx).astype(BF16)] * SB_TERMS, axis=0)
    after_half = jnp.concatenate([(t_idx > s_idx)[:half, :half].astype(BF16)] * SB_TERMS, axis=0)

    def lanes(p):
        return slice(p * LANES, (p + 1) * LANES)

    def head_queries(rows):
        out = []
        for p in range(npair):
            q = q_ref[0, rows, lanes(p)]
            out += [jnp.where(keep, q, jnp.zeros_like(q)) for keep in (first, jnp.logical_not(first))]
        return out

    def softplus2(z):
        return jnp.where(z > SB_LINEAR_ABOVE, z, jnp.log2(1.0 + jnp.exp2(z)))

    def merge_heads(accs):
        return jnp.concatenate([jnp.where(first, accs[2 * p], accs[2 * p + 1]) for p in range(npair)], axis=-1)

    def windowed():
        units = [(j, hf, p) for j in range(SB_QBLOCKS) for hf in range(2) for p in range(npair)]
        qh, kw, vw = {}, {}, {}
        for j, (i, row0) in enumerate(qblocks):
            for hf in range(2):
                start = pl.multiple_of(jnp.maximum((i - 1) * blk + hf * half, 0), half)
                qs = head_queries(slice(row0 + hf * half, row0 + (hf + 1) * half))
                for p in range(npair):
                    qh[j, hf, p] = jnp.concatenate(qs[2 * p:2 * p + 2], axis=0)
                    kw[j, hf, p] = k_ref[0, pl.ds(start, blk + half), lanes(p)]
                    vw[j, hf, p] = v_ref[0, pl.ds(start, blk + half), lanes(p)]
        own = jnp.concatenate([causal[:half, :half]] * 2, axis=0)
        zs, logb, later, total, pvs = {}, {}, {}, {}, {}

        def logits(u):
            z = lax.dot_general(qh[u], kw[u], _NT, preferred_element_type=F32)
            zs[u] = jnp.concatenate([z[:, :blk], jnp.where(own, z[:, blk:], -SB_NO_TILE)], axis=1)

        def suffix_sums(u):
            z = zs.pop(u)
            sp = softplus2(z)
            sp_old, sp_new = sp[:, :blk], sp[:, blk:]
            later_new = jnp.dot(_bf16_terms(sp_new, SB_TERMS), after_half, preferred_element_type=F32)
            total_new = later_new[:, 0:1] + sp_new[:, 0:1]
            later_old = jnp.dot(_bf16_terms(sp_old, SB_TERMS), after, preferred_element_type=F32)
            later[u] = (later_old, later_new, total_new)
            logb[u] = z - sp
            total[u] = later_old[:, 0:1] + sp_old[:, 0:1] + total_new

        def weigh(u):
            later_old, later_new, total_new = later.pop(u)
            lb = logb.pop(u)
            a = jnp.exp2(jnp.concatenate([lb[:, :blk] - later_old - total_new, lb[:, blk:] - later_new], axis=1))
            pv = jnp.dot(a.astype(BF16), vw[u], preferred_element_type=F32)
            pvs[u] = jnp.where(first, pv[:half], pv[half:])

        _skewed(units, (logits, suffix_sums, weigh))
        o_ref[0] = jnp.concatenate([jnp.concatenate([pvs[j, hf, p] for p in range(npair)], axis=-1)
                                    for j in range(SB_QBLOCKS) for hf in range(2)], axis=0).astype(o_ref.dtype)
        enough = []
        for j, (i, _) in enumerate(qblocks):
            low = [jnp.min(functools.reduce(jnp.minimum, [total[j, hf, p] for p in range(npair)]))
                   for hf in range(2)]
            enough.append(jnp.logical_and(jnp.logical_or(low[0] >= SB_SKIP_LOG2, i == 1),
                                          jnp.logical_and(low[1] >= SB_SKIP_LOG2, i >= 1)))
        return enough

    def sweep(qs, tiles, carries):
        ks, vs = [], []
        for j, _, _ in tiles:
            rows = pl.ds(pl.multiple_of(j * blk, blk), blk)
            ks.append([k_ref[0, rows, lanes(p)] for p in range(npair)])
            vs.append([v_ref[0, rows, lanes(p)] for p in range(npair)])
        units = [(n, h) for n in range(len(tiles)) for h in range(nh)]
        carries, pvs = list(carries), [None] * nh
        zs, logb, later, total = {}, {}, {}, {}

        def logits(u):
            n, h = u
            z = lax.dot_general(qs[h], ks[n][h // 2], _NT, preferred_element_type=F32)
            zs[u] = jnp.where(causal, z, -SB_NO_TILE) if tiles[n][1] else z

        def suffix_sums(u):
            z = zs.pop(u)
            sp = softplus2(z)
            later[u] = jnp.dot(_bf16_terms(sp, SB_TERMS), after, preferred_element_type=F32)
            logb[u] = z - sp
            total[u] = later[u][:, 0:1] + sp[:, 0:1]

        def weigh(u):
            n, h = u
            if tiles[n][2] is not None:
                carries[h] = carries[h] + tiles[n][2]
            loga = logb.pop(u) - later.pop(u)
            a = jnp.exp2(loga if carries[h] is None else loga - carries[h])
            pv = jnp.dot(a.astype(BF16), vs[n][h // 2], preferred_element_type=F32)
            pvs[h] = pv if pvs[h] is None else pvs[h] + pv
            carries[h] = total[u] if carries[h] is None else carries[h] + total[u]

        _skewed(units, (logits, suffix_sums, weigh))
        done = jnp.min(functools.reduce(jnp.minimum, carries)) >= SB_SKIP_LOG2
        return pvs, carries, done

    def general(i, row0):
        out_rows = slice(row0, row0 + blk)
        qs = head_queries(out_rows)
        no_prev = jnp.where(i == 0, jnp.float32(SB_NO_TILE), jnp.float32(0.0))
        accs, carries, done = sweep(qs, [(i, True, None), (jnp.maximum(i - 1, 0), False, no_prev)], [None] * nh)
        o_ref[0, out_rows, :] = merge_heads(accs).astype(o_ref.dtype)
        for h in range(nh):
            acc_ref[h] = accs[h]
            carry_ref[h] = carries[h]

        @pl.when(jnp.logical_and(i >= 2, jnp.logical_not(done)))
        def _():
            def body(st):
                n, _ = st
                pvs, new_carries, done = sweep(qs, [(i - n, False, None)], [carry_ref[h] for h in range(nh)])
                for h in range(nh):
                    acc_ref[h] += pvs[h]
                    carry_ref[h] = new_carries[h]
                return n + 1, done.astype(jnp.int32)

            lax.while_loop(lambda st: jnp.logical_and(st[0] <= i, st[1] == 0), body, (jnp.int32(2), jnp.int32(0)))
            o_ref[0, out_rows, :] = merge_heads([acc_ref[h] for h in range(nh)]).astype(o_ref.dtype)

    for j, enough in enumerate(windowed()):
        todo_ref[j] = jnp.where(enough, 0, 1)
    for j, (i, row0) in enumerate(qblocks):
        pl.when(todo_ref[j] == 1)(functools.partial(general, i, row0))


def _stickbreak(sb, blk):
    b, t, w3 = sb.shape
    w = w3 // 3
    return pl.pallas_call(
        functools.partial(_sb_kernel, blk=blk, npair=w // LANES),
        grid=(b, t // (SB_QBLOCKS * blk)),
        in_specs=[pl.BlockSpec((1, SB_QBLOCKS * blk, w), lambda bi, i: (bi, i, 0)),
                  pl.BlockSpec((1, t, w), lambda bi, i: (bi, 0, 1)),
                  pl.BlockSpec((1, t, w), lambda bi, i: (bi, 0, 2))],
        out_specs=pl.BlockSpec((1, SB_QBLOCKS * blk, w), lambda bi, i: (bi, i, 0)),
        out_shape=jax.ShapeDtypeStruct((b, t, w), BF16),
        scratch_shapes=[pltpu.VMEM((SB_HEADS, blk, LANES), F32), pltpu.VMEM((SB_HEADS, blk, 1), F32),
                        pltpu.SMEM((SB_QBLOCKS,), jnp.int32)],
        compiler_params=_params("stickbreak", ("parallel", "arbitrary")),
        name="stickbreak",
    )(sb, sb, sb)


def _mem_fold_kernel(m_ref, g_ref, wkv_ref, wq_ref, wo_ref, sk_ref, vo_ref):
    d = m_ref.shape[-1]
    dh = d // XATTN_HEADS
    nm = m_ref.shape[1]
    m = _rms(m_ref[0], g_ref[...]).astype(BF16)
    kv = jnp.dot(m, wkv_ref[...].astype(BF16), preferred_element_type=F32).astype(BF16)
    for e in range(XATTN_HEADS):
        hs = slice(e * dh, (e + 1) * dh)
        sk = lax.dot_general(wq_ref[:, hs].astype(BF16), kv[:, hs], _NT, preferred_element_type=F32)
        sk_ref[0, :, e * nm:(e + 1) * nm] = (sk * (dh ** -0.5)).astype(sk_ref.dtype)
        vo = jnp.dot(kv[:, d + e * dh:d + (e + 1) * dh], wo_ref[hs, :].astype(BF16),
                     preferred_element_type=F32)
        vo_ref[0, e * nm:(e + 1) * nm, :] = vo.astype(vo_ref.dtype)


def _mem_fold(mem, g, wkv, wq, wo):
    b, nm, d = mem.shape
    hn = XATTN_HEADS * nm
    return pl.pallas_call(
        _mem_fold_kernel,
        grid=(b,),
        in_specs=[pl.BlockSpec((1, nm, d), lambda bi: (bi, 0, 0)), _resident((1, d)),
                  _resident(wkv.shape), _resident(wq.shape), _resident(wo.shape)],
        out_specs=[pl.BlockSpec((1, d, hn), lambda bi: (bi, 0, 0)),
                   pl.BlockSpec((1, hn, d), lambda bi: (bi, 0, 0))],
        out_shape=[jax.ShapeDtypeStruct((b, d, hn), BF16), jax.ShapeDtypeStruct((b, hn, d), BF16)],
        compiler_params=_params("mem_fold", ("parallel",)),
        name="mem_fold",
    )(mem, g, wkv, wq, wo)


def _mix_stage(x, ya, yb, g_ref, wg_ref, wa_ref, wb_ref, wo_ref):
    d = x.shape[-1]
    h = _rms(x, g_ref[...]).astype(BF16)
    gate = _sigmoid(jnp.dot(h, wg_ref[...], preferred_element_type=F32))
    pa = jnp.dot(ya, wa_ref[...], preferred_element_type=F32)
    pb = jnp.dot(yb, wb_ref[...], preferred_element_type=F32)
    merged = gate[:, :d] * pa + gate[:, d:] * pb
    return x + jnp.dot(merged.astype(BF16), wo_ref[...], preferred_element_type=F32)


def _xattn_stage(x, g_ref, sk_ref, vo_ref):
    nm = sk_ref.shape[-1] // XATTN_HEADS
    h = _rms(x, g_ref[...]).astype(BF16)
    s = jnp.dot(h, sk_ref[0], preferred_element_type=F32)
    probs = []
    for e in range(XATTN_HEADS):
        s_e = s[:, e * nm:(e + 1) * nm]
        p = jnp.exp(s_e - jnp.max(s_e, axis=-1, keepdims=True))
        probs.append((p / jnp.sum(p, axis=-1, keepdims=True)).astype(BF16))
    return x + jnp.dot(jnp.concatenate(probs, axis=-1), vo_ref[0], preferred_element_type=F32)


def _ffn_stage(x, g_ref, wi_ref, wo_ref):
    dff = wo_ref.shape[0]
    h = _rms(x, g_ref[...]).astype(BF16)
    gate = jnp.dot(h, wi_ref[:, :dff], preferred_element_type=F32)
    up = jnp.dot(h, wi_ref[:, dff:], preferred_element_type=F32)
    act = (gate * _sigmoid(gate) * up).astype(BF16)
    return x + jnp.dot(act, wo_ref[...], preferred_element_type=F32)


def _post_kernel(x_ref, ya_ref, yb_ref, gm_ref, wg_ref, wa_ref, wb_ref, wout_ref, gx_ref, sk_ref, vo_ref,
                 gf_ref, wi_ref, wo_ref, gfin_ref, o_ref, *, sub, final_norm):
    def sub_block(r):
        rows = slice(r, r + sub)
        x = _mix_stage(x_ref[0, rows, :], ya_ref[0, rows, :], yb_ref[0, rows, :], gm_ref, wg_ref, wa_ref,
                       wb_ref, wout_ref)
        yield
        x = _xattn_stage(x, gx_ref, sk_ref, vo_ref)
        yield
        x = _ffn_stage(x, gf_ref, wi_ref, wo_ref)
        o_ref[0, rows, :] = _rms(x, gfin_ref[...]) if final_norm else x

    blocks = [sub_block(r) for r in range(0, x_ref.shape[1], sub)]
    for step in range(len(blocks) + 2):
        for stage in range(3):
            n = step - stage
            if 0 <= n < len(blocks):
                next(blocks[n], None)


def _post(x3d, ya, yb, gm, wg, wa, wb, wout, gx, sk, vo, gf, wi, wo, gfin, final_norm, tm, sub):
    b, t, d = x3d.shape
    rows = lambda n: pl.BlockSpec((1, tm, n), lambda bi, i: (bi, i, 0))
    per_batch = lambda a: pl.BlockSpec((1,) + a.shape[1:], lambda bi, i: (bi, 0, 0))
    return pl.pallas_call(
        functools.partial(_post_kernel, sub=sub, final_norm=final_norm),
        grid=(b, t // tm),
        in_specs=[rows(d), rows(ya.shape[-1]), rows(yb.shape[-1]), _resident((1, d)), _resident(wg.shape),
                  _resident(wa.shape), _resident(wb.shape), _resident(wout.shape), _resident((1, d)),
                  per_batch(sk), per_batch(vo), _resident((1, d)), _resident(wi.shape), _resident(wo.shape),
                  _resident((1, d))],
        out_specs=rows(d),
        out_shape=jax.ShapeDtypeStruct((b, t, d), F32),
        compiler_params=_params("post", ("parallel", "parallel")),
        name="post",
    )(x3d, ya, yb, gm, wg, wa, wb, wout, gx, sk, vo, gf, wi, wo, gfin)


def _tile(n, want):
    return want if n % want == 0 else n


def kernel(x, mem, g_mix, w_in, lb_table, g_hgrn, w_gate, w_proj_a, w_proj_b, w_out, g_xattn, g_mem,
           w_xq, w_xkv, w_xo, g_ffn, w_ffn_in, w_ffn_out, g_final):
    b, t, d = x.shape
    depth = w_in.shape[0]
    hq = HGRN_HEADS * HGRN_D
    sbw = SB_HEADS * SB_HEAD_DIM
    row = lambda v: v.reshape(1, -1).astype(F32)
    lb_all = jnp.cumsum(jax.nn.softmax(lb_table.astype(F32), axis=0), axis=0)

    sb_scale = jnp.ones((1, 3 * sbw), F32).at[:, :sbw].set(SB_HEAD_DIM ** -0.5 * LOG2E)

    blk_rows, sub_rows = _tile(t, 1024), _tile(t, 512)
    post_rows, post_sub_rows = _tile(t, 512), _tile(t, 256)

    x2d = x.reshape(b * t, d)
    for l in range(depth):
        hg, sb = _in_proj(x2d, row(g_mix[l]), w_in[l], sb_scale, 4 * hq, blk_rows, sub_rows)
        post_w = (w_gate[l], w_proj_a[l], w_proj_b[l], w_out[l], w_ffn_in[l], w_ffn_out[l])
        ya, wg, wa, wb, wout, wi, wo = _hgrn(hg.reshape(b, t, 4 * hq), row(lb_all[l]), row(g_hgrn[l]),
                                             blk_rows, post_w)
        yb = _stickbreak(sb.reshape(b, t, 3 * sbw), _tile(t, SB_BLOCK))
        score_w, out_w = _mem_fold(mem, row(g_mem[l]), w_xkv[l], w_xq[l], w_xo[l])
        x2d = _post(x2d.reshape(b, t, d), ya, yb, row(g_mix[l]), wg, wa, wb, wout,
                    row(g_xattn[l]), score_w, out_w, row(g_ffn[l]), wi, wo, row(g_final), l == depth - 1,
                    post_rows, post_sub_rows).reshape(b * t, d)
    return x2d.reshape(b, t, d)
```

```python
import functools

import jax
import jax.numpy as jnp
import numpy as np
from jax import lax
from jax.experimental import pallas as pl
from jax.experimental.pallas import tpu as pltpu

F32 = jnp.float32
BF16 = jnp.bfloat16
EPS = 1e-6

CHUNK = 64
HGRN_HEADS = 4
HGRN_D = 128
SB_HEADS = 8
SB_HEAD_DIM = 64
XATTN_HEADS = 4

LANES = 128
BF16_SUBLANES = 16
SB_BLOCK = 256

VMEM_LIMIT_MB = dict(in_proj=56, hgrn2=48, stickbreak=56, mem_fold=48, post=58)

_NT = (((1,), (1,)), ((), ()))
_TN = (((0,), (0,)), ((), ()))


def _rms(x, g):
    return x * lax.rsqrt(jnp.mean(x * x, axis=-1, keepdims=True) + EPS) * g


def _sigmoid(x):
    return 1.0 / (1.0 + jnp.exp(-x))


def _params(name, sem):
    return pltpu.CompilerParams(dimension_semantics=sem, vmem_limit_bytes=VMEM_LIMIT_MB[name] * 1024 * 1024)


def _resident(shape):
    nd = len(shape)
    return pl.BlockSpec(shape, lambda *_: (0,) * nd, pipeline_mode=pl.Buffered(1))


def _in_proj_kernel(x_ref, g_ref, w_ref, scale_ref, hg_ref, sb_ref, *, sub):
    n_hg = hg_ref.shape[-1]
    w_hg = w_ref[:, :n_hg].astype(BF16)
    w_sb = w_ref[:, n_hg:].astype(BF16)
    for r in range(0, x_ref.shape[0], sub):
        h = _rms(x_ref[r:r + sub, :], g_ref[...]).astype(BF16)
        hg_ref[r:r + sub, :] = jnp.dot(h, w_hg, preferred_element_type=F32)
        sb = jnp.dot(h, w_sb, preferred_element_type=F32) * scale_ref[...]
        sb_ref[r:r + sub, :] = sb.astype(sb_ref.dtype)


def _in_proj(x2d, g, w, sb_scale, n_hg, tm, sub):
    m, d = x2d.shape
    n = w.shape[1]
    return pl.pallas_call(
        functools.partial(_in_proj_kernel, sub=sub),
        grid=(m // tm,),
        in_specs=[pl.BlockSpec((tm, d), lambda i: (i, 0)), _resident((1, d)), _resident((d, n)),
                  _resident((1, n - n_hg))],
        out_specs=[pl.BlockSpec((tm, n_hg), lambda i: (i, 0)), pl.BlockSpec((tm, n - n_hg), lambda i: (i, 0))],
        out_shape=[jax.ShapeDtypeStruct((m, n_hg), F32), jax.ShapeDtypeStruct((m, n - n_hg), BF16)],
        compiler_params=_params("in_proj", ("parallel",)),
        name="in_proj",
    )(x2d, g, w, sb_scale)


HGRN_LEVELS = (32, 16, 8)
HGRN_DIAG = 8


def _hgrn_masks():
    c = CHUNK
    t = lax.broadcasted_iota(jnp.int32, (c, c), 0)
    s = lax.broadcasted_iota(jnp.int32, (c, c), 1)
    masks = []
    for m in HGRN_LEVELS:
        same = (t // (2 * m)) == (s // (2 * m))
        masks.append(same & ((t % (2 * m)) >= m) & ((s % (2 * m)) < m))
    masks.append(((t // HGRN_DIAG) == (s // HGRN_DIAG)) & (s <= t))
    return masks


HGRN_CUMSUM_TERMS = 2
HGRN_GROUP = 1


def _silu(x):
    hx = 0.5 * x
    return hx + hx * jnp.tanh(hx)


def _hgrn_kernel(hg_ref, lb_ref, gn_ref, tril_ref, *refs, n_chunks, group, n_riders):
    c, nh, dh = CHUNK, HGRN_HEADS, HGRN_D
    w = nh * dh
    rider_in, o_ref, rider_out, st_ref = refs[:n_riders], refs[n_riders], refs[n_riders + 1:-1], refs[-1]
    for src, dst in zip(rider_in, rider_out):
        dst[...] = src[...].astype(dst.dtype)

    @pl.when(pl.program_id(1) == 0)
    def _():
        st_ref[...] = jnp.zeros_like(st_ref)

    masks = _hgrn_masks()
    states = [st_ref[h] for h in range(nh)]
    groups = [_hgrn_group(hg_ref, lb_ref, gn_ref, tril_ref, o_ref, states, masks, c0, group)
              for c0 in range(0, n_chunks, group)]
    for step in range(len(groups) + 2):
        for phase in range(3):
            g = step - phase
            if 0 <= g < len(groups):
                next(groups[g], None)
    for h in range(nh):
        st_ref[h] = states[h]


def _hgrn_group(hg_ref, lb_ref, gn_ref, tril_ref, o_ref, states, masks, c0, n):
    c, nh, dh = CHUNK, HGRN_HEADS, HGRN_D
    w = nh * dh

    def wide(col):
        return jnp.concatenate(
            [hg_ref[0, (c0 + ci) * c:(c0 + ci + 1) * c, col * w:(col + 1) * w] for ci in range(n)], axis=1)

    def tiled(row):
        return jnp.concatenate([row] * n, axis=1)

    qp, fp, iv, gp = wide(0), wide(1), wide(2), wide(3)
    lb = tiled(lb_ref[...])
    q = _silu(qp)
    f = 0.5 * (1.0 + lb) + (0.5 * (1.0 - lb)) * jnp.tanh(0.5 * fp)
    k = 1.0 - f
    b = jnp.dot(tril_ref[...], _bf16_terms(jnp.log(f) * LOG2E, HGRN_CUMSUM_TERMS, axis=0),
                preferred_element_type=F32)

    units = [(ci, h) for ci in range(n) for h in range(nh)]
    lanes = {u: slice(u[0] * w + u[1] * dh, u[0] * w + (u[1] + 1) * dh) for u in units}
    e_b = jnp.exp2(b)
    e_last = e_b[c - 1:c, :]
    q_in = (q * e_b).astype(BF16)
    k_out = (k * jnp.exp2(b[c - 1:c, :] - b)).astype(BF16)
    v16 = iv.astype(BF16)
    sides = []
    for m in HGRN_LEVELS:
        decay, qk = [], []
        for s0 in range(0, c, 2 * m):
            bp = b[s0 + m:s0 + m + 1, :]
            decay += [bp - b[s0:s0 + m, :], b[s0 + m:s0 + 2 * m, :] - bp]
            qk += [k[s0:s0 + m, :], q[s0 + m:s0 + 2 * m, :]]
        both = (jnp.concatenate(qk, axis=0) * jnp.exp2(jnp.concatenate(decay, axis=0))).astype(BF16)
        sides.append((both, both))
    half = HGRN_DIAG // 2
    dd = jnp.concatenate([b[s0:s0 + HGRN_DIAG, :] - b[s0 + half:s0 + half + 1, :]
                          for s0 in range(0, c, HGRN_DIAG)], axis=0)
    sides.append(((q * jnp.exp2(dd)).astype(BF16), (k * jnp.exp2(-dd)).astype(BF16)))
    gate = tiled(gn_ref[...]) * _silu(gp)
    yield

    kv = {u: lax.dot_general(v16[:, lanes[u]], k_out[:, lanes[u]], _TN, preferred_element_type=F32)
          for u in units}
    inter = {}
    for h in range(nh):
        for ci in range(n):
            u = (ci, h)
            inter[u] = lax.dot_general(q_in[:, lanes[u]], states[h].astype(BF16), _NT,
                                       preferred_element_type=F32)
            states[h] = e_last[:, lanes[u]] * states[h] + kv[u]
    parts = {u: [lax.dot_general(a_l[:, lanes[u]], b_l[:, lanes[u]], _NT, preferred_element_type=F32)
                 for a_l, b_l in sides] for u in units}
    yield

    scores = {}
    for u in units:
        s_u = jnp.zeros((c, c), F32)
        for s_l, m_l in zip(parts[u], masks):
            s_u = jnp.where(m_l, s_l, s_u)
        scores[u] = s_u.astype(BF16)
    for ci in range(n):
        outs = []
        for h in range(nh):
            u = (ci, h)
            o = jnp.dot(scores[u], v16[:, lanes[u]], preferred_element_type=F32) + inter[u]
            outs.append(o * lax.rsqrt(jnp.mean(o * o, axis=-1, keepdims=True) + EPS))
        on = jnp.concatenate(outs, axis=-1) * gate[:, ci * w:(ci + 1) * w]
        o_ref[0, (c0 + ci) * c:(c0 + ci + 1) * c, :] = on.astype(o_ref.dtype)


def _hgrn(hg, lb, gn, rows_per_step, riders):
    b, t, w4 = hg.shape
    w = w4 // 4
    nsteps = t // rows_per_step
    tril = np.tril(np.ones((CHUNK, CHUNK), np.float32))
    tril3 = jnp.asarray(np.concatenate([tril] * HGRN_CUMSUM_TERMS, axis=1), BF16)
    slices = [r.shape[0] // (b * nsteps) for r in riders]
    assert all(r.shape[0] == s * b * nsteps and s % BF16_SUBLANES == 0 for r, s in zip(riders, slices))
    rider_specs = [pl.BlockSpec((s, r.shape[1]), lambda bi, i: (bi * nsteps + i, 0)) for r, s in zip(riders, slices)]
    return pl.pallas_call(
        functools.partial(_hgrn_kernel, n_chunks=rows_per_step // CHUNK, group=HGRN_GROUP, n_riders=len(riders)),
        grid=(b, nsteps),
        in_specs=[pl.BlockSpec((1, rows_per_step, w4), lambda bi, i: (bi, i, 0)),
                  _resident((1, w)), _resident((1, w)), _resident(tril3.shape)] + rider_specs,
        out_specs=[pl.BlockSpec((1, rows_per_step, w), lambda bi, i: (bi, i, 0))] + rider_specs,
        out_shape=[jax.ShapeDtypeStruct((b, t, w), BF16)] + [jax.ShapeDtypeStruct(r.shape, BF16) for r in riders],
        scratch_shapes=[pltpu.VMEM((HGRN_HEADS, HGRN_D, HGRN_D), F32)],
        compiler_params=_params("hgrn2", ("parallel", "arbitrary")),
        name="hgrn2",
    )(hg, lb, gn, tril3, *riders)


LOG2E = 1.4426950408889634
SB_SKIP_LOG2 = 160.0
SB_TERMS = 1
SB_NO_TILE = 1e30
SB_LINEAR_ABOVE = 64.0
SB_SKEW = 1
SB_QBLOCKS = 2


def _bf16_terms(x, n, axis=-1):
    terms, r = [], x
    for i in range(n):
        t = r.astype(BF16)
        terms.append(t)
        if i + 1 < n:
            r = r - t.astype(F32)
    return terms[0] if n == 1 else jnp.concatenate(terms, axis=axis)


def _skewed(units, stages):
    for step in range(len(units) + (len(stages) - 1) * SB_SKEW):
        for k, stage in enumerate(stages):
            idx = step - k * SB_SKEW
            if 0 <= idx < len(units):
                stage(units[idx])


def _sb_kernel(q_ref, k_ref, v_ref, o_ref, acc_ref, carry_ref, todo_ref, *, blk, npair):
    half = blk // 2
    nh = 2 * npair
    qblocks = [(pl.program_id(1) * SB_QBLOCKS + j, j * blk) for j in range(SB_QBLOCKS)]
    lane = lax.broadcasted_iota(jnp.int32, (1, LANES), 1)
    first = lane < SB_HEAD_DIM
    t_idx = lax.broadcasted_iota(jnp.int32, (blk, blk), 0)
    s_idx = lax.broadcasted_iota(jnp.int32, (blk, blk), 1)
    causal = s_idx < t_idx
    after = jnp.concatenate([(t_idx > s_idx).astype(BF16)] * SB_TERMS, axis=0)
    after_half = jnp.concatenate([(t_idx > s_idx)[:half, :half].astype(BF16)] * SB_TERMS, axis=0)

    def lanes(p):
        return slice(p * LANES, (p + 1) * LANES)

    def head_queries(rows):
        out = []
        for p in range(npair):
            q = q_ref[0, rows, lanes(p)]
            out += [jnp.where(keep, q, jnp.zeros_like(q)) for keep in (first, jnp.logical_not(first))]
        return out

    def softplus2(z):
        return jnp.where(z > SB_LINEAR_ABOVE, z, jnp.log2(1.0 + jnp.exp2(z)))

    def merge_heads(accs):
        return jnp.concatenate([jnp.where(first, accs[2 * p], accs[2 * p + 1]) for p in range(npair)], axis=-1)

    def windowed():
        units = [(j, hf, p) for j in range(SB_QBLOCKS) for hf in range(2) for p in range(npair)]
        qh, kw, vw, keep = {}, {}, {}, {}
        row_in_half = jnp.concatenate([lax.broadcasted_iota(jnp.int32, (half, blk + half), 0)] * 2, axis=0)
        col_minus_row = lax.broadcasted_iota(jnp.int32, (2 * half, blk + half), 1) - row_in_half
        for j, (i, row0) in enumerate(qblocks):
            for hf in range(2):
                want = (i - 1) * blk + hf * half
                start = pl.multiple_of(jnp.maximum(want, 0), half)
                keep[j, hf] = (col_minus_row < blk - (start - want)) if j == 0 else None
                qs = head_queries(slice(row0 + hf * half, row0 + (hf + 1) * half))
                for p in range(npair):
                    qh[j, hf, p] = jnp.concatenate(qs[2 * p:2 * p + 2], axis=0)
                    kw[j, hf, p] = k_ref[0, pl.ds(start, blk + half), lanes(p)]
                    vw[j, hf, p] = v_ref[0, pl.ds(start, blk + half), lanes(p)]
        own = jnp.concatenate([causal[:half, :half]] * 2, axis=0)
        zs, logb, later, total, pvs = {}, {}, {}, {}, {}

        def logits(u):
            z = lax.dot_general(qh[u], kw[u], _NT, preferred_element_type=F32)
            if keep[u[:2]] is not None:
                zs[u] = jnp.where(keep[u[:2]], z, -SB_NO_TILE)
            else:
                zs[u] = jnp.concatenate([z[:, :blk], jnp.where(own, z[:, blk:], -SB_NO_TILE)], axis=1)

        def suffix_sums(u):
            z = zs.pop(u)
            sp = softplus2(z)
            sp_old, sp_new = sp[:, :blk], sp[:, blk:]
            later_new = jnp.dot(_bf16_terms(sp_new, SB_TERMS), after_half, preferred_element_type=F32)
            total_new = later_new[:, 0:1] + sp_new[:, 0:1]
            later_old = jnp.dot(_bf16_terms(sp_old, SB_TERMS), after, preferred_element_type=F32)
            later[u] = (later_old, later_new, total_new)
            logb[u] = z - sp
            total[u] = later_old[:, 0:1] + sp_old[:, 0:1] + total_new

        def weigh(u):
            later_old, later_new, total_new = later.pop(u)
            lb = logb.pop(u)
            a = jnp.exp2(jnp.concatenate([lb[:, :blk] - later_old - total_new, lb[:, blk:] - later_new], axis=1))
            pv = jnp.dot(a.astype(BF16), vw[u], preferred_element_type=F32)
            pvs[u] = jnp.where(first, pv[:half], pv[half:])

        _skewed(units, (logits, suffix_sums, weigh))
        o_ref[0] = jnp.concatenate([jnp.concatenate([pvs[j, hf, p] for p in range(npair)], axis=-1)
                                    for j in range(SB_QBLOCKS) for hf in range(2)], axis=0).astype(o_ref.dtype)
        enough = []
        for j, (i, _) in enumerate(qblocks):
            low = [jnp.min(functools.reduce(jnp.minimum, [total[j, hf, p] for p in range(npair)]))
                   for hf in range(2)]
            enough.append(jnp.logical_and(jnp.logical_or(low[0] >= SB_SKIP_LOG2, i <= 1),
                                          jnp.logical_or(low[1] >= SB_SKIP_LOG2, i == 0)))
        return enough

    def sweep(qs, tiles, carries):
        ks, vs = [], []
        for j, _, _ in tiles:
            rows = pl.ds(pl.multiple_of(j * blk, blk), blk)
            ks.append([k_ref[0, rows, lanes(p)] for p in range(npair)])
            vs.append([v_ref[0, rows, lanes(p)] for p in range(npair)])
        units = [(n, h) for n in range(len(tiles)) for h in range(nh)]
        carries, pvs = list(carries), [None] * nh
        zs, logb, later, total = {}, {}, {}, {}

        def logits(u):
            n, h = u
            z = lax.dot_general(qs[h], ks[n][h // 2], _NT, preferred_element_type=F32)
            zs[u] = jnp.where(causal, z, -SB_NO_TILE) if tiles[n][1] else z

        def suffix_sums(u):
            z = zs.pop(u)
            sp = softplus2(z)
            later[u] = jnp.dot(_bf16_terms(sp, SB_TERMS), after, preferred_element_type=F32)
            logb[u] = z - sp
            total[u] = later[u][:, 0:1] + sp[:, 0:1]

        def weigh(u):
            n, h = u
            if tiles[n][2] is not None:
                carries[h] = carries[h] + tiles[n][2]
            loga = logb.pop(u) - later.pop(u)
            a = jnp.exp2(loga if carries[h] is None else loga - carries[h])
            pv = jnp.dot(a.astype(BF16), vs[n][h // 2], preferred_element_type=F32)
            pvs[h] = pv if pvs[h] is None else pvs[h] + pv
            carries[h] = total[u] if carries[h] is None else carries[h] + total[u]

        _skewed(units, (logits, suffix_sums, weigh))
        done = jnp.min(functools.reduce(jnp.minimum, carries)) >= SB_SKIP_LOG2
        return pvs, carries, done

    def general(i, row0):
        out_rows = slice(row0, row0 + blk)
        qs = head_queries(out_rows)
        no_prev = jnp.where(i == 0, jnp.float32(SB_NO_TILE), jnp.float32(0.0))
        accs, carries, done = sweep(qs, [(i, True, None), (jnp.maximum(i - 1, 0), False, no_prev)], [None] * nh)
        o_ref[0, out_rows, :] = merge_heads(accs).astype(o_ref.dtype)
        for h in range(nh):
            acc_ref[h] = accs[h]
            carry_ref[h] = carries[h]

        @pl.when(jnp.logical_and(i >= 2, jnp.logical_not(done)))
        def _():
            def body(st):
                n, _ = st
                pvs, new_carries, done = sweep(qs, [(i - n, False, None)], [carry_ref[h] for h in range(nh)])
                for h in range(nh):
                    acc_ref[h] += pvs[h]
                    carry_ref[h] = new_carries[h]
                return n + 1, done.astype(jnp.int32)

            lax.while_loop(lambda st: jnp.logical_and(st[0] <= i, st[1] == 0), body, (jnp.int32(2), jnp.int32(0)))
            o_ref[0, out_rows, :] = merge_heads([acc_ref[h] for h in range(nh)]).astype(o_ref.dtype)

    for j, enough in enumerate(windowed()):
        todo_ref[j] = jnp.where(enough, 0, 1)
    for j, (i, row0) in enumerate(qblocks):
        pl.when(todo_ref[j] == 1)(functools.partial(general, i, row0))


def _stickbreak(sb, blk):
    b, t, w3 = sb.shape
    w = w3 // 3
    return pl.pallas_call(
        functools.partial(_sb_kernel, blk=blk, npair=w // LANES),
        grid=(b, t // (SB_QBLOCKS * blk)),
        in_specs=[pl.BlockSpec((1, SB_QBLOCKS * blk, w), lambda bi, i: (bi, i, 0)),
                  pl.BlockSpec((1, t, w), lambda bi, i: (bi, 0, 1)),
                  pl.BlockSpec((1, t, w), lambda bi, i: (bi, 0, 2))],
        out_specs=pl.BlockSpec((1, SB_QBLOCKS * blk, w), lambda bi, i: (bi, i, 0)),
        out_shape=jax.ShapeDtypeStruct((b, t, w), BF16),
        scratch_shapes=[pltpu.VMEM((SB_HEADS, blk, LANES), F32), pltpu.VMEM((SB_HEADS, blk, 1), F32),
                        pltpu.SMEM((SB_QBLOCKS,), jnp.int32)],
        compiler_params=_params("stickbreak", ("parallel", "arbitrary")),
        name="stickbreak",
    )(sb, sb, sb)


def _mem_fold_kernel(m_ref, g_ref, wkv_ref, wq_ref, wo_ref, sk_ref, vo_ref):
    d = m_ref.shape[-1]
    dh = d // XATTN_HEADS
    nm = m_ref.shape[1]
    m = _rms(m_ref[0], g_ref[...]).astype(BF16)
    kv = jnp.dot(m, wkv_ref[...].astype(BF16), preferred_element_type=F32).astype(BF16)
    for e in range(XATTN_HEADS):
        hs = slice(e * dh, (e + 1) * dh)
        sk = lax.dot_general(wq_ref[:, hs].astype(BF16), kv[:, hs], _NT, preferred_element_type=F32)
        sk_ref[0, :, e * nm:(e + 1) * nm] = (sk * (dh ** -0.5)).astype(sk_ref.dtype)
        vo = jnp.dot(kv[:, d + e * dh:d + (e + 1) * dh], wo_ref[hs, :].astype(BF16),
                     preferred_element_type=F32)
        vo_ref[0, e * nm:(e + 1) * nm, :] = vo.astype(vo_ref.dtype)


def _mem_fold(mem, g, wkv, wq, wo):
    b, nm, d = mem.shape
    hn = XATTN_HEADS * nm
    return pl.pallas_call(
        _mem_fold_kernel,
        grid=(b,),
        in_specs=[pl.BlockSpec((1, nm, d), lambda bi: (bi, 0, 0)), _resident((1, d)),
                  _resident(wkv.shape), _resident(wq.shape), _resident(wo.shape)],
        out_specs=[pl.BlockSpec((1, d, hn), lambda bi: (bi, 0, 0)),
                   pl.BlockSpec((1, hn, d), lambda bi: (bi, 0, 0))],
        out_shape=[jax.ShapeDtypeStruct((b, d, hn), BF16), jax.ShapeDtypeStruct((b, hn, d), BF16)],
        compiler_params=_params("mem_fold", ("parallel",)),
        name="mem_fold",
    )(mem, g, wkv, wq, wo)


def _mix_stage(x, ya, yb, g_ref, wg_ref, wa_ref, wb_ref, wo_ref):
    d = x.shape[-1]
    h = _rms(x, g_ref[...]).astype(BF16)
    gate = _sigmoid(jnp.dot(h, wg_ref[...], preferred_element_type=F32))
    pa = jnp.dot(ya, wa_ref[...], preferred_element_type=F32)
    pb = jnp.dot(yb, wb_ref[...], preferred_element_type=F32)
    merged = gate[:, :d] * pa + gate[:, d:] * pb
    return x + jnp.dot(merged.astype(BF16), wo_ref[...], preferred_element_type=F32)


def _xattn_stage(x, g_ref, sk_ref, vo_ref):
    nm = sk_ref.shape[-1] // XATTN_HEADS
    h = _rms(x, g_ref[...]).astype(BF16)
    s = jnp.dot(h, sk_ref[0], preferred_element_type=F32)
    probs = []
    for e in range(XATTN_HEADS):
        s_e = s[:, e * nm:(e + 1) * nm]
        p = jnp.exp(s_e - jnp.max(s_e, axis=-1, keepdims=True))
        probs.append((p / jnp.sum(p, axis=-1, keepdims=True)).astype(BF16))
    return x + jnp.dot(jnp.concatenate(probs, axis=-1), vo_ref[0], preferred_element_type=F32)


def _ffn_stage(x, g_ref, wi_ref, wo_ref):
    dff = wo_ref.shape[0]
    h = _rms(x, g_ref[...]).astype(BF16)
    gate = jnp.dot(h, wi_ref[:, :dff], preferred_element_type=F32)
    up = jnp.dot(h, wi_ref[:, dff:], preferred_element_type=F32)
    act = (gate * _sigmoid(gate) * up).astype(BF16)
    return x + jnp.dot(act, wo_ref[...], preferred_element_type=F32)


def _post_kernel(x_ref, ya_ref, yb_ref, gm_ref, wg_ref, wa_ref, wb_ref, wout_ref, gx_ref, sk_ref, vo_ref,
                 gf_ref, wi_ref, wo_ref, gfin_ref, o_ref, *, sub, final_norm):
    def sub_block(r):
        rows = slice(r, r + sub)
        x = _mix_stage(x_ref[0, rows, :], ya_ref[0, rows, :], yb_ref[0, rows, :], gm_ref, wg_ref, wa_ref,
                       wb_ref, wout_ref)
        yield
        x = _xattn_stage(x, gx_ref, sk_ref, vo_ref)
        yield
        x = _ffn_stage(x, gf_ref, wi_ref, wo_ref)
        o_ref[0, rows, :] = _rms(x, gfin_ref[...]) if final_norm else x

    blocks = [sub_block(r) for r in range(0, x_ref.shape[1], sub)]
    for step in range(len(blocks) + 2):
        for stage in range(3):
            n = step - stage
            if 0 <= n < len(blocks):
                next(blocks[n], None)


def _post(x3d, ya, yb, gm, wg, wa, wb, wout, gx, sk, vo, gf, wi, wo, gfin, final_norm, tm, sub):
    b, t, d = x3d.shape
    rows = lambda n: pl.BlockSpec((1, tm, n), lambda bi, i: (bi, i, 0))
    per_batch = lambda a: pl.BlockSpec((1,) + a.shape[1:], lambda bi, i: (bi, 0, 0))
    return pl.pallas_call(
        functools.partial(_post_kernel, sub=sub, final_norm=final_norm),
        grid=(b, t // tm),
        in_specs=[rows(d), rows(ya.shape[-1]), rows(yb.shape[-1]), _resident((1, d)), _resident(wg.shape),
                  _resident(wa.shape), _resident(wb.shape), _resident(wout.shape), _resident((1, d)),
                  per_batch(sk), per_batch(vo), _resident((1, d)), _resident(wi.shape), _resident(wo.shape),
                  _resident((1, d))],
        out_specs=rows(d),
        out_shape=jax.ShapeDtypeStruct((b, t, d), F32),
        compiler_params=_params("post", ("parallel", "parallel")),
        name="post",
    )(x3d, ya, yb, gm, wg, wa, wb, wout, gx, sk, vo, gf, wi, wo, gfin)


def _tile(n, want):
    return want if n % want == 0 else n


def kernel(x, mem, g_mix, w_in, lb_table, g_hgrn, w_gate, w_proj_a, w_proj_b, w_out, g_xattn, g_mem,
           w_xq, w_xkv, w_xo, g_ffn, w_ffn_in, w_ffn_out, g_final):
    b, t, d = x.shape
    depth = w_in.shape[0]
    hq = HGRN_HEADS * HGRN_D
    sbw = SB_HEADS * SB_HEAD_DIM
    row = lambda v: v.reshape(1, -1).astype(F32)
    lb_all = jnp.cumsum(jax.nn.softmax(lb_table.astype(F32), axis=0), axis=0)

    sb_scale = jnp.ones((1, 3 * sbw), F32).at[:, :sbw].set(SB_HEAD_DIM ** -0.5 * LOG2E)

    blk_rows, sub_rows = _tile(t, 1024), _tile(t, 512)
    post_rows, post_sub_rows = _tile(t, 512), _tile(t, 256)

    x2d = x.reshape(b * t, d)
    for l in range(depth):
        hg, sb = _in_proj(x2d, row(g_mix[l]), w_in[l], sb_scale, 4 * hq, blk_rows, sub_rows)
        post_w = (w_gate[l], w_proj_a[l], w_proj_b[l], w_out[l], w_ffn_in[l], w_ffn_out[l])
        ya, wg, wa, wb, wout, wi, wo = _hgrn(hg.reshape(b, t, 4 * hq), row(lb_all[l]), row(g_hgrn[l]),
                                             blk_rows, post_w)
        yb = _stickbreak(sb.reshape(b, t, 3 * sbw), _tile(t, SB_BLOCK))
        score_w, out_w = _mem_fold(mem, row(g_mem[l]), w_xkv[l], w_xq[l], w_xo[l])
        x2d = _post(x2d.reshape(b, t, d), ya, yb, row(g_mix[l]), wg, wa, wb, wout,
                    row(g_xattn[l]), score_w, out_w, row(g_ffn[l]), wi, wo, row(g_final), l == depth - 1,
                    post_rows, post_sub_rows).reshape(b * t, d)
    return x2d.reshape(b, t, d)
```
